```python
import jax, jax.numpy as jnp
from jax import lax
import numpy as np

D_MODEL = 2048
BATCH = 1
SEQ = 8192
DEPTH = 1
DEC_BATCH = 128
DEC_SEQ = 4
PAST_LEN = 2048
PAGE_SIZE = 128

HG_HEADS = 8
HG_DK = 128
HG_DV = 128
HG_WIDTH = HG_HEADS * HG_DK
HG_VWIDTH = HG_HEADS * HG_DV
HG_CHUNK = 64
NSA_HEADS = 16
NSA_KV_HEADS = 2
NSA_GROUP = NSA_HEADS // NSA_KV_HEADS
HEAD_DIM = 64
NSA_WIDTH = NSA_HEADS * HEAD_DIM
KV_WIDTH = NSA_KV_HEADS * HEAD_DIM
CMP_BLOCK = 32
CMP_STRIDE = 16
CMP_HIDDEN = 256
SLC_BLOCK = 64
SLC_TOP = 16
WINDOW = 512
Q_BLOCK = 128
FORCE_SCORE = 1.0e4
D_FF = 5504
EPS = 1e-6
IN_WIDTHS = (HG_WIDTH, HG_WIDTH, HG_VWIDTH, HG_VWIDTH, NSA_WIDTH,
             KV_WIDTH, KV_WIDTH, KV_WIDTH, KV_WIDTH, KV_WIDTH, KV_WIDTH,
             3 * NSA_HEADS, D_MODEL, D_MODEL)
D_IN = sum(IN_WIDTHS)

kernel_name = 'hybrid_hgrn2_nsa_macaron_step'


def rmsnorm(x, g):
    xf = x.astype(jnp.float32)
    y = xf * lax.rsqrt(jnp.mean(xf * xf, axis=-1, keepdims=True) + EPS)
    return (y * g.astype(jnp.float32)).astype(x.dtype)


def half_ffn(x, n_pre, n_post, w_gate, w_up, w_down):
    h = rmsnorm(x, n_pre)
    y = (jax.nn.silu(h @ w_gate) * (h @ w_up)) @ w_down
    return x + 0.5 * rmsnorm(y, n_post)


def alibi_slopes():
    return jnp.asarray(np.power(2.0, -8.0 * np.arange(1, NSA_HEADS + 1) / NSA_HEADS), dtype=jnp.float32)


def in_project(h, w_in):
    points = np.cumsum(IN_WIDTHS)[:-1].tolist()
    parts = jnp.split(h @ w_in, points, axis=-1)
    hg_in = (parts[0], parts[1], parts[2], parts[3])
    nsa_in = (parts[4], (parts[5], parts[6], parts[7], parts[8], parts[9], parts[10]), parts[11])
    return hg_in, nsa_in, parts[12], parts[13]


def hgrn2_scan(q, logf, k, v, s0):
    B, T, H, _ = q.shape
    C = min(HG_CHUNK, T)
    n_chunks = -(-T // C)
    pad = n_chunks * C - T

    def prep(a):
        a = jnp.pad(a.astype(jnp.float32), ((0, 0), (0, pad), (0, 0), (0, 0)))
        return a.reshape(B, n_chunks, C, H, a.shape[-1]).transpose(1, 0, 2, 3, 4)

    qc, fc, kc, vc = prep(q), prep(logf), prep(k), prep(v)
    causal = jnp.tril(jnp.ones((C, C), dtype=bool))[None, :, :, None, None]

    def step(S, inp):
        qi, fi, ki, vi = inp
        b = jnp.cumsum(fi, axis=1)
        o_inter = jnp.einsum('bchk,bhkv->bchv', qi * jnp.exp(b), S)
        diff = b[:, :, None] - b[:, None, :]
        decay = jnp.exp(jnp.where(causal, diff, -jnp.inf))
        A = jnp.einsum('bthk,bshk,btshk->bhts', qi, ki, decay)
        o_intra = jnp.einsum('bhts,bshv->bthv', A, vi)
        b_last = b[:, -1]
        k_dec = ki * jnp.exp(b_last[:, None] - b)
        S_new = jnp.exp(b_last)[..., None] * S + jnp.einsum('bshk,bshv->bhkv', k_dec, vi)
        return S_new, o_inter + o_intra

    S_fin, o = lax.scan(step, s0.astype(jnp.float32), (qc, fc, kc, vc))
    o = o.transpose(1, 0, 2, 3, 4).reshape(B, n_chunks * C, H, HG_DV)[:, :T]
    return o, S_fin


def hgrn2_branch(q_raw, f_raw, i_raw, g_raw, lb, g_norm, s0):
    B, T, _ = q_raw.shape
    f = lb + (1.0 - lb) * jax.nn.sigmoid(f_raw.astype(jnp.float32))
    logf = jnp.log(f)
    k = 1.0 - f
    q = jax.nn.silu(q_raw.astype(jnp.float32))
    heads = lambda a: a.reshape(B, T, HG_HEADS, -1)
    o, S = hgrn2_scan(heads(q), heads(logf), heads(k), heads(i_raw), s0)
    o = rmsnorm(o, g_norm) * jax.nn.silu(heads(g_raw).astype(jnp.float32))
    return o.reshape(B, T, HG_VWIDTH).astype(q_raw.dtype), S


def nsa_heads(q_raw, kv_raws, g_raw):
    B, T, _ = q_raw.shape
    q = q_raw.reshape(B, T, NSA_HEADS, HEAD_DIM) * (HEAD_DIM ** -0.5)
    kvs = tuple(a.reshape(B, T, NSA_KV_HEADS, HEAD_DIM) for a in kv_raws)
    gates = jax.nn.sigmoid(g_raw.astype(jnp.float32)).reshape(B, T, NSA_HEADS, 3)
    return q, gates, kvs


def compress(k, pos_emb, w1, w2):
    B, L, N, D = k.shape
    nc = (L - CMP_BLOCK) // CMP_STRIDE + 1
    idx = jnp.arange(nc)[:, None] * CMP_STRIDE + jnp.arange(CMP_BLOCK)[None, :]
    blocks = k[:, idx] + pos_emb[None, None, :, None, :]
    flat = blocks.transpose(0, 1, 3, 2, 4).reshape(B, nc, N, CMP_BLOCK * D)
    return jax.nn.gelu(flat @ w1) @ w2


def to_blocks(k):
    B, L, N, D = k.shape
    ns = -(-L // SLC_BLOCK)
    k = jnp.pad(k, ((0, 0), (0, ns * SLC_BLOCK - L), (0, 0), (0, 0)))
    return k.reshape(B, ns, SLC_BLOCK, N, D).transpose(0, 3, 1, 2, 4)


def overlap_matrix(nc, ns):
    cs = jnp.arange(nc)[:, None] * CMP_STRIDE
    ss = jnp.arange(ns)[None, :] * SLC_BLOCK
    return ((cs <= ss + SLC_BLOCK - 1) & (cs + CMP_BLOCK - 1 >= ss)).astype(jnp.float32)


def masked_attend(s, mask, v, eq):
    s = jnp.where(mask, s, -jnp.inf)
    m = jnp.max(s, axis=-1, keepdims=True)
    m = jnp.where(jnp.isfinite(m), m, 0.0)
    p = jnp.where(mask, jnp.exp(s - m), 0.0)
    p = p / jnp.maximum(jnp.sum(p, axis=-1, keepdims=True), 1e-30)
    return p, jnp.einsum(eq, p, v.astype(jnp.float32))


def nsa_block(q, qpos, gates, kc, vc, cpos, ks_blocks, vs_blocks, kw, vw, wpos):
    B, Tq, H, D = q.shape
    nc = kc.shape[1]
    ns = ks_blocks.shape[2]
    slopes = alibi_slopes().reshape(NSA_KV_HEADS, NSA_GROUP, 1, 1)
    qg = q.astype(jnp.float32).reshape(B, Tq, NSA_KV_HEADS, NSA_GROUP, D)
    dist_c = (qpos[:, None] - cpos[None, :]).astype(jnp.float32)
    s_c = jnp.einsum('btngd,bcnd->bngtc', qg, kc.astype(jnp.float32)) - slopes * dist_c
    p_c, o_c = masked_attend(s_c, dist_c >= 0, vc, 'bngtc,bcnd->btngd')
    score = jnp.einsum('bngtc,cs->bnts', p_c, overlap_matrix(nc, ns))
    blk = jnp.arange(ns)[None, :]
    qblk = (qpos // SLC_BLOCK)[:, None]
    forced = (blk == 0) | (blk == qblk) | (blk == qblk - 1)
    valid = blk * SLC_BLOCK <= qpos[:, None]
    score = jnp.where(valid, jnp.where(forced, FORCE_SCORE, score), -jnp.inf)
    n_top = min(SLC_TOP, ns)
    _, idx = lax.top_k(score, n_top)
    idx_flat = idx.reshape(B, NSA_KV_HEADS, Tq * n_top)
    b_ix = jnp.arange(B)[:, None, None]
    n_ix = jnp.arange(NSA_KV_HEADS)[None, :, None]
    ks = ks_blocks[b_ix, n_ix, idx_flat].reshape(B, NSA_KV_HEADS, Tq, n_top * SLC_BLOCK, D)
    vs = vs_blocks[b_ix, n_ix, idx_flat].reshape(B, NSA_KV_HEADS, Tq, n_top * SLC_BLOCK, D)
    spos = (idx[..., None] * SLC_BLOCK + jnp.arange(SLC_BLOCK)).reshape(B, NSA_KV_HEADS, Tq, n_top * SLC_BLOCK)
    dist_s = (qpos[:, None] - spos).astype(jnp.float32)[:, :, None]
    s_s = jnp.einsum('btngd,bntsd->bngts', qg, ks.astype(jnp.float32)) - slopes * dist_s
    _, o_s = masked_attend(s_s, dist_s >= 0, vs, 'bngts,bntsd->btngd')
    dist_w = qpos[:, None] - wpos[None, :]
    mask_w = (dist_w >= 0) & (dist_w < WINDOW) & (wpos[None, :] >= 0)
    s_w = jnp.einsum('btngd,bwnd->bngtw', qg, kw.astype(jnp.float32)) - slopes * dist_w.astype(jnp.float32)
    _, o_w = masked_attend(s_w, mask_w, vw, 'bngtw,bwnd->btngd')
    o = jnp.stack([o_c, o_s, o_w], axis=-2).reshape(B, Tq, H, 3, D)
    return jnp.einsum('bthr,bthrd->bthd', gates, o)


def nsa_prompt(q, gates, kvs, cmp_k, cmp_v):
    B, T, H, D = q.shape
    k_c, v_c, k_s, v_s, k_w, v_w = kvs
    kc = compress(k_c, *cmp_k)
    vc = compress(v_c, *cmp_v)
    cpos = jnp.arange(kc.shape[1]) * CMP_STRIDE + CMP_BLOCK - 1
    ks_blocks, vs_blocks = to_blocks(k_s), to_blocks(v_s)
    kw_pad = jnp.pad(k_w, ((0, 0), (WINDOW, 0), (0, 0), (0, 0)))
    vw_pad = jnp.pad(v_w, ((0, 0), (WINDOW, 0), (0, 0), (0, 0)))
    nq = T // Q_BLOCK
    qb = q.reshape(B, nq, Q_BLOCK, H, D).swapaxes(0, 1)
    gb = gates.reshape(B, nq, Q_BLOCK, H, 3).swapaxes(0, 1)

    def step(args):
        i, q_i, g_i = args
        start = i * Q_BLOCK
        qpos = start + jnp.arange(Q_BLOCK)
        kw = lax.dynamic_slice_in_dim(kw_pad, start, WINDOW + Q_BLOCK, axis=1)
        vw = lax.dynamic_slice_in_dim(vw_pad, start, WINDOW + Q_BLOCK, axis=1)
        wpos = start - WINDOW + jnp.arange(WINDOW + Q_BLOCK)
        return nsa_block(q_i, qpos, g_i, kc, vc, cpos, ks_blocks, vs_blocks, kw, vw, wpos)

    o = lax.map(step, (jnp.arange(nq), qb, gb))
    o = o.swapaxes(0, 1).reshape(B, T, NSA_WIDTH)
    kv_rows = jnp.stack([k_c, v_c, k_s, v_s], axis=2)
    wl = min(WINDOW, T)
    win_rows = jnp.stack([k_w, v_w], axis=2)[:, T - wl:]
    return o, kv_rows, win_rows


def nsa_sample(q, gates, kvs, cache_kv_l, cache_win_l, page_table, cmp_k, cmp_v):
    B, Tn, H, D = q.shape
    k_c, v_c, k_s, v_s, k_w, v_w = kvs
    past_len = page_table.shape[1] * cache_kv_l.shape[1]
    past = cache_kv_l[page_table].reshape(B, past_len, 4, NSA_KV_HEADS, HEAD_DIM)
    new_rows = jnp.stack([k_c, v_c, k_s, v_s], axis=2)
    full = jnp.concatenate([past, new_rows.astype(past.dtype)], axis=1)
    kc = compress(full[:, :, 0], *cmp_k)
    vc = compress(full[:, :, 1], *cmp_v)
    cpos = jnp.arange(kc.shape[1]) * CMP_STRIDE + CMP_BLOCK - 1
    ks_blocks, vs_blocks = to_blocks(full[:, :, 2]), to_blocks(full[:, :, 3])
    wl = cache_win_l.shape[1]
    wfull = jnp.concatenate([cache_win_l, jnp.stack([k_w, v_w], axis=2).astype(cache_win_l.dtype)], axis=1)
    wpos = past_len - wl + jnp.arange(wl + Tn)
    qpos = past_len + jnp.arange(Tn)
    o = nsa_block(q, qpos, gates, kc, vc, cpos, ks_blocks, vs_blocks, wfull[:, :, 0], wfull[:, :, 1], wpos)
    return o.reshape(B, Tn, NSA_WIDTH), new_rows, wfull[:, Tn:]


def merge_out(o_hg, o_nsa, gate_a, gate_b, w_proj_hg, w_proj_nsa, w_out):
    y = jax.nn.sigmoid(gate_a) * (o_hg @ w_proj_hg) + jax.nn.sigmoid(gate_b) * (o_nsa @ w_proj_nsa)
    return y @ w_out


def setup_inputs(seed: int = 0) -> dict:
    key = jax.random.key(seed)
    keys = list(jax.random.split(key, 48))
    cnt = [0]

    def nk():
        cnt[0] += 1
        return keys[cnt[0] - 1]

    def nrm(shape, scale):
        return jax.random.normal(nk(), shape, jnp.float32) * scale

    def gain(shape):
        return 1.0 + nrm(shape, 0.1)

    n_pages = PAST_LEN // PAGE_SIZE
    n_pool = (DEC_BATCH * n_pages * 5) // 4
    win_len = min(WINDOW, PAST_LEN)
    inp = {}
    inp['x_prompt'] = nrm((BATCH, SEQ, D_MODEL), 1.0)
    inp['x_sample'] = nrm((DEC_BATCH, DEC_SEQ, D_MODEL), 1.0)
    inp['cache_kv'] = nrm((DEPTH, n_pool, PAGE_SIZE, 4, NSA_KV_HEADS, HEAD_DIM), 1.0)
    inp['cache_win'] = nrm((DEPTH, DEC_BATCH, win_len, 2, NSA_KV_HEADS, HEAD_DIM), 1.0)
    inp['state_hgrn'] = nrm((DEPTH, DEC_BATCH, HG_HEADS, HG_DK, HG_DV), 0.3)
    inp['page_table'] = jax.random.permutation(nk(), n_pool)[: DEC_BATCH * n_pages].reshape(DEC_BATCH, n_pages).astype(jnp.int32)
    inp['norm_pre1'] = gain((DEPTH, D_MODEL))
    inp['norm_post1'] = gain((DEPTH, D_MODEL))
    inp['ff1_gate'] = nrm((DEPTH, D_MODEL, D_FF), D_MODEL ** -0.5)
    inp['ff1_up'] = nrm((DEPTH, D_MODEL, D_FF), D_MODEL ** -0.5)
    inp['ff1_down'] = nrm((DEPTH, D_FF, D_MODEL), D_FF ** -0.5)
    inp['norm_pre2'] = gain((DEPTH, D_MODEL))
    inp['norm_post2'] = gain((DEPTH, D_MODEL))
    inp['w_in'] = nrm((DEPTH, D_MODEL, D_IN), D_MODEL ** -0.5)
    inp['hg_lb'] = nrm((DEPTH + 1, HG_WIDTH), 0.5)
    inp['hg_gnorm'] = gain((DEPTH, HG_DV))
    inp['cmp_pos_k'] = nrm((DEPTH, CMP_BLOCK, HEAD_DIM), 0.5)
    inp['cmp_w1_k'] = nrm((DEPTH, CMP_BLOCK * HEAD_DIM, CMP_HIDDEN), (CMP_BLOCK * HEAD_DIM) ** -0.5)
    inp['cmp_w2_k'] = nrm((DEPTH, CMP_HIDDEN, HEAD_DIM), CMP_HIDDEN ** -0.5)
    inp['cmp_pos_v'] = nrm((DEPTH, CMP_BLOCK, HEAD_DIM), 0.5)
    inp['cmp_w1_v'] = nrm((DEPTH, CMP_BLOCK * HEAD_DIM, CMP_HIDDEN), (CMP_BLOCK * HEAD_DIM) ** -0.5)
    inp['cmp_w2_v'] = nrm((DEPTH, CMP_HIDDEN, HEAD_DIM), CMP_HIDDEN ** -0.5)
    inp['w_proj_hg'] = nrm((DEPTH, HG_VWIDTH, D_MODEL), HG_VWIDTH ** -0.5)
    inp['w_proj_nsa'] = nrm((DEPTH, NSA_WIDTH, D_MODEL), NSA_WIDTH ** -0.5)
    inp['w_out'] = nrm((DEPTH, D_MODEL, D_MODEL), D_MODEL ** -0.5)
    inp['norm_pre3'] = gain((DEPTH, D_MODEL))
    inp['norm_post3'] = gain((DEPTH, D_MODEL))
    inp['ff2_gate'] = nrm((DEPTH, D_MODEL, D_FF), D_MODEL ** -0.5)
    inp['ff2_up'] = nrm((DEPTH, D_MODEL, D_FF), D_MODEL ** -0.5)
    inp['ff2_down'] = nrm((DEPTH, D_FF, D_MODEL), D_FF ** -0.5)
    return inp


def reference(x_prompt, x_sample, cache_kv, cache_win, state_hgrn, page_table,
              norm_pre1, norm_post1, ff1_gate, ff1_up, ff1_down,
              norm_pre2, norm_post2, w_in, hg_lb, hg_gnorm,
              cmp_pos_k, cmp_w1_k, cmp_w2_k, cmp_pos_v, cmp_w1_v, cmp_w2_v,
              w_proj_hg, w_proj_nsa, w_out,
              norm_pre3, norm_post3, ff2_gate, ff2_up, ff2_down):
    lb_all = jnp.cumsum(jax.nn.softmax(hg_lb.astype(jnp.float32), axis=0), axis=0)[:DEPTH]
    xp, xs = x_prompt, x_sample
    kv_p, kv_s, win_p, win_s, st_p, st_s = [], [], [], [], [], []
    for l in range(DEPTH):
        xp = half_ffn(xp, norm_pre1[l], norm_post1[l], ff1_gate[l], ff1_up[l], ff1_down[l])
        xs = half_ffn(xs, norm_pre1[l], norm_post1[l], ff1_gate[l], ff1_up[l], ff1_down[l])
        cmp_k = (cmp_pos_k[l], cmp_w1_k[l], cmp_w2_k[l])
        cmp_v = (cmp_pos_v[l], cmp_w1_v[l], cmp_w2_v[l])
        h = rmsnorm(xp, norm_pre2[l])
        hg_in, nsa_in, gate_a, gate_b = in_project(h, w_in[l])
        s0 = jnp.zeros((xp.shape[0], HG_HEADS, HG_DK, HG_DV), jnp.float32)
        o_hg, s_new = hgrn2_branch(*hg_in, lb_all[l], hg_gnorm[l], s0)
        q, gates, kvs = nsa_heads(*nsa_in)
        o_nsa, kv_rows, win_rows = nsa_prompt(q, gates, kvs, cmp_k, cmp_v)
        mix = merge_out(o_hg, o_nsa.astype(h.dtype), gate_a, gate_b, w_proj_hg[l], w_proj_nsa[l], w_out[l])
        xp = xp + rmsnorm(mix, norm_post2[l])
        kv_p.append(kv_rows)
        win_p.append(win_rows)
        st_p.append(s_new)
        h = rmsnorm(xs, norm_pre2[l])
        hg_in, nsa_in, gate_a, gate_b = in_project(h, w_in[l])
        o_hg, s_new = hgrn2_branch(*hg_in, lb_all[l], hg_gnorm[l], state_hgrn[l])
        q, gates, kvs = nsa_heads(*nsa_in)
        o_nsa, kv_rows, win_buf = nsa_sample(q, gates, kvs, cache_kv[l], cache_win[l], page_table, cmp_k, cmp_v)
        mix = merge_out(o_hg, o_nsa.astype(h.dtype), gate_a, gate_b, w_proj_hg[l], w_proj_nsa[l], w_out[l])
        xs = xs + rmsnorm(mix, norm_post2[l])
        kv_s.append(kv_rows)
        win_s.append(win_buf)
        st_s.append(s_new)
        xp = half_ffn(xp, norm_pre3[l], norm_post3[l], ff2_gate[l], ff2_up[l], ff2_down[l])
        xs = half_ffn(xs, norm_pre3[l], norm_post3[l], ff2_gate[l], ff2_up[l], ff2_down[l])
    kv_prompt = jnp.stack(kv_p, axis=0)
    kv_sample = jnp.stack(kv_s, axis=0)
    win_prompt = jnp.stack(win_p, axis=0)
    win_sample = jnp.stack(win_s, axis=0)
    hgrn_prompt = jnp.stack(st_p, axis=0)
    hgrn_sample = jnp.stack(st_s, axis=0)
    return (xp, xs, kv_prompt, kv_sample, win_prompt, win_sample, hgrn_prompt, hgrn_sample)
```

```python
import functools

import numpy as np
import jax
import jax.numpy as jnp
from jax import lax
from jax.experimental import pallas as pl
from jax.experimental.pallas import tpu as pltpu

F32 = jnp.float32
BF16 = jnp.bfloat16

EPS = 1e-6
HG_HEADS = 8
HG_D = 128
NSA_HEADS = 16
NSA_KV_HEADS = 2
NSA_GROUP = NSA_HEADS // NSA_KV_HEADS
HEAD_DIM = 64
CMP_BLOCK = 32
CMP_STRIDE = 16
SLC_BLOCK = 64
SLC_SHIFT = 6
SLC_TOP = 16
WINDOW = 512
Q_BLOCK = 128
FORCE_SCORE = 1.0e4
NEG = -1.0e30
NEG_PICKED = -3.0e38

LANE = 128
VMEM_LIMIT = 56 * 1024 * 1024

C_HGQ, C_HGF, C_HGI, C_HGG = 0, 1024, 2048, 3072
C_NSAQ = 4096
C_KV = 5120
C_WIN = 5632
C_NGATE = 5888
C_GA = 6144
C_GB = 8192
D_PROJ = 10240


def _cparams(sem):
    return pltpu.CompilerParams(dimension_semantics=sem, vmem_limit_bytes=VMEM_LIMIT)


def _rms(x, g):
    return x * lax.rsqrt(jnp.mean(x * x, axis=-1, keepdims=True) + EPS) * g


def _silu(x):
    return x * jax.nn.sigmoid(x)


def _dot(a, b):
    return jnp.dot(a, b, preferred_element_type=F32)


def _dot_nt(a, b):
    return lax.dot_general(a, b, (((1,), (1,)), ((), ())), preferred_element_type=F32)


def _dot_tn(a, b):
    return lax.dot_general(a, b, (((0,), (0,)), ((), ())), preferred_element_type=F32)


def _split3(x):
    hi = x.astype(BF16)
    r = x - hi.astype(F32)
    mid = r.astype(BF16)
    lo = (r - mid.astype(F32)).astype(BF16)
    return hi, mid, lo


def _split2(x):
    hi = x.astype(BF16)
    return hi, (x - hi.astype(F32)).astype(BF16)


def _ffn_body(x_ref, npre_ref, npost_ref, wg_ref, wu_ref, wd_ref, o_ref, h_ref, acc_ref):
    f = pl.program_id(1)

    @pl.when(f == 0)
    def _():
        h_ref[...] = _rms(x_ref[...], npre_ref[...]).astype(BF16)
        acc_ref[...] = jnp.zeros_like(acc_ref)

    h = h_ref[...]
    a = _silu(_dot(h, wg_ref[...])) * _dot(h, wu_ref[...])
    acc_ref[...] += _dot(a.astype(BF16), wd_ref[...])

    @pl.when(f == pl.num_programs(1) - 1)
    def _():
        o_ref[...] = x_ref[...] + 0.5 * _rms(acc_ref[...], npost_ref[...])


def _ffn(x, npre, npost, wg, wu, wd, *, tm=512, tf=512):
    n, d = x.shape
    dff = wg.shape[1]
    tm = min(tm, n)
    return pl.pallas_call(
        _ffn_body,
        grid=(n // tm, dff // tf),
        in_specs=[
            pl.BlockSpec((tm, d), lambda i, f: (i, 0)),
            pl.BlockSpec((1, d), lambda i, f: (0, 0)),
            pl.BlockSpec((1, d), lambda i, f: (0, 0)),
            pl.BlockSpec((d, tf), lambda i, f: (0, f)),
            pl.BlockSpec((d, tf), lambda i, f: (0, f)),
            pl.BlockSpec((tf, d), lambda i, f: (f, 0)),
        ],
        out_specs=pl.BlockSpec((tm, d), lambda i, f: (i, 0)),
        out_shape=jax.ShapeDtypeStruct((n, d), F32),
        scratch_shapes=[pltpu.VMEM((tm, d), BF16), pltpu.VMEM((tm, d), F32)],
        compiler_params=_cparams(("parallel", "arbitrary")),
        name="ffn",
    )(x, npre, npost, wg, wu, wd)


def _inproj_body(x_ref, g_ref, w_ref, o_ref, h_ref):
    @pl.when(pl.program_id(1) == 0)
    def _():
        h_ref[...] = _rms(x_ref[...], g_ref[...]).astype(BF16)

    o_ref[...] = _dot(h_ref[...], w_ref[...])


def _inproj(x, g, w, *, tm=512, tn=512):
    n, d = x.shape
    dout = w.shape[1]
    tm = min(tm, n)
    return pl.pallas_call(
        _inproj_body,
        grid=(n // tm, dout // tn),
        in_specs=[
            pl.BlockSpec((tm, d), lambda i, j: (i, 0)),
            pl.BlockSpec((1, d), lambda i, j: (0, 0)),
            pl.BlockSpec((d, tn), lambda i, j: (0, j)),
        ],
        out_specs=pl.BlockSpec((tm, tn), lambda i, j: (i, j)),
        out_shape=jax.ShapeDtypeStruct((n, dout), F32),
        scratch_shapes=[pltpu.VMEM((tm, d), BF16)],
        compiler_params=_cparams(("parallel", "arbitrary")),
        name="inproj",
    )(x, g, w)


def _merge_body(ohg_ref, on0_ref, on1_ref, ga_ref, gb_ref, wh_ref, wn_ref, y_ref):
    onsa = jnp.concatenate([on0_ref[...], on1_ref[...]], axis=-1).astype(BF16)
    yh = _dot(ohg_ref[...].astype(BF16), wh_ref[...])
    yn = _dot(onsa, wn_ref[...])
    y = jax.nn.sigmoid(ga_ref[...]) * yh + jax.nn.sigmoid(gb_ref[...]) * yn
    y_ref[...] = y.astype(BF16)


def _merge(ohg, on0, on1, proj, wh, wn, *, tm=512):
    n = ohg.shape[0]
    d = wh.shape[1]
    tm = min(tm, n)
    return pl.pallas_call(
        _merge_body,
        grid=(n // tm,),
        in_specs=[
            pl.BlockSpec((tm, ohg.shape[1]), lambda i: (i, 0)),
            pl.BlockSpec((tm, on0.shape[1]), lambda i: (i, 0)),
            pl.BlockSpec((tm, on1.shape[1]), lambda i: (i, 0)),
            pl.BlockSpec((tm, d), lambda i: (i, C_GA // d)),
            pl.BlockSpec((tm, d), lambda i: (i, C_GB // d)),
            pl.BlockSpec(wh.shape, lambda i: (0, 0)),
            pl.BlockSpec(wn.shape, lambda i: (0, 0)),
        ],
        out_specs=pl.BlockSpec((tm, d), lambda i: (i, 0)),
        out_shape=jax.ShapeDtypeStruct((n, d), BF16),
        compiler_params=_cparams(("parallel",)),
        name="merge",
    )(ohg, on0, on1, proj, proj, wh, wn)


def _outproj_body(y_ref, x_ref, w_ref, g_ref, o_ref):
    o_ref[...] = x_ref[...] + _rms(_dot(y_ref[...], w_ref[...]), g_ref[...])


def _outproj(y, x, w, g, *, tm=512):
    n, d = x.shape
    tm = min(tm, n)
    return pl.pallas_call(
        _outproj_body,
        grid=(n // tm,),
        in_specs=[
            pl.BlockSpec((tm, d), lambda i: (i, 0)),
            pl.BlockSpec((tm, d), lambda i: (i, 0)),
            pl.BlockSpec(w.shape, lambda i: (0, 0)),
            pl.BlockSpec((1, d), lambda i: (0, 0)),
        ],
        out_specs=pl.BlockSpec((tm, d), lambda i: (i, 0)),
        out_shape=jax.ShapeDtypeStruct((n, d), F32),
        compiler_params=_cparams(("parallel",)),
        name="outproj",
    )(y, x, w, g)


def _hgrn_range_matrix(c):
    t = np.arange(c)[:, None]
    u = np.arange(c)[None, :]
    blocks = [u <= t, u > t]
    m = c // 2
    while m >= 1:
        p = (t // (2 * m)) * 2 * m + m - 1
        upper = (t // m) % 2 == 1
        blocks.append(np.where(upper, (u > p) & (u <= t), (u > t) & (u <= p)))
        m //= 2
    return np.concatenate(blocks, axis=0).astype(np.float32)


def _hgrn_gates(q_raw, f_raw, lb):
    f = lb + (1.0 - lb) * jax.nn.sigmoid(f_raw)
    return _silu(q_raw), jnp.log(f), 1.0 - f


def _hgrn_finish(o, g_raw, gnorm):
    return _rms(o, gnorm) * _silu(g_raw)


def _hgrn_prompt_body(q_ref, f_ref, i_ref, g_ref, lb_ref, gn_ref, rm_ref, o_ref, sfin_ref, s_ref, *, c):
    @pl.when(pl.program_id(1) == 0)
    def _():
        s_ref[...] = jnp.zeros((HG_D, HG_D), F32)

    q, logf, k = _hgrn_gates(q_ref[...], f_ref[...], lb_ref[...])
    v = i_ref[...]
    vb = v.astype(BF16)
    rm = rm_ref[...]
    hi, mid, lo = _split3(logf)
    e = _dot(rm, hi) + _dot(rm, mid) + _dot(rm, lo)
    b = e[0:c]
    s = s_ref[...]
    o = _dot((q * jnp.exp(b)).astype(BF16), s.astype(BF16))
    row = lax.broadcasted_iota(jnp.int32, (c, c), 0)
    col = lax.broadcasted_iota(jnp.int32, (c, c), 1)
    x = row ^ col
    a = jnp.where(x == 0, _dot_nt(q.astype(BF16), k.astype(BF16)), 0.0)
    rowk = lax.broadcasted_iota(jnp.int32, (c, HG_D), 0)
    m = c // 2
    lvl = 0
    while m >= 1:
        w = jnp.exp(e[(2 + lvl) * c:(3 + lvl) * c])
        upper = (rowk & m) != 0
        ql = jnp.where(upper, q * w, 0.0).astype(BF16)
        kl = jnp.where(upper, 0.0, k * w).astype(BF16)
        a = a + jnp.where(x < 2 * m, _dot_nt(ql, kl), 0.0)
        m //= 2
        lvl += 1
    o = o + _dot(a.astype(BF16), vb)
    kd = (k * jnp.exp(e[c:2 * c])).astype(BF16)
    ecol = jnp.transpose(jnp.broadcast_to(jnp.exp(b[c - 1:c, :]), (HG_D, HG_D)))
    s_new = ecol * s + _dot_tn(kd, vb)
    s_ref[...] = s_new
    sfin_ref[0] = s_new
    o_ref[...] = _hgrn_finish(o, g_ref[...], gn_ref[...])


def _hgrn_prompt(proj, lb, gnorm, *, c=128):
    t = proj.shape[0]
    assert c == HG_D
    rm = jnp.asarray(_hgrn_range_matrix(c), BF16)
    col = lambda base: (lambda h, ci: (ci, base // HG_D + h))
    return pl.pallas_call(
        functools.partial(_hgrn_prompt_body, c=c),
        grid=(HG_HEADS, t // c),
        in_specs=[
            pl.BlockSpec((c, HG_D), col(C_HGQ)),
            pl.BlockSpec((c, HG_D), col(C_HGF)),
            pl.BlockSpec((c, HG_D), col(C_HGI)),
            pl.BlockSpec((c, HG_D), col(C_HGG)),
            pl.BlockSpec((1, HG_D), lambda h, ci: (0, h)),
            pl.BlockSpec((1, HG_D), lambda h, ci: (0, 0)),
            pl.BlockSpec(rm.shape, lambda h, ci: (0, 0)),
        ],
        out_specs=[
            pl.BlockSpec((c, HG_D), lambda h, ci: (ci, h)),
            pl.BlockSpec((1, HG_D, HG_D), lambda h, ci: (h, 0, 0)),
        ],
        out_shape=[
            jax.ShapeDtypeStruct((t, HG_HEADS * HG_D), F32),
            jax.ShapeDtypeStruct((HG_HEADS, HG_D, HG_D), F32),
        ],
        scratch_shapes=[pltpu.VMEM((HG_D, HG_D), F32)],
        compiler_params=_cparams(("parallel", "arbitrary")),
        name="hgrn_prompt",
    )(proj, proj, proj, proj, lb, gnorm, rm)


def _pad16(x):
    return jnp.concatenate([x, jnp.zeros_like(x)], axis=0)


def _hgrn_sample_body(q_ref, f_ref, i_ref, g_ref, lb_ref, gn_ref, s_ref, o_ref, so_ref, *, tn):
    rows = 2 * tn
    row = lax.broadcasted_iota(jnp.int32, (rows, HG_D), 0)
    pos = row & (tn - 1)
    first = row < tn
    for h in range(HG_HEADS):
        sl = slice(h * HG_D, (h + 1) * HG_D)
        q, logf, k = _hgrn_gates(q_ref[:, sl], f_ref[:, sl], lb_ref[:, sl])
        v = i_ref[:, sl]
        b = logf
        d = 1
        while d < tn:
            b = b + jnp.where(pos >= d, pltpu.roll(b, d, 0), 0.0)
            d *= 2
        oi = jnp.sum(q * k, axis=-1, keepdims=True) * v
        for d in range(1, tn):
            ok = pos >= d
            w = jnp.exp(jnp.where(ok, b - pltpu.roll(b, d, 0), 0.0))
            a = jnp.sum(jnp.where(ok, q * pltpu.roll(k, d, 0) * w, 0.0), axis=-1, keepdims=True)
            oi = oi + a * pltpu.roll(v, d, 0)
        qe = _pad16(q * jnp.exp(b)).astype(BF16)
        blast = jnp.where(first, b[tn - 1:tn, :], b[rows - 1:rows, :])
        kd = k * jnp.exp(blast - b)
        v16 = _pad16(v).astype(BF16)
        o_inter = []
        for bi in range(2):
            s = s_ref[bi, h]
            o_inter.append(_dot(qe, s.astype(BF16))[0:rows])
            mine = first if bi == 0 else jnp.logical_not(first)
            kd16 = _pad16(jnp.where(mine, kd, 0.0)).astype(BF16)
            e_row = jnp.exp(b[(bi + 1) * tn - 1:(bi + 1) * tn, :])
            ecol = jnp.transpose(jnp.broadcast_to(e_row, (HG_D, HG_D)))
            so_ref[bi, h] = ecol * s + _dot_tn(kd16, v16)
        o = jnp.where(first, o_inter[0], o_inter[1]) + oi
        o_ref[:, sl] = _hgrn_finish(o, g_ref[:, sl], gn_ref[...])


def _hgrn_sample(proj, lb, gnorm, state, *, tn):
    n = proj.shape[0]
    nb = state.shape[0]
    assert n == nb * tn and tn == 4 and nb % 2 == 0
    w = HG_HEADS * HG_D
    rows = 2 * tn
    col = lambda base: (lambda i: (i, base // w))
    return pl.pallas_call(
        functools.partial(_hgrn_sample_body, tn=tn),
        grid=(nb // 2,),
        in_specs=[
            pl.BlockSpec((rows, w), col(C_HGQ)),
            pl.BlockSpec((rows, w), col(C_HGF)),
            pl.BlockSpec((rows, w), col(C_HGI)),
            pl.BlockSpec((rows, w), col(C_HGG)),
            pl.BlockSpec((1, w), lambda i: (0, 0)),
            pl.BlockSpec((1, HG_D), lambda i: (0, 0)),
            pl.BlockSpec((2, HG_HEADS, HG_D, HG_D), lambda i: (i, 0, 0, 0)),
        ],
        out_specs=[
            pl.BlockSpec((rows, w), lambda i: (i, 0)),
            pl.BlockSpec((2, HG_HEADS, HG_D, HG_D), lambda i: (i, 0, 0, 0)),
        ],
        out_shape=[
            jax.ShapeDtypeStruct((n, w), F32),
            jax.ShapeDtypeStruct(state.shape, F32),
        ],
        compiler_params=_cparams(("parallel",)),
        name="hgrn_sample",
    )(proj, proj, proj, proj, lb, gnorm, state)


def _alibi_slopes():
    return np.power(2.0, -8.0 * np.arange(1, NSA_HEADS + 1) / NSA_HEADS).astype(np.float32)


def _gelu_tanh(x):
    return 0.5 * x * (1.0 + jnp.tanh(np.float32(np.sqrt(2.0 / np.pi)) * (x + 0.044715 * (x * x * x))))


def _overlap_matrix(ncp, nlanes):
    cs = np.arange(ncp)[:, None] * CMP_STRIDE
    ss = np.arange(nlanes)[None, :] * SLC_BLOCK
    return ((cs <= ss + SLC_BLOCK - 1) & (cs + CMP_BLOCK - 1 >= ss)).astype(np.float32)


def _expand_matrix(nlanes, nkeys):
    return (np.arange(nkeys)[None, :] // SLC_BLOCK == np.arange(nlanes)[:, None]).astype(np.float32)


def _softmax_rows(s, mask):
    m = jnp.max(s, axis=-1, keepdims=True)
    e = jnp.where(mask, jnp.exp(s - m), 0.0)
    l = jnp.sum(e, axis=-1, keepdims=True)
    return e * (1.0 / jnp.maximum(l, 1e-30))


def _select_blocks(score, qblk, n_top):
    blk = lax.broadcasted_iota(jnp.int32, score.shape, 1)
    forced = (blk == 0) | (blk == qblk) | (blk == qblk - 1)
    valid = blk <= qblk
    work = jnp.where(valid, jnp.where(forced, FORCE_SCORE, score), NEG)
    sel = jnp.zeros(score.shape, F32)
    for _ in range(n_top):
        m = jnp.max(work, axis=-1, keepdims=True)
        first = jnp.min(jnp.where(work == m, blk, score.shape[1]), axis=-1, keepdims=True)
        pick = blk == first
        sel = jnp.where(pick, 1.0, sel)
        work = jnp.where(pick, NEG_PICKED, work)
    return jnp.where(valid, sel, 0.0)


def _compress_body(g_ref, pos_ref, w1_ref, w2_ref, o_ref):
    g = g_ref[0]
    ylo = _dot((g + pos_ref[0, 0]).astype(BF16), w1_ref[0, 0])
    yhi = _dot((g + pos_ref[0, 1]).astype(BF16), w1_ref[0, 1])
    pre = ylo + pltpu.roll(yhi, g.shape[0] - 1, 0)
    o_ref[0] = _dot(_gelu_tanh(pre).astype(BF16), w2_ref[0])


def _compress_prompt(groups, pos, w1, w2):
    na, ng, gw = groups.shape
    hid = w1.shape[-1]
    kv = lambda a: a // NSA_KV_HEADS
    return pl.pallas_call(
        _compress_body,
        grid=(na,),
        in_specs=[
            pl.BlockSpec((1, ng, gw), lambda a: (a, 0, 0)),
            pl.BlockSpec((1, 2, 1, gw), lambda a: (kv(a), 0, 0, 0)),
            pl.BlockSpec((1, 2, gw, hid), lambda a: (kv(a), 0, 0, 0)),
            pl.BlockSpec((1, hid, HEAD_DIM), lambda a: (kv(a), 0, 0)),
        ],
        out_specs=pl.BlockSpec((1, ng, HEAD_DIM), lambda a: (a, 0, 0)),
        out_shape=jax.ShapeDtypeStruct((na, ng, HEAD_DIM), F32),
        compiler_params=_cparams(("parallel",)),
        name="compress_prompt",
    )(groups, pos, w1, w2)


SLC_CHUNK = 512


def _nsa_prompt_body(q_ref, gate_ref, kct_ref, vc_ref, kst_ref, vs_ref, kwt_ref, vw_ref, ov_ref, ex_ref,
                     o_ref, m_ref, l_ref, acc_ref, *, n, n_top):
    g_n = NSA_GROUP
    qb = Q_BLOCK
    slopes = [float(s) for s in _alibi_slopes()[n * g_n:(n + 1) * g_n]]
    i = pl.program_id(0)
    start = i * qb
    q = (q_ref[...] * (HEAD_DIM ** -0.5)).astype(BF16)
    qs = jnp.concatenate([q[:, HEAD_DIM * g:HEAD_DIM * (g + 1)] for g in range(g_n)], axis=0)

    ncp = kct_ref.shape[2]
    t_c = lax.broadcasted_iota(jnp.int32, (qb, ncp), 0)
    c_c = lax.broadcasted_iota(jnp.int32, (qb, ncp), 1)
    dist_c = (t_c - CMP_STRIDE * c_c + (start - (CMP_BLOCK - 1))).astype(F32)
    mask_c = dist_c >= 0.0
    sc = _dot(qs, kct_ref[0])
    psum = jnp.zeros((qb, ncp), F32)
    p_all = []
    for g in range(g_n):
        s = jnp.where(mask_c, sc[g * qb:(g + 1) * qb] - slopes[g] * dist_c, NEG)
        p = _softmax_rows(s, mask_c)
        psum = psum + p
        p_all.append(p.astype(BF16))
    o_cmp = _dot(jnp.concatenate(p_all, axis=0), vc_ref[0])

    ph, plo = _split2(psum)
    ov = ov_ref[...]
    score = _dot(ph, ov) + _dot(plo, ov)
    t_b = lax.broadcasted_iota(jnp.int32, score.shape, 0)
    qblk = jnp.right_shift(start + t_b, SLC_SHIFT)
    sel = _select_blocks(score, qblk, n_top).astype(BF16)

    ck = SLC_CHUNK
    t_s = lax.broadcasted_iota(jnp.int32, (qb, ck), 0)
    s_s = lax.broadcasted_iota(jnp.int32, (qb, ck), 1)
    rel = t_s - s_s
    m_ref[...] = jnp.full(m_ref.shape, -1.0e20, F32)
    l_ref[...] = jnp.zeros(l_ref.shape, F32)
    acc_ref[...] = jnp.zeros(acc_ref.shape, F32)

    def chunk(c, carry):
        k0 = pl.multiple_of(c * ck, ck)
        dist = (rel + (start - k0)).astype(F32)
        keep = (_dot(sel, ex_ref[:, pl.ds(k0, ck)]) > 0.5) & (dist >= 0.0)
        bias = jnp.where(keep, 0.0, NEG)
        sk = _dot(qs, kst_ref[0, :, pl.ds(k0, ck)])
        p_all = []
        for g in range(g_n):
            s = sk[g * qb:(g + 1) * qb] - slopes[g] * dist + bias
            m_old = m_ref[g]
            m_new = jnp.maximum(m_old, jnp.max(s, axis=-1, keepdims=True))
            alpha = jnp.exp(m_old - m_new)
            p = jnp.exp(s - m_new)
            l_ref[g] = alpha * l_ref[g] + jnp.sum(p, axis=-1, keepdims=True)
            m_ref[g] = m_new
            acc_ref[g] = alpha * acc_ref[g]
            p_all.append(p.astype(BF16))
        pv = _dot(jnp.concatenate(p_all, axis=0), vs_ref[0, pl.ds(k0, ck), :])
        for g in range(g_n):
            acc_ref[g] += pv[g * qb:(g + 1) * qb]
        return carry

    lax.fori_loop(0, lax.div(start + qb + ck - 1, ck), chunk, 0)

    nw = WINDOW + qb
    t_w = lax.broadcasted_iota(jnp.int32, (qb, nw), 0)
    w_w = lax.broadcasted_iota(jnp.int32, (qb, nw), 1)
    dist_wi = t_w + WINDOW - w_w
    mask_w = (dist_wi >= 0) & (dist_wi < WINDOW) & (w_w >= WINDOW - start)
    dist_w = dist_wi.astype(F32)
    w0 = pl.multiple_of(start, qb)
    sw = _dot(qs, kwt_ref[0, :, pl.ds(w0, nw)])
    p_all = []
    for g in range(g_n):
        s = jnp.where(mask_w, sw[g * qb:(g + 1) * qb] - slopes[g] * dist_w, NEG)
        p_all.append(_softmax_rows(s, mask_w).astype(BF16))
    o_win = _dot(jnp.concatenate(p_all, axis=0), vw_ref[0, pl.ds(w0, nw), :])

    gate = jax.nn.sigmoid(gate_ref[...])
    outs = []
    for g in range(g_n):
        c0 = 3 * (n * g_n + g)
        o_sel = acc_ref[g] * (1.0 / jnp.maximum(l_ref[g], 1e-30))
        outs.append(gate[:, c0:c0 + 1] * o_cmp[g * qb:(g + 1) * qb]
                    + gate[:, c0 + 1:c0 + 2] * o_sel
                    + gate[:, c0 + 2:c0 + 3] * o_win[g * qb:(g + 1) * qb])
    o_ref[...] = jnp.concatenate(outs, axis=-1)


def _nsa_prompt(proj, kct, vc, kst, vs, kwt, vw, n):
    t = proj.shape[0]
    ncp = kct.shape[2]
    ns = t // SLC_BLOCK
    assert ns <= LANE and t % SLC_CHUNK == 0
    ov = jnp.asarray(_overlap_matrix(ncp, LANE), BF16)
    ex = jnp.asarray(_expand_matrix(LANE, t), BF16)
    wq = NSA_GROUP * HEAD_DIM
    head = lambda i: (n, 0, 0)
    full = lambda a: pl.BlockSpec((1,) + a.shape[1:], head)
    return pl.pallas_call(
        functools.partial(_nsa_prompt_body, n=n, n_top=min(SLC_TOP, ns)),
        grid=(t // Q_BLOCK,),
        in_specs=[
            pl.BlockSpec((Q_BLOCK, wq), lambda i: (i, C_NSAQ // wq + n)),
            pl.BlockSpec((Q_BLOCK, LANE), lambda i: (i, C_NGATE // LANE)),
            full(kct), full(vc), full(kst), full(vs), full(kwt), full(vw),
            pl.BlockSpec(ov.shape, lambda i: (0, 0)),
            pl.BlockSpec(ex.shape, lambda i: (0, 0)),
        ],
        out_specs=pl.BlockSpec((Q_BLOCK, wq), lambda i: (i, 0)),
        out_shape=jax.ShapeDtypeStruct((t, wq), F32),
        scratch_shapes=[
            pltpu.VMEM((NSA_GROUP, Q_BLOCK, 1), F32),
            pltpu.VMEM((NSA_GROUP, Q_BLOCK, 1), F32),
            pltpu.VMEM((NSA_GROUP, Q_BLOCK, HEAD_DIM), F32),
        ],
        compiler_params=_cparams(("parallel",)),
        name=f"nsa_prompt_{n}",
    )(proj, proj, kct, vc, kst, vs, kwt, vw, ov, ex)


def _nsa_sample_body(pt_ref, page_ref, newkv_ref, q_ref, win_ref, neww_ref, gate_ref, slope_ref,
                     posk_ref, w1k_ref, w2k_ref, posv_ref, w1v_ref, w2v_ref, ov_ref, ex_ref, same_ref,
                     o_ref, past_ref, wbuf_ref, *, past_len, tn, n_top):
    del pt_ref
    j = pl.program_id(1)
    page = page_ref.shape[1]
    nkeys = past_ref.shape[1]
    kvw = NSA_KV_HEADS * HEAD_DIM

    for a in range(4):
        cols = slice(a * kvw, (a + 1) * kvw)

        @pl.when(j == 0)
        def _():
            past_ref[a, past_len:past_len + tn, :] = newkv_ref[0, :, cols]
            past_ref[a, past_len + tn:nkeys, :] = jnp.zeros((nkeys - past_len - tn, kvw), F32)

        past_ref[a, pl.ds(pl.multiple_of(j * page, page), page), :] = page_ref[0, :, cols]

    @pl.when(j == pl.num_programs(1) - 1)
    def _():
        ncp = past_len // CMP_STRIDE

        def compress(a, pos_ref, w1_ref, w2_ref):
            acc = jnp.zeros((ncp, w1_ref.shape[2]), F32)
            for p in range(CMP_BLOCK):
                x = past_ref[a, pl.ds(p, ncp, stride=CMP_STRIDE), :] + pos_ref[p]
                acc = acc + _dot(x.astype(BF16), w1_ref[p])
            return _dot(_gelu_tanh(acc).astype(BF16), w2_ref[...])

        kc = compress(0, posk_ref, w1k_ref, w2k_ref).astype(BF16)
        vc = compress(1, posv_ref, w1v_ref, w2v_ref).astype(BF16)

        rows = q_ref.shape[1]
        q = (q_ref[0] * (HEAD_DIM ** -0.5)).astype(BF16)
        slope = slope_ref[:, 0:1]
        r_i = lax.broadcasted_iota(jnp.int32, (rows, 1), 0)
        qpos = past_len + (r_i & (tn - 1))
        low = r_i < rows // 2

        def pick_head(o):
            return jnp.where(low, o[:, 0:HEAD_DIM], o[:, HEAD_DIM:kvw])

        cpos = CMP_STRIDE * lax.broadcasted_iota(jnp.int32, (rows, ncp), 1) + (CMP_BLOCK - 1)
        dist_c = (qpos - cpos).astype(F32)
        mask_c = dist_c >= 0.0
        s = jnp.where(mask_c, _dot_nt(q, kc) - slope * dist_c, NEG)
        p_c = _softmax_rows(s, mask_c)
        o_cmp = pick_head(_dot(p_c.astype(BF16), vc))

        same = same_ref[...]
        ph, plo = _split2(p_c)
        psum = _dot(same, ph) + _dot(same, plo)
        ph, plo = _split2(psum)
        ov = ov_ref[...]
        score = _dot(ph, ov) + _dot(plo, ov)
        sel = _select_blocks(score, jnp.right_shift(qpos, SLC_SHIFT), n_top).astype(BF16)

        spos = lax.broadcasted_iota(jnp.int32, (rows, nkeys), 1)
        dist_s = (qpos - spos).astype(F32)
        mask_s = (_dot(sel, ex_ref[...]) > 0.5) & (dist_s >= 0.0)
        ks = past_ref[2].astype(BF16)
        s = jnp.where(mask_s, _dot_nt(q, ks) - slope * dist_s, NEG)
        p_s = _softmax_rows(s, mask_s)
        o_sel = pick_head(_dot(p_s.astype(BF16), past_ref[3].astype(BF16)))

        wl = win_ref.shape[1]
        nwk = wbuf_ref.shape[0]
        wbuf_ref[0:wl, :] = win_ref[0]
        wbuf_ref[wl:wl + tn, :] = neww_ref[0]
        wbuf_ref[wl + tn:nwk, :] = jnp.zeros((nwk - wl - tn, wbuf_ref.shape[1]), F32)
        wpos = (past_len - wl) + lax.broadcasted_iota(jnp.int32, (rows, nwk), 1)
        dist_wi = qpos - wpos
        mask_w = (dist_wi >= 0) & (dist_wi < WINDOW) & (wpos >= 0)
        s = jnp.where(mask_w, _dot_nt(q, wbuf_ref[:, 0:kvw].astype(BF16)) - slope * dist_wi.astype(F32), NEG)
        p_w = _softmax_rows(s, mask_w)
        o_win = pick_head(_dot(p_w.astype(BF16), wbuf_ref[:, kvw:2 * kvw].astype(BF16)))

        gate = jax.nn.sigmoid(gate_ref[0])
        o_ref[0] = gate[:, 0:1] * o_cmp + gate[:, 1:2] * o_sel + gate[:, 2:3] * o_win


def _nsa_sample(page_table, cache, newkv, qrows, win, neww, gates, cmpk, cmpv, *, tn):
    nb, npages = page_table.shape
    page = cache.shape[1]
    past_len = npages * page
    wl = win.shape[1]
    rows = qrows.shape[1]
    assert tn == 4 and past_len % LANE == 0 and page % CMP_STRIDE == 0
    nkeys = past_len + LANE
    nwk = wl + LANE
    ncp = past_len // CMP_STRIDE
    ns = -(-(past_len + tn) // SLC_BLOCK)
    assert ncp <= LANE and ns <= LANE
    r = np.arange(rows)
    h = r // tn
    slope = jnp.asarray(np.broadcast_to(_alibi_slopes()[h][:, None], (rows, LANE)).copy(), F32)
    nt = h // NSA_GROUP * tn + r % tn
    same = jnp.asarray((nt[:, None] == nt[None, :]).astype(np.float32), BF16)
    ov = jnp.asarray(_overlap_matrix(ncp, LANE), BF16)
    ex = jnp.asarray(_expand_matrix(LANE, nkeys), BF16)
    posk, w1k, w2k = cmpk
    posv, w1v, w2v = cmpv
    const = lambda a: pl.BlockSpec(a.shape, lambda b, j, pt: (0,) * a.ndim)
    per_b = lambda a: pl.BlockSpec((1,) + a.shape[1:], lambda b, j, pt: (b,) + (0,) * (a.ndim - 1))
    grid_spec = pltpu.PrefetchScalarGridSpec(
        num_scalar_prefetch=1,
        grid=(nb, npages),
        in_specs=[
            pl.BlockSpec((1, page, cache.shape[2]), lambda b, j, pt: (pt[b, j], 0, 0)),
            per_b(newkv), per_b(qrows), per_b(win), per_b(neww), per_b(gates), const(slope),
            const(posk), const(w1k), const(w2k), const(posv), const(w1v), const(w2v),
            const(ov), const(ex), const(same),
        ],
        out_specs=pl.BlockSpec((1, rows, HEAD_DIM), lambda b, j, pt: (b, 0, 0)),
        scratch_shapes=[pltpu.VMEM((4, nkeys, LANE), F32), pltpu.VMEM((nwk, win.shape[2]), F32)],
    )
    return pl.pallas_call(
        functools.partial(_nsa_sample_body, past_len=past_len, tn=tn, n_top=min(SLC_TOP, ns)),
        grid_spec=grid_spec,
        out_shape=jax.ShapeDtypeStruct((nb, rows, HEAD_DIM), F32),
        compiler_params=_cparams(("parallel", "arbitrary")),
        name="nsa_sample",
    )(page_table, cache, newkv, qrows, win, neww, gates, slope, posk, w1k, w2k, posv, w1v, w2v, ov, ex, same)


def _prep_ffn(wg, wu, wd, tf=512):
    dff = wg.shape[1]
    pad = -dff % tf
    return (jnp.pad(wg, ((0, 0), (0, pad))).astype(BF16), jnp.pad(wu, ((0, 0), (0, pad))).astype(BF16),
            jnp.pad(wd, ((0, pad), (0, 0))).astype(BF16))


def _prep_w_in(w_in):
    d = w_in.shape[0]
    n_gate = 3 * NSA_HEADS
    return jnp.concatenate([
        w_in[:, :C_NGATE], w_in[:, C_NGATE:C_NGATE + n_gate],
        jnp.zeros((d, C_GA - C_NGATE - n_gate), w_in.dtype), w_in[:, C_NGATE + n_gate:]], axis=1).astype(BF16)


def _prep_cmp_prompt(pos_k, w1_k, w2_k, pos_v, w1_v, w2_v):
    half = CMP_STRIDE * HEAD_DIM
    pos = jnp.stack([pos_k.reshape(2, 1, half), pos_v.reshape(2, 1, half)])
    w1 = jnp.stack([w1_k.reshape(2, half, -1), w1_v.reshape(2, half, -1)]).astype(BF16)
    w2 = jnp.stack([w2_k, w2_v]).astype(BF16)
    return pos, w1, w2


def _prep_cmp_sample(pos, w1, w2):
    hid = w1.shape[1]
    w1p = w1.reshape(CMP_BLOCK, HEAD_DIM, hid)
    z1 = jnp.zeros_like(w1p)
    w1b = jnp.concatenate([jnp.concatenate([w1p, z1], axis=2), jnp.concatenate([z1, w1p], axis=2)], axis=1)
    z2 = jnp.zeros_like(w2)
    w2b = jnp.concatenate([jnp.concatenate([w2, z2], axis=1), jnp.concatenate([z2, w2], axis=1)], axis=0)
    posb = jnp.concatenate([pos, pos], axis=1)[:, None, :]
    return posb, w1b.astype(BF16), w2b.astype(BF16)


def _heads_t(a, t):
    return a.reshape(t, NSA_KV_HEADS, HEAD_DIM).transpose(1, 2, 0).astype(BF16)


def _heads(a, t):
    return a.reshape(t, NSA_KV_HEADS, HEAD_DIM).transpose(1, 0, 2).astype(BF16)


def _mix_prompt(proj, lb, gnorm, cmp_w):
    t = proj.shape[0]
    o_hg, s_fin = _hgrn_prompt(proj, lb, gnorm)
    kw = HEAD_DIM * NSA_KV_HEADS
    cols = lambda i: proj[:, C_KV + i * kw:C_KV + (i + 1) * kw]
    k_c, v_c, k_s, v_s, k_w, v_w = (cols(i) for i in range(6))
    groups = jnp.stack([k_c, v_c]).reshape(2, t // CMP_STRIDE, CMP_STRIDE, NSA_KV_HEADS, HEAD_DIM)
    groups = groups.transpose(0, 3, 1, 2, 4).reshape(2 * NSA_KV_HEADS, t // CMP_STRIDE, CMP_STRIDE * HEAD_DIM)
    cmp = _compress_prompt(groups, *cmp_w)
    kct = cmp[:NSA_KV_HEADS].transpose(0, 2, 1).astype(BF16)
    vc = cmp[NSA_KV_HEADS:].astype(BF16)
    kwt = jnp.pad(_heads_t(k_w, t), ((0, 0), (0, 0), (WINDOW, 0)))
    vw = jnp.pad(_heads(v_w, t), ((0, 0), (WINDOW, 0), (0, 0)))
    kst, vs = _heads_t(k_s, t), _heads(v_s, t)
    o_nsa = [_nsa_prompt(proj, kct, vc, kst, vs, kwt, vw, n) for n in range(NSA_KV_HEADS)]
    return o_hg, o_nsa, s_fin


def _mix_sample(proj, lb, gnorm, state, cache, win, page_table, cmp_w, *, tn):
    nb = state.shape[0]
    o_hg, s_new = _hgrn_sample(proj, lb, gnorm, state, tn=tn)
    g_n, nkv, hd = NSA_GROUP, NSA_KV_HEADS, HEAD_DIM
    q = proj[:, C_NSAQ:C_NSAQ + NSA_HEADS * hd].reshape(nb, tn, nkv, g_n, hd).transpose(0, 2, 3, 1, 4)
    eye = jnp.eye(nkv, dtype=F32)
    qrows = (q[:, :, :, :, None, :] * eye[None, :, None, None, :, None]).reshape(nb, nkv * g_n * tn, nkv * hd)
    gates = proj[:, C_NGATE:C_NGATE + 3 * NSA_HEADS].reshape(nb, tn, NSA_HEADS, 3).transpose(0, 2, 1, 3)
    gates = jnp.pad(gates.reshape(nb, NSA_HEADS * tn, 3), ((0, 0), (0, 0), (0, LANE - 3)))
    newkv = proj[:, C_KV:C_WIN].reshape(nb, tn, C_WIN - C_KV)
    neww = proj[:, C_WIN:C_NGATE].reshape(nb, tn, C_NGATE - C_WIN)
    o = _nsa_sample(page_table, cache, newkv, qrows, win, neww, gates, *cmp_w, tn=tn)
    o = o.reshape(nb, nkv, g_n, tn, hd).transpose(0, 3, 1, 2, 4).reshape(nb * tn, nkv, g_n * hd)
    return o_hg, [o[:, n] for n in range(nkv)], s_new


def kernel(x_prompt, x_sample, cache_kv, cache_win, state_hgrn, page_table, norm_pre1, norm_post1, ff1_gate, ff1_up, ff1_down, norm_pre2, norm_post2, w_in, hg_lb, hg_gnorm, cmp_pos_k, cmp_w1_k, cmp_w2_k, cmp_pos_v, cmp_w1_v, cmp_w2_v, w_proj_hg, w_proj_nsa, w_out, norm_pre3, norm_post3, ff2_gate, ff2_up, ff2_down):
    depth = norm_pre1.shape[0]
    bp, tp, d = x_prompt.shape
    nb, tn, _ = x_sample.shape
    assert bp == 1
    lb_all = jnp.cumsum(jax.nn.softmax(hg_lb.astype(F32), axis=0), axis=0)[:depth]
    xp = x_prompt.reshape(tp, d)
    xs = x_sample.reshape(nb * tn, d)
    kv_p, kv_s, win_p, win_s, st_p, st_s = [], [], [], [], [], []
    row = lambda a: a.reshape(1, -1)
    for l in range(depth):
        ff1 = _prep_ffn(ff1_gate[l], ff1_up[l], ff1_down[l])
        ff2 = _prep_ffn(ff2_gate[l], ff2_up[l], ff2_down[l])
        w_in_l = _prep_w_in(w_in[l])
        wh, wn, wo = w_proj_hg[l].astype(BF16), w_proj_nsa[l].astype(BF16), w_out[l].astype(BF16)
        cmp_p = _prep_cmp_prompt(cmp_pos_k[l], cmp_w1_k[l], cmp_w2_k[l], cmp_pos_v[l], cmp_w1_v[l], cmp_w2_v[l])
        cmp_s = (_prep_cmp_sample(cmp_pos_k[l], cmp_w1_k[l], cmp_w2_k[l]),
                 _prep_cmp_sample(cmp_pos_v[l], cmp_w1_v[l], cmp_w2_v[l]))
        lb, gnorm = row(lb_all[l]), row(hg_gnorm[l])
        cache = cache_kv[l].reshape(cache_kv.shape[1], cache_kv.shape[2], -1)
        win = cache_win[l].reshape(nb, cache_win.shape[2], -1)

        xp = _ffn(xp, row(norm_pre1[l]), row(norm_post1[l]), *ff1)
        xs = _ffn(xs, row(norm_pre1[l]), row(norm_post1[l]), *ff1)

        pp = _inproj(xp, row(norm_pre2[l]), w_in_l)
        ps = _inproj(xs, row(norm_pre2[l]), w_in_l)
        ohg_p, onsa_p, s_p = _mix_prompt(pp, lb, gnorm, cmp_p)
        ohg_s, onsa_s, s_s = _mix_sample(ps, lb, gnorm, state_hgrn[l], cache, win, page_table, cmp_s, tn=tn)
        xp = _outproj(_merge(ohg_p, *onsa_p, pp, wh, wn), xp, wo, row(norm_post2[l]))
        xs = _outproj(_merge(ohg_s, *onsa_s, ps, wh, wn), xs, wo, row(norm_post2[l]))

        kv_p.append(pp[:, C_KV:C_WIN].reshape(bp, tp, 4, NSA_KV_HEADS, HEAD_DIM))
        kv_s.append(ps[:, C_KV:C_WIN].reshape(nb, tn, 4, NSA_KV_HEADS, HEAD_DIM))
        wl = min(WINDOW, tp)
        win_p.append(pp[tp - wl:, C_WIN:C_NGATE].reshape(bp, wl, 2, NSA_KV_HEADS, HEAD_DIM))
        new_w = ps[:, C_WIN:C_NGATE].reshape(nb, tn, 2, NSA_KV_HEADS, HEAD_DIM)
        win_s.append(jnp.concatenate([cache_win[l], new_w], axis=1)[:, tn:])
        st_p.append(s_p[None])
        st_s.append(s_s)

        xp = _ffn(xp, row(norm_pre3[l]), row(norm_post3[l]), *ff2)
        xs = _ffn(xs, row(norm_pre3[l]), row(norm_post3[l]), *ff2)
    return (xp.reshape(bp, tp, d), xs.reshape(nb, tn, d), jnp.stack(kv_p), jnp.stack(kv_s),
            jnp.stack(win_p), jnp.stack(win_s), jnp.stack(st_p), jnp.stack(st_s))
```

```python
import functools

import numpy as np
import jax
import jax.numpy as jnp
from jax import lax
from jax.experimental import pallas as pl
from jax.experimental.pallas import tpu as pltpu

F32 = jnp.float32
BF16 = jnp.bfloat16

EPS = 1e-6
HG_HEADS = 8
HG_D = 128
NSA_HEADS = 16
NSA_KV_HEADS = 2
NSA_GROUP = NSA_HEADS // NSA_KV_HEADS
HEAD_DIM = 64
CMP_BLOCK = 32
CMP_STRIDE = 16
SLC_BLOCK = 64
SLC_SHIFT = 6
SLC_TOP = 16
WINDOW = 512
Q_BLOCK = 128
FORCE_SCORE = 1.0e4
NEG = -1.0e30
NEG_PICKED = -3.0e38
M_INIT = -1.0e20
LOG2E = float(np.log2(np.e))

LANE = 128
VMEM_LIMIT = 56 * 1024 * 1024

W_NSAQ, W_KV, W_NGATE, W_GA = 4096, 5120, 5888, 5936
N_NGATE = 3 * NSA_HEADS
KVW = NSA_KV_HEADS * HEAD_DIM
C_HGQ, C_HGF, C_HGI, C_HGG = 0, 1024, 2048, 3072
C_GA, C_GB = 4096, 6144
C_CMP = 8192
C_NGATE = 8448
D_MAIN = 8704
R_KSLC, R_VSLC, R_KWIN, R_VWIN = 2 * KVW, 3 * KVW, 4 * KVW, 5 * KVW
N_KV_ROWS = 6 * KVW


def _cparams(sem):
    return pltpu.CompilerParams(dimension_semantics=sem, vmem_limit_bytes=VMEM_LIMIT)


def _rms(x, g):
    return x * lax.rsqrt(jnp.mean(x * x, axis=-1, keepdims=True) + EPS) * g


def _silu(x):
    return x * jax.nn.sigmoid(x)


def _dot(a, b):
    return jnp.dot(a, b, preferred_element_type=F32)


def _dot_nt(a, b):
    return lax.dot_general(a, b, (((1,), (1,)), ((), ())), preferred_element_type=F32)


def _dot_tn(a, b):
    return lax.dot_general(a, b, (((0,), (0,)), ((), ())), preferred_element_type=F32)


def _split3(x):
    hi = x.astype(BF16)
    r = x - hi.astype(F32)
    mid = r.astype(BF16)
    lo = (r - mid.astype(F32)).astype(BF16)
    return hi, mid, lo


def _split2(x):
    hi = x.astype(BF16)
    return hi, (x - hi.astype(F32)).astype(BF16)


def _ffn_body(x_ref, npre_ref, npost_ref, wg_ref, wu_ref, wd_ref, o_ref, h_ref, acc_ref):
    f = pl.program_id(1)

    @pl.when(f == 0)
    def _():
        h_ref[...] = _rms(x_ref[...], npre_ref[...]).astype(BF16)
        acc_ref[...] = jnp.zeros_like(acc_ref)

    h = h_ref[...]
    a = _silu(_dot(h, wg_ref[...])) * _dot(h, wu_ref[...])
    acc_ref[...] += _dot(a.astype(BF16), wd_ref[...])

    @pl.when(f == pl.num_programs(1) - 1)
    def _():
        o_ref[...] = x_ref[...] + 0.5 * _rms(acc_ref[...], npost_ref[...])


def _ffn(x, npre, npost, wg, wu, wd, *, tm=512, tf=512):
    n, d = x.shape
    dff = wg.shape[1]
    tm = min(tm, n)
    return pl.pallas_call(
        _ffn_body,
        grid=(n // tm, dff // tf),
        in_specs=[
            pl.BlockSpec((tm, d), lambda i, f: (i, 0)),
            pl.BlockSpec((1, d), lambda i, f: (0, 0)),
            pl.BlockSpec((1, d), lambda i, f: (0, 0)),
            pl.BlockSpec((d, tf), lambda i, f: (0, f)),
            pl.BlockSpec((d, tf), lambda i, f: (0, f)),
            pl.BlockSpec((tf, d), lambda i, f: (f, 0)),
        ],
        out_specs=pl.BlockSpec((tm, d), lambda i, f: (i, 0)),
        out_shape=jax.ShapeDtypeStruct((n, d), F32),
        scratch_shapes=[pltpu.VMEM((tm, d), BF16), pltpu.VMEM((tm, d), F32)],
        compiler_params=_cparams(("parallel", "arbitrary")),
        name="ffn",
    )(x, npre, npost, wg, wu, wd)


def _inproj_body(x_ref, g_ref, w_ref, o_ref, h_ref, *, scale):
    @pl.when(pl.program_id(1) == 0)
    def _():
        h_ref[...] = _rms(x_ref[...], g_ref[...]).astype(BF16)

    o_ref[...] = (_dot(h_ref[...], w_ref[...]) * scale).astype(o_ref.dtype)


def _inproj(x, g, w, *, out_dtype=F32, scale=1.0, tm=512, tn=512):
    n, d = x.shape
    dout = w.shape[1]
    tm = min(tm, n)
    return pl.pallas_call(
        functools.partial(_inproj_body, scale=scale),
        grid=(n // tm, dout // tn),
        in_specs=[
            pl.BlockSpec((tm, d), lambda i, j: (i, 0)),
            pl.BlockSpec((1, d), lambda i, j: (0, 0)),
            pl.BlockSpec((d, tn), lambda i, j: (0, j)),
        ],
        out_specs=pl.BlockSpec((tm, tn), lambda i, j: (i, j)),
        out_shape=jax.ShapeDtypeStruct((n, dout), out_dtype),
        scratch_shapes=[pltpu.VMEM((tm, d), BF16)],
        compiler_params=_cparams(("parallel", "arbitrary")),
        name="inproj",
    )(x, g, w)


def _inproj_t_body(x_ref, g_ref, wt_ref, o_ref, ob_ref):
    y = _dot_nt(wt_ref[...], _rms(x_ref[...], g_ref[...]).astype(BF16))
    o_ref[...] = y
    ob_ref[...] = y.astype(BF16)


def _inproj_t(x, g, wt, *, tm=512):
    n, d = x.shape
    dout = wt.shape[0]
    tm = min(tm, n)
    return pl.pallas_call(
        _inproj_t_body,
        grid=(n // tm,),
        in_specs=[
            pl.BlockSpec((tm, d), lambda i: (i, 0)),
            pl.BlockSpec((1, d), lambda i: (0, 0)),
            pl.BlockSpec((dout, d), lambda i: (0, 0)),
        ],
        out_specs=[pl.BlockSpec((dout, tm), lambda i: (0, i)), pl.BlockSpec((dout, tm), lambda i: (0, i))],
        out_shape=[jax.ShapeDtypeStruct((dout, n), F32), jax.ShapeDtypeStruct((dout, n), BF16)],
        compiler_params=_cparams(("parallel",)),
        name="inproj_t",
    )(x, g, wt)


def _merge_body(ohg_ref, on0_ref, on1_ref, ga_ref, gb_ref, wh_ref, wn_ref, y_ref):
    onsa = jnp.concatenate([on0_ref[...], on1_ref[...]], axis=-1).astype(BF16)
    yh = _dot(ohg_ref[...].astype(BF16), wh_ref[...])
    yn = _dot(onsa, wn_ref[...])
    y = jax.nn.sigmoid(ga_ref[...]) * yh + jax.nn.sigmoid(gb_ref[...]) * yn
    y_ref[...] = y.astype(BF16)


def _merge(ohg, on0, on1, main, wh, wn, *, tm=512):
    n = ohg.shape[0]
    d = wh.shape[1]
    tm = min(tm, n)
    return pl.pallas_call(
        _merge_body,
        grid=(n // tm,),
        in_specs=[
            pl.BlockSpec((tm, ohg.shape[1]), lambda i: (i, 0)),
            pl.BlockSpec((tm, on0.shape[1]), lambda i: (i, 0)),
            pl.BlockSpec((tm, on1.shape[1]), lambda i: (i, 0)),
            pl.BlockSpec((tm, d), lambda i: (i, C_GA // d)),
            pl.BlockSpec((tm, d), lambda i: (i, C_GB // d)),
            pl.BlockSpec(wh.shape, lambda i: (0, 0)),
            pl.BlockSpec(wn.shape, lambda i: (0, 0)),
        ],
        out_specs=pl.BlockSpec((tm, d), lambda i: (i, 0)),
        out_shape=jax.ShapeDtypeStruct((n, d), BF16),
        compiler_params=_cparams(("parallel",)),
        name="merge",
    )(ohg, on0, on1, main, main, wh, wn)


def _outproj_body(y_ref, x_ref, w_ref, g_ref, o_ref):
    o_ref[...] = x_ref[...] + _rms(_dot(y_ref[...], w_ref[...]), g_ref[...])


def _outproj(y, x, w, g, *, tm=512):
    n, d = x.shape
    tm = min(tm, n)
    return pl.pallas_call(
        _outproj_body,
        grid=(n // tm,),
        in_specs=[
            pl.BlockSpec((tm, d), lambda i: (i, 0)),
            pl.BlockSpec((tm, d), lambda i: (i, 0)),
            pl.BlockSpec(w.shape, lambda i: (0, 0)),
            pl.BlockSpec((1, d), lambda i: (0, 0)),
        ],
        out_specs=pl.BlockSpec((tm, d), lambda i: (i, 0)),
        out_shape=jax.ShapeDtypeStruct((n, d), F32),
        compiler_params=_cparams(("parallel",)),
        name="outproj",
    )(y, x, w, g)


def _hgrn_range_matrix(c):
    t = np.arange(c)[:, None]
    u = np.arange(c)[None, :]
    blocks = [u <= t, u > t]
    m = c // 2
    while m >= 1:
        p = (t // (2 * m)) * 2 * m + m - 1
        upper = (t // m) % 2 == 1
        blocks.append(np.where(upper, (u > p) & (u <= t), (u > t) & (u <= p)))
        m //= 2
    return np.concatenate(blocks, axis=0).astype(np.float32)


def _hgrn_gates(q_raw, f_raw, lb):
    f = lb + (1.0 - lb) * jax.nn.sigmoid(f_raw)
    return _silu(q_raw), jnp.log(f), 1.0 - f


def _hgrn_finish(o, g_raw, gnorm):
    return _rms(o, gnorm) * _silu(g_raw)


def _hgrn_prompt_body(q_ref, f_ref, i_ref, g_ref, lb_ref, gn_ref, rm_ref, o_ref, sfin_ref, s_ref, *, c):
    @pl.when(pl.program_id(1) == 0)
    def _():
        s_ref[...] = jnp.zeros((HG_D, HG_D), F32)

    q, logf, k = _hgrn_gates(q_ref[...], f_ref[...], lb_ref[...])
    v = i_ref[...]
    vb = v.astype(BF16)
    rm = rm_ref[...]
    hi, mid, lo = _split3(logf)
    e = _dot(rm, hi) + _dot(rm, mid) + _dot(rm, lo)
    b = e[0:c]
    s = s_ref[...]
    o = _dot((q * jnp.exp(b)).astype(BF16), s.astype(BF16))
    row = lax.broadcasted_iota(jnp.int32, (c, c), 0)
    col = lax.broadcasted_iota(jnp.int32, (c, c), 1)
    x = row ^ col
    a = jnp.where(x == 0, _dot_nt(q.astype(BF16), k.astype(BF16)), 0.0)
    rowk = lax.broadcasted_iota(jnp.int32, (c, HG_D), 0)
    m = c // 2
    lvl = 0
    while m >= 1:
        w = jnp.exp(e[(2 + lvl) * c:(3 + lvl) * c])
        upper = (rowk & m) != 0
        ql = jnp.where(upper, q * w, 0.0).astype(BF16)
        kl = jnp.where(upper, 0.0, k * w).astype(BF16)
        a = a + jnp.where(x < 2 * m, _dot_nt(ql, kl), 0.0)
        m //= 2
        lvl += 1
    o = o + _dot(a.astype(BF16), vb)
    kd = (k * jnp.exp(e[c:2 * c])).astype(BF16)
    ecol = jnp.transpose(jnp.broadcast_to(jnp.exp(b[c - 1:c, :]), (HG_D, HG_D)))
    s_new = ecol * s + _dot_tn(kd, vb)
    s_ref[...] = s_new
    sfin_ref[0] = s_new
    o_ref[...] = _hgrn_finish(o, g_ref[...], gn_ref[...])


def _hgrn_prompt(main, lb, gnorm, *, c=128):
    t = main.shape[0]
    assert c == HG_D
    rm = jnp.asarray(_hgrn_range_matrix(c), BF16)
    col = lambda base: (lambda h, ci: (ci, base // HG_D + h))
    return pl.pallas_call(
        functools.partial(_hgrn_prompt_body, c=c),
        grid=(HG_HEADS, t // c),
        in_specs=[
            pl.BlockSpec((c, HG_D), col(C_HGQ)),
            pl.BlockSpec((c, HG_D), col(C_HGF)),
            pl.BlockSpec((c, HG_D), col(C_HGI)),
            pl.BlockSpec((c, HG_D), col(C_HGG)),
            pl.BlockSpec((1, HG_D), lambda h, ci: (0, h)),
            pl.BlockSpec((1, HG_D), lambda h, ci: (0, 0)),
            pl.BlockSpec(rm.shape, lambda h, ci: (0, 0)),
        ],
        out_specs=[
            pl.BlockSpec((c, HG_D), lambda h, ci: (ci, h)),
            pl.BlockSpec((1, HG_D, HG_D), lambda h, ci: (h, 0, 0)),
        ],
        out_shape=[
            jax.ShapeDtypeStruct((t, HG_HEADS * HG_D), F32),
            jax.ShapeDtypeStruct((HG_HEADS, HG_D, HG_D), F32),
        ],
        scratch_shapes=[pltpu.VMEM((HG_D, HG_D), F32)],
        compiler_params=_cparams(("parallel", "arbitrary")),
        name="hgrn_prompt",
    )(main, main, main, main, lb, gnorm, rm)


def _pad16(x):
    return jnp.concatenate([x, jnp.zeros_like(x)], axis=0)


def _hgrn_sample_body(q_ref, f_ref, i_ref, g_ref, lb_ref, gn_ref, s_ref, o_ref, so_ref, *, tn):
    rows = 2 * tn
    row = lax.broadcasted_iota(jnp.int32, (rows, HG_D), 0)
    pos = row & (tn - 1)
    first = row < tn
    for h in range(HG_HEADS):
        sl = slice(h * HG_D, (h + 1) * HG_D)
        q, logf, k = _hgrn_gates(q_ref[:, sl], f_ref[:, sl], lb_ref[:, sl])
        v = i_ref[:, sl]
        b = logf
        d = 1
        while d < tn:
            b = b + jnp.where(pos >= d, pltpu.roll(b, d, 0), 0.0)
            d *= 2
        oi = jnp.sum(q * k, axis=-1, keepdims=True) * v
        for d in range(1, tn):
            ok = pos >= d
            w = jnp.exp(jnp.where(ok, b - pltpu.roll(b, d, 0), 0.0))
            a = jnp.sum(jnp.where(ok, q * pltpu.roll(k, d, 0) * w, 0.0), axis=-1, keepdims=True)
            oi = oi + a * pltpu.roll(v, d, 0)
        qe = _pad16(q * jnp.exp(b)).astype(BF16)
        blast = jnp.where(first, b[tn - 1:tn, :], b[rows - 1:rows, :])
        kd = k * jnp.exp(blast - b)
        v16 = _pad16(v).astype(BF16)
        o_inter = []
        for bi in range(2):
            s = s_ref[bi, h]
            o_inter.append(_dot(qe, s.astype(BF16))[0:rows])
            mine = first if bi == 0 else jnp.logical_not(first)
            kd16 = _pad16(jnp.where(mine, kd, 0.0)).astype(BF16)
            e_row = jnp.exp(b[(bi + 1) * tn - 1:(bi + 1) * tn, :])
            ecol = jnp.transpose(jnp.broadcast_to(e_row, (HG_D, HG_D)))
            so_ref[bi, h] = ecol * s + _dot_tn(kd16, v16)
        o = jnp.where(first, o_inter[0], o_inter[1]) + oi
        o_ref[:, sl] = _hgrn_finish(o, g_ref[:, sl], gn_ref[...])


def _hgrn_sample(main, lb, gnorm, state, *, tn):
    n = main.shape[0]
    nb = state.shape[0]
    assert n == nb * tn and tn == 4 and nb % 2 == 0
    w = HG_HEADS * HG_D
    rows = 2 * tn
    col = lambda base: (lambda i: (i, base // w))
    return pl.pallas_call(
        functools.partial(_hgrn_sample_body, tn=tn),
        grid=(nb // 2,),
        in_specs=[
            pl.BlockSpec((rows, w), col(C_HGQ)),
            pl.BlockSpec((rows, w), col(C_HGF)),
            pl.BlockSpec((rows, w), col(C_HGI)),
            pl.BlockSpec((rows, w), col(C_HGG)),
            pl.BlockSpec((1, w), lambda i: (0, 0)),
            pl.BlockSpec((1, HG_D), lambda i: (0, 0)),
            pl.BlockSpec((2, HG_HEADS, HG_D, HG_D), lambda i: (i, 0, 0, 0)),
        ],
        out_specs=[
            pl.BlockSpec((rows, w), lambda i: (i, 0)),
            pl.BlockSpec((2, HG_HEADS, HG_D, HG_D), lambda i: (i, 0, 0, 0)),
        ],
        out_shape=[
            jax.ShapeDtypeStruct((n, w), F32),
            jax.ShapeDtypeStruct(state.shape, F32),
        ],
        compiler_params=_cparams(("parallel",)),
        name="hgrn_sample",
    )(main, main, main, main, lb, gnorm, state)


def _alibi_slopes():
    return np.power(2.0, -8.0 * np.arange(1, NSA_HEADS + 1) / NSA_HEADS).astype(np.float32)


def _gelu_tanh(x):
    return 0.5 * x * (1.0 + jnp.tanh(np.float32(np.sqrt(2.0 / np.pi)) * (x + 0.044715 * (x * x * x))))


def _overlap_matrix(ncp, nlanes):
    cs = np.arange(ncp)[:, None] * CMP_STRIDE
    ss = np.arange(nlanes)[None, :] * SLC_BLOCK
    return ((cs <= ss + SLC_BLOCK - 1) & (cs + CMP_BLOCK - 1 >= ss)).astype(np.float32)


def _expand_matrix(nlanes, nkeys):
    return (np.arange(nkeys)[None, :] // SLC_BLOCK == np.arange(nlanes)[:, None]).astype(np.float32)


def _np_split3(x):
    parts = []
    r = np.asarray(x, np.float64)
    for _ in range(3):
        p = r.astype(np.float32).astype(BF16).astype(np.float64)
        parts.append(p.astype(np.float32))
        r = r - p
    return parts


def _softmax2_rows(s, mask):
    m = jnp.max(s, axis=-1, keepdims=True)
    e = jnp.where(mask, jnp.exp2(s - m), 0.0)
    l = jnp.sum(e, axis=-1, keepdims=True)
    return e * (1.0 / jnp.maximum(l, 1e-30))


def _select_blocks(score, qblk, n_top):
    blk = lax.broadcasted_iota(jnp.int32, score.shape, 1)
    forced = (blk == 0) | (blk == qblk) | (blk == qblk - 1)
    valid = blk <= qblk
    work = jnp.where(valid, jnp.where(forced, FORCE_SCORE, score), NEG)
    sel = jnp.zeros(score.shape, F32)
    for _ in range(n_top):
        m = jnp.max(work, axis=-1, keepdims=True)
        first = jnp.min(jnp.where(work == m, blk, score.shape[1]), axis=-1, keepdims=True)
        pick = blk == first
        sel = jnp.where(pick, 1.0, sel)
        work = jnp.where(pick, NEG_PICKED, work)
    return jnp.where(valid, sel, 0.0)


def _compress_body(g_ref, pos_ref, w1_ref, w2_ref, o_ref):
    g = g_ref[0]
    ylo = _dot((g + pos_ref[0, 0]).astype(BF16), w1_ref[0, 0])
    yhi = _dot((g + pos_ref[0, 1]).astype(BF16), w1_ref[0, 1])
    pre = ylo + pltpu.roll(yhi, g.shape[0] - 1, 0)
    o_ref[0] = _dot(_gelu_tanh(pre).astype(BF16), w2_ref[0]).astype(BF16)


def _compress_prompt(groups, pos, w1, w2):
    na, ng, gw = groups.shape
    hid = w1.shape[-1]
    kv = lambda a: a // NSA_KV_HEADS
    return pl.pallas_call(
        _compress_body,
        grid=(na,),
        in_specs=[
            pl.BlockSpec((1, ng, gw), lambda a: (a, 0, 0)),
            pl.BlockSpec((1, 2, 1, gw), lambda a: (kv(a), 0, 0, 0)),
            pl.BlockSpec((1, 2, gw, hid), lambda a: (kv(a), 0, 0, 0)),
            pl.BlockSpec((1, hid, HEAD_DIM), lambda a: (kv(a), 0, 0)),
        ],
        out_specs=pl.BlockSpec((1, ng, HEAD_DIM), lambda a: (a, 0, 0)),
        out_shape=jax.ShapeDtypeStruct((na, ng, HEAD_DIM), BF16),
        compiler_params=_cparams(("parallel",)),
        name="compress_prompt",
    )(groups, pos, w1, w2)


SLC_CHUNK = 512
CHUNK_BLOCK_SHIFT = 3
N_SLOPE_COLS = 6


def _nsa_prompt_body(q_ref, gate_ref, sl_ref, kct_ref, vc_ref, ka_ref, vst_ref, kwt_ref, vwt_ref, ov_ref,
                     o_ref, m_ref, l_ref, acc_ref, *, n, n_top):
    g_n = NSA_GROUP
    qb = Q_BLOCK
    slopes = [float(s) * LOG2E for s in _alibi_slopes()[n * g_n:(n + 1) * g_n]]
    i = pl.program_id(0)
    start = i * qb
    q = q_ref[...]
    qs = jnp.concatenate([q[:, HEAD_DIM * g:HEAD_DIM * (g + 1)] for g in range(g_n)], axis=0)

    ncp = kct_ref.shape[1]
    t_c = lax.broadcasted_iota(jnp.int32, (qb, ncp), 0)
    c_c = lax.broadcasted_iota(jnp.int32, (qb, ncp), 1)
    dist_c = (t_c - CMP_STRIDE * c_c + (start - (CMP_BLOCK - 1))).astype(F32)
    mask_c = dist_c >= 0.0
    sc = _dot(qs, kct_ref[...])
    psum = jnp.zeros((qb, ncp), F32)
    p_all = []
    for g in range(g_n):
        s = jnp.where(mask_c, sc[g * qb:(g + 1) * qb] - slopes[g] * dist_c, NEG)
        p = _softmax2_rows(s, mask_c)
        psum = psum + p
        p_all.append(p.astype(BF16))
    o_cmp = _dot(jnp.concatenate(p_all, axis=0), vc_ref[...])

    ph, plo = _split2(psum)
    ov = ov_ref[...]
    score = _dot(ph, ov) + _dot(plo, ov)
    t_b = lax.broadcasted_iota(jnp.int32, score.shape, 0)
    qblk = jnp.right_shift(start + t_b, SLC_SHIFT)
    sel = _select_blocks(score, qblk, n_top)

    ck = SLC_CHUNK
    notsel = (1.0 - sel).astype(BF16)
    qa = jnp.concatenate([qs, sl_ref[...], jnp.concatenate([notsel] * g_n, axis=0)], axis=1)
    t_s = lax.broadcasted_iota(jnp.int32, (qb, ck), 0)
    s_s = lax.broadcasted_iota(jnp.int32, (qb, ck), 1)
    rel = t_s - s_s
    m_ref[...] = jnp.full(m_ref.shape, M_INIT, F32)
    l_ref[...] = jnp.zeros(l_ref.shape, F32)
    acc_ref[...] = jnp.zeros(acc_ref.shape, F32)

    def attend(k0, causal):
        s_all = _dot(qa, ka_ref[:, pl.ds(k0, ck)])
        if causal:
            keep = (rel + (start - k0)) >= 0
        p_all = []
        for g in range(g_n):
            s = s_all[g * qb:(g + 1) * qb]
            if causal:
                s = jnp.where(keep, s, NEG)
            m_old = m_ref[g]
            m_new = jnp.maximum(m_old, jnp.max(s, axis=-1, keepdims=True))
            alpha = jnp.exp2(m_old - m_new)
            p = jnp.exp2(s - m_new)
            l_ref[g] = alpha * l_ref[g] + jnp.sum(p, axis=-1, keepdims=True)
            m_ref[g] = m_new
            acc_ref[g] = alpha * acc_ref[g]
            p_all.append(p.astype(BF16))
        pv = _dot_nt(jnp.concatenate(p_all, axis=0), vst_ref[:, pl.ds(k0, ck)])
        for g in range(g_n):
            acc_ref[g] += pv[g * qb:(g + 1) * qb]

    blk_any = jnp.max(sel, axis=0, keepdims=True)
    lane = lax.broadcasted_iota(jnp.int32, blk_any.shape, 1)
    c_last = lax.div(start, ck)

    def chunk(c, carry):
        used = jnp.max(jnp.where(jnp.right_shift(lane, CHUNK_BLOCK_SHIFT) == c, blk_any, 0.0))

        @pl.when(used > 0.5)
        def _():
            attend(pl.multiple_of(c * ck, ck), causal=False)

        return carry

    lax.fori_loop(0, c_last, chunk, 0)
    attend(pl.multiple_of(c_last * ck, ck), causal=True)

    nw = WINDOW + qb
    t_w = lax.broadcasted_iota(jnp.int32, (qb, nw), 0)
    w_w = lax.broadcasted_iota(jnp.int32, (qb, nw), 1)
    dist_wi = t_w + WINDOW - w_w
    mask_w = (dist_wi >= 0) & (dist_wi < WINDOW) & (w_w >= WINDOW - start)
    dist_w = dist_wi.astype(F32)
    w0 = pl.multiple_of(start, qb)
    sw = _dot(qs, kwt_ref[:, pl.ds(w0, nw)])
    p_all = []
    for g in range(g_n):
        s = jnp.where(mask_w, sw[g * qb:(g + 1) * qb] - slopes[g] * dist_w, NEG)
        p_all.append(_softmax2_rows(s, mask_w).astype(BF16))
    o_win = _dot_nt(jnp.concatenate(p_all, axis=0), vwt_ref[:, pl.ds(w0, nw)])

    gate = jax.nn.sigmoid(gate_ref[...])
    outs = []
    for g in range(g_n):
        c0 = 3 * (n * g_n + g)
        o_sel = acc_ref[g] * (1.0 / jnp.maximum(l_ref[g], 1e-30))
        outs.append(gate[:, c0:c0 + 1] * o_cmp[g * qb:(g + 1) * qb]
                    + gate[:, c0 + 1:c0 + 2] * o_sel
                    + gate[:, c0 + 2:c0 + 3] * o_win[g * qb:(g + 1) * qb])
    o_ref[...] = jnp.concatenate(outs, axis=-1)


def _nsa_prompt(main, q16, kvtb, kwtp, kct, vc, n):
    t = main.shape[0]
    ncp = kct.shape[1]
    ns = t // SLC_BLOCK
    assert ns <= LANE and t % SLC_CHUNK == 0
    g_n, hd = NSA_GROUP, HEAD_DIM
    ov = jnp.asarray(_overlap_matrix(ncp, LANE), BF16)
    sl = np.zeros((g_n, Q_BLOCK, hd), np.float32)
    parts = _np_split3(_alibi_slopes()[n * g_n:(n + 1) * g_n].astype(np.float64) * LOG2E)
    for j in range(N_SLOPE_COLS):
        sl[:, :, j] = parts[j % 3][:, None]
    sl = jnp.asarray(sl.reshape(g_n * Q_BLOCK, hd), BF16)
    pos = np.arange(t)
    pos_rows = np.zeros((hd, t), np.float32)
    pos_rows[0:3] = (pos // SLC_BLOCK) * SLC_BLOCK
    pos_rows[3:6] = pos % SLC_BLOCK
    ka = jnp.concatenate([kvtb[R_KSLC + n * hd:R_KSLC + (n + 1) * hd], jnp.asarray(pos_rows, BF16),
                          jnp.asarray(NEG * _expand_matrix(LANE, t), BF16)], axis=0)
    wq = g_n * hd
    whole =lambda a: pl.BlockSpec(a.shape, lambda i: (0,) * a.ndim)
    return pl.pallas_call(
        functools.partial(_nsa_prompt_body, n=n, n_top=min(SLC_TOP, ns)),
        grid=(t // Q_BLOCK,),
        in_specs=[
            pl.BlockSpec((Q_BLOCK, wq), lambda i: (i, n)),
            pl.BlockSpec((Q_BLOCK, LANE), lambda i: (i, C_NGATE // LANE)),
            whole(sl), whole(kct), whole(vc), whole(ka),
            pl.BlockSpec((hd, t), lambda i: (R_VSLC // hd + n, 0)),
            pl.BlockSpec((hd, kwtp.shape[1]), lambda i: (n, 0)),
            pl.BlockSpec((hd, kwtp.shape[1]), lambda i: (NSA_KV_HEADS + n, 0)),
            whole(ov),
        ],
        out_specs=pl.BlockSpec((Q_BLOCK, wq), lambda i: (i, 0)),
        out_shape=jax.ShapeDtypeStruct((t, wq), F32),
        scratch_shapes=[
            pltpu.VMEM((g_n, Q_BLOCK, 1), F32),
            pltpu.VMEM((g_n, Q_BLOCK, 1), F32),
            pltpu.VMEM((g_n, Q_BLOCK, hd), F32),
        ],
        compiler_params=_cparams(("parallel",)),
        name=f"nsa_prompt_{n}",
    )(q16, main, sl, kct, vc, ka, kvtb, kwtp, kwtp, ov)


N_KINDS = 4
NEW_LANES = 8


def _nsa_sample_body(pt_ref, cache_ref, q_ref, new_ref, win_ref, gate_ref, slope_ref,
                     posk_ref, w1k_ref, w2k_ref, posv_ref, w1v_ref, w2v_ref, ov_ref, ex_ref, same_ref,
                     o_ref, wout_ref, buf_ref, rows_ref, newp_ref, sem, *, tn, n_top):
    b = pl.program_id(0)
    nb = pl.num_programs(0)
    npages, page = buf_ref.shape[1], buf_ref.shape[4]
    past_len = npages * page
    slot = b % 2

    def page_copy(bb, sl, j):
        return pltpu.make_async_copy(cache_ref.at[pt_ref[bb, j]], buf_ref.at[sl, j], sem.at[sl])

    @pl.when(b == 0)
    def _():
        for j in range(npages):
            page_copy(0, 0, j).start()
        newp_ref[...] = jnp.zeros(newp_ref.shape, F32)

    @pl.when(b + 1 < nb)
    def _():
        for j in range(npages):
            page_copy(b + 1, 1 - slot, j).start()

    for j in range(npages):
        page_copy(b, slot, j).wait()
    pg = buf_ref.at[slot]

    for a in range(4):
        newp_ref[a, :, 0:NEW_LANES] = new_ref[0, (2 + a) * KVW:(3 + a) * KVW, :]

    ncp = past_len // CMP_STRIDE
    for a in range(2):
        for j in range(npages):
            rows_ref[a, j * page:(j + 1) * page, :] = jnp.transpose(pg[j, a])
        rows_ref[a, past_len:past_len + LANE, :] = jnp.zeros((LANE, KVW), F32)

    def compress(a, pos_ref, w1_ref, w2_ref):
        acc = jnp.zeros((ncp, w1_ref.shape[2]), F32)
        for p in range(CMP_BLOCK):
            x = rows_ref[a, pl.ds(p, ncp, stride=CMP_STRIDE), :] + pos_ref[p]
            acc = acc + _dot(x.astype(BF16), w1_ref[p])
        return _dot(_gelu_tanh(acc).astype(BF16), w2_ref[...]).astype(BF16)

    kc = compress(0, posk_ref, w1k_ref, w2k_ref)
    vc = compress(1, posv_ref, w1v_ref, w2v_ref)

    rows = q_ref.shape[1]
    q = q_ref[0]
    slope = slope_ref[:, 0:1]
    r_i = lax.broadcasted_iota(jnp.int32, (rows, 1), 0)
    t_i = r_i & (tn - 1)
    qpos = past_len + t_i
    low = r_i < rows // 2

    def pick_head(o):
        return jnp.where(low, o[:, 0:HEAD_DIM], o[:, HEAD_DIM:KVW])

    cpos = CMP_STRIDE * lax.broadcasted_iota(jnp.int32, (rows, ncp), 1) + (CMP_BLOCK - 1)
    dist_c = (qpos - cpos).astype(F32)
    mask_c = dist_c >= 0.0
    s = jnp.where(mask_c, _dot_nt(q, kc) - slope * dist_c, NEG)
    p_c = _softmax2_rows(s, mask_c)
    o_cmp = pick_head(_dot(p_c.astype(BF16), vc))

    same = same_ref[...]
    ph, plo = _split2(p_c)
    psum = _dot(same, ph) + _dot(same, plo)
    ph, plo = _split2(psum)
    ov = ov_ref[...]
    score = _dot(ph, ov) + _dot(plo, ov)
    sel = _select_blocks(score, jnp.right_shift(qpos, SLC_SHIFT), n_top)

    u_i = lax.broadcasted_iota(jnp.int32, (rows, LANE), 1)
    new_ok = u_i <= t_i
    new_blk = past_len // SLC_BLOCK
    keep = _dot(sel.astype(BF16), ex_ref[...]) > 0.5
    spos = lax.broadcasted_iota(jnp.int32, (rows, past_len), 1)
    s_past = jnp.concatenate([_dot(q, pg[j, 2].astype(BF16)) for j in range(npages)], axis=1)
    s_past = jnp.where(keep, s_past - slope * (qpos - spos).astype(F32), NEG)
    keep_new = new_ok & (sel[:, new_blk:new_blk + 1] > 0.5)
    s_new = jnp.where(keep_new, _dot(q, newp_ref[0].astype(BF16)) - slope * (t_i - u_i).astype(F32), NEG)
    m = jnp.maximum(jnp.max(s_past, axis=-1, keepdims=True), jnp.max(s_new, axis=-1, keepdims=True))
    e_past = jnp.where(keep, jnp.exp2(s_past - m), 0.0)
    e_new = jnp.where(keep_new, jnp.exp2(s_new - m), 0.0)
    l = jnp.sum(e_past, axis=-1, keepdims=True) + jnp.sum(e_new, axis=-1, keepdims=True)
    e_past = e_past.astype(BF16)
    o_sel = _dot_nt(e_new.astype(BF16), newp_ref[1].astype(BF16))
    for j in range(npages):
        o_sel = o_sel + _dot_nt(e_past[:, j * page:(j + 1) * page], pg[j, 3].astype(BF16))
    o_sel = pick_head(o_sel * (1.0 / jnp.maximum(l, 1e-30)))

    wl = win_ref.shape[3]
    w_i = lax.broadcasted_iota(jnp.int32, (rows, wl), 1)
    dist_wi = (wl + t_i) - w_i
    mask_w = (dist_wi < WINDOW) & (w_i >= wl - past_len)
    s_win = jnp.where(mask_w, _dot(q, win_ref[0, 0].astype(BF16)) - slope * dist_wi.astype(F32), NEG)
    s_new = jnp.where(new_ok, _dot(q, newp_ref[2].astype(BF16)) - slope * (t_i - u_i).astype(F32), NEG)
    m = jnp.maximum(jnp.max(s_win, axis=-1, keepdims=True), jnp.max(s_new, axis=-1, keepdims=True))
    e_win = jnp.where(mask_w, jnp.exp2(s_win - m), 0.0)
    e_new = jnp.where(new_ok, jnp.exp2(s_new - m), 0.0)
    l = jnp.sum(e_win, axis=-1, keepdims=True) + jnp.sum(e_new, axis=-1, keepdims=True)
    o_win = (_dot_nt(e_win.astype(BF16), win_ref[0, 1].astype(BF16))
             + _dot_nt(e_new.astype(BF16), newp_ref[3].astype(BF16)))
    o_win = pick_head(o_win * (1.0 / jnp.maximum(l, 1e-30)))

    gate = jax.nn.sigmoid(gate_ref[0])
    o_ref[0] = gate[:, 0:1] * o_cmp + gate[:, 1:2] * o_sel + gate[:, 2:3] * o_win

    lane_w = lax.broadcasted_iota(jnp.int32, (KVW, LANE), 1)
    for a in range(2):
        shifted = pltpu.roll(win_ref[0, a], wl - tn, 1)
        tail = pltpu.roll(newp_ref[2 + a], LANE - tn, 1)
        wout_ref[0, a, :, 0:wl - LANE] = shifted[:, 0:wl - LANE]
        wout_ref[0, a, :, wl - LANE:wl] = jnp.where(lane_w >= LANE - tn, tail, shifted[:, wl - LANE:wl])


def _nsa_sample(page_table, cache, qrows, newt, win, gates, cmpk, cmpv, *, tn):
    nb, npages = page_table.shape
    page = cache.shape[3]
    past_len = npages * page
    wl = win.shape[3]
    rows = qrows.shape[1]
    assert tn == 4 and page == LANE and wl % LANE == 0 and wl > LANE and past_len >= wl
    ncp = past_len // CMP_STRIDE
    ns = -(-(past_len + tn) // SLC_BLOCK)
    assert ncp <= LANE and ns <= LANE
    r = np.arange(rows)
    h = r // tn
    slope = jnp.asarray(np.broadcast_to((_alibi_slopes()[h].astype(np.float64) * LOG2E)[:, None], (rows, LANE)), F32)
    nt = h // NSA_GROUP * tn + r % tn
    same = jnp.asarray((nt[:, None] == nt[None, :]).astype(np.float32), BF16)
    ov = jnp.asarray(_overlap_matrix(ncp, LANE), BF16)
    ex = jnp.asarray(_expand_matrix(LANE, past_len), BF16)
    posk, w1k, w2k = cmpk
    posv, w1v, w2v = cmpv
    const = lambda a: pl.BlockSpec(a.shape, lambda b, pt: (0,) * a.ndim)
    per_b = lambda a: pl.BlockSpec((1,) + a.shape[1:], lambda b, pt: (b,) + (0,) * (a.ndim - 1))
    grid_spec = pltpu.PrefetchScalarGridSpec(
        num_scalar_prefetch=1,
        grid=(nb,),
        in_specs=[
            pl.BlockSpec(memory_space=pl.ANY),
            per_b(qrows), per_b(newt), per_b(win), per_b(gates), const(slope),
            const(posk), const(w1k), const(w2k), const(posv), const(w1v), const(w2v),
            const(ov), const(ex), const(same),
        ],
        out_specs=[pl.BlockSpec((1, rows, HEAD_DIM), lambda b, pt: (b, 0, 0)), per_b(win)],
        scratch_shapes=[
            pltpu.VMEM((2, npages, N_KINDS, KVW, page), F32),
            pltpu.VMEM((2, past_len + LANE, KVW), F32),
            pltpu.VMEM((4, KVW, LANE), F32),
            pltpu.SemaphoreType.DMA((2,)),
        ],
    )
    return pl.pallas_call(
        functools.partial(_nsa_sample_body, tn=tn, n_top=min(SLC_TOP, ns)),
        grid_spec=grid_spec,
        out_shape=[jax.ShapeDtypeStruct((nb, rows, HEAD_DIM), F32), jax.ShapeDtypeStruct(win.shape, F32)],
        compiler_params=_cparams(("arbitrary",)),
        name="nsa_sample",
    )(page_table, cache, qrows, newt, win, gates, slope, posk, w1k, w2k, posv, w1v, w2v, ov, ex, same)


def _prep_ffn(wg, wu, wd, tf=512):
    dff = wg.shape[1]
    pad = -dff % tf
    return (jnp.pad(wg, ((0, 0), (0, pad))).astype(BF16), jnp.pad(wu, ((0, 0), (0, pad))).astype(BF16),
            jnp.pad(wd, ((0, pad), (0, 0))).astype(BF16))


def _prep_w_in(w_in):
    d = w_in.shape[0]
    cols = [w_in[:, :W_NSAQ], w_in[:, W_GA:], w_in[:, W_KV:W_KV + 2 * KVW], w_in[:, W_NGATE:W_GA]]
    used = sum(c.shape[1] for c in cols)
    w_main = jnp.concatenate(cols + [jnp.zeros((d, D_MAIN - used), w_in.dtype)], axis=1).astype(BF16)
    w_q = w_in[:, W_NSAQ:W_KV].astype(BF16)
    w_kvt = w_in[:, W_KV:W_NGATE].T.astype(BF16)
    return w_main, w_q, w_kvt


def _prep_cmp_prompt(pos_k, w1_k, w2_k, pos_v, w1_v, w2_v):
    half = CMP_STRIDE * HEAD_DIM
    pos = jnp.stack([pos_k.reshape(2, 1, half), pos_v.reshape(2, 1, half)])
    w1 = jnp.stack([w1_k.reshape(2, half, -1), w1_v.reshape(2, half, -1)]).astype(BF16)
    w2 = jnp.stack([w2_k, w2_v]).astype(BF16)
    return pos, w1, w2


def _prep_cmp_sample(pos, w1, w2):
    hid = w1.shape[1]
    w1p = w1.reshape(CMP_BLOCK, HEAD_DIM, hid)
    z1 = jnp.zeros_like(w1p)
    w1b = jnp.concatenate([jnp.concatenate([w1p, z1], axis=2), jnp.concatenate([z1, w1p], axis=2)], axis=1)
    z2 = jnp.zeros_like(w2)
    w2b = jnp.concatenate([jnp.concatenate([w2, z2], axis=1), jnp.concatenate([z2, w2], axis=1)], axis=0)
    posb = jnp.concatenate([pos, pos], axis=1)[:, None, :]
    return posb, w1b.astype(BF16), w2b.astype(BF16)


def _project(x, g, w_main, w_q, w_kvt):
    main = _inproj(x, g, w_main)
    q16 = _inproj(x, g, w_q, out_dtype=BF16, scale=HEAD_DIM ** -0.5 * LOG2E)
    kvt, kvtb = _inproj_t(x, g, w_kvt)
    return main, q16, kvt, kvtb


def _mix_prompt(main, q16, kvtb, lb, gnorm, cmp_w):
    t = main.shape[0]
    o_hg, s_fin = _hgrn_prompt(main, lb, gnorm)
    nkv, hd = NSA_KV_HEADS, HEAD_DIM
    groups = main[:, C_CMP:C_CMP + 2 * KVW].reshape(t // CMP_STRIDE, CMP_STRIDE, 2, nkv, hd)
    groups = groups.transpose(2, 3, 0, 1, 4).reshape(2 * nkv, t // CMP_STRIDE, CMP_STRIDE * hd)
    cmp = _compress_prompt(groups, *cmp_w)
    kct = cmp[:nkv].transpose(0, 2, 1)
    vc = cmp[nkv:]
    kwtp = jnp.pad(kvtb[R_KWIN:N_KV_ROWS], ((0, 0), (WINDOW, 0)))
    o_nsa = [_nsa_prompt(main, q16, kvtb, kwtp, kct[n], vc[n], n) for n in range(nkv)]
    return o_hg, o_nsa, s_fin


def _mix_sample(main, q16, kvt, lb, gnorm, state, cache, win, page_table, cmp_w, *, tn):
    nb = state.shape[0]
    o_hg, s_new = _hgrn_sample(main, lb, gnorm, state, tn=tn)
    g_n, nkv, hd = NSA_GROUP, NSA_KV_HEADS, HEAD_DIM
    q = q16.reshape(nb, tn, nkv, g_n, hd).transpose(0, 2, 3, 1, 4)
    eye = jnp.eye(nkv, dtype=BF16)
    qrows = (q[:, :, :, :, None, :] * eye[None, :, None, None, :, None]).reshape(nb, nkv * g_n * tn, nkv * hd)
    gates = main[:, C_NGATE:C_NGATE + N_NGATE].reshape(nb, tn, NSA_HEADS, 3).transpose(0, 2, 1, 3)
    gates = jnp.pad(gates.reshape(nb, NSA_HEADS * tn, 3), ((0, 0), (0, 0), (0, LANE - 3)))
    newt = jnp.pad(kvt.reshape(N_KV_ROWS, nb, tn).transpose(1, 0, 2), ((0, 0), (0, 0), (0, NEW_LANES - tn)))
    o, wout = _nsa_sample(page_table, cache, qrows, newt, win, gates, *cmp_w, tn=tn)
    o = o.reshape(nb, nkv, g_n, tn, hd).transpose(0, 3, 1, 2, 4).reshape(nb * tn, nkv, g_n * hd)
    return o_hg, [o[:, n] for n in range(nkv)], s_new, wout


def kernel(x_prompt, x_sample, cache_kv, cache_win, state_hgrn, page_table, norm_pre1, norm_post1, ff1_gate, ff1_up, ff1_down, norm_pre2, norm_post2, w_in, hg_lb, hg_gnorm, cmp_pos_k, cmp_w1_k, cmp_w2_k, cmp_pos_v, cmp_w1_v, cmp_w2_v, w_proj_hg, w_proj_nsa, w_out, norm_pre3, norm_post3, ff2_gate, ff2_up, ff2_down):
    depth = norm_pre1.shape[0]
    bp, tp, d = x_prompt.shape
    nb, tn, _ = x_sample.shape
    assert bp == 1
    nkv, hd = NSA_KV_HEADS, HEAD_DIM
    lb_all = jnp.cumsum(jax.nn.softmax(hg_lb.astype(F32), axis=0), axis=0)[:depth]
    xp = x_prompt.reshape(tp, d)
    xs = x_sample.reshape(nb * tn, d)
    kv_p, kv_s, win_p, win_s, st_p, st_s = [], [], [], [], [], []
    row = lambda a: a.reshape(1, -1)
    for l in range(depth):
        ff1 = _prep_ffn(ff1_gate[l], ff1_up[l], ff1_down[l])
        ff2 = _prep_ffn(ff2_gate[l], ff2_up[l], ff2_down[l])
        w_in_l = _prep_w_in(w_in[l])
        wh, wn, wo = w_proj_hg[l].astype(BF16), w_proj_nsa[l].astype(BF16), w_out[l].astype(BF16)
        cmp_p = _prep_cmp_prompt(cmp_pos_k[l], cmp_w1_k[l], cmp_w2_k[l], cmp_pos_v[l], cmp_w1_v[l], cmp_w2_v[l])
        cmp_s = (_prep_cmp_sample(cmp_pos_k[l], cmp_w1_k[l], cmp_w2_k[l]),
                 _prep_cmp_sample(cmp_pos_v[l], cmp_w1_v[l], cmp_w2_v[l]))
        lb, gnorm = row(lb_all[l]), row(hg_gnorm[l])
        n_pool, page = cache_kv.shape[1], cache_kv.shape[2]
        cache = cache_kv[l].transpose(0, 2, 3, 4, 1).reshape(n_pool, N_KINDS, KVW, page)
        wl = cache_win.shape[2]
        win = cache_win[l].transpose(0, 2, 3, 4, 1).reshape(nb, 2, KVW, wl)

        xp = _ffn(xp, row(norm_pre1[l]), row(norm_post1[l]), *ff1)
        xs = _ffn(xs, row(norm_pre1[l]), row(norm_post1[l]), *ff1)

        main_p, q_p, kvt_p, kvtb_p = _project(xp, row(norm_pre2[l]), *w_in_l)
        main_s, q_s, kvt_s, _ = _project(xs, row(norm_pre2[l]), *w_in_l)
        ohg_p, onsa_p, s_p = _mix_prompt(main_p, q_p, kvtb_p, lb, gnorm, cmp_p)
        ohg_s, onsa_s, s_s, wout = _mix_sample(main_s, q_s, kvt_s, lb, gnorm, state_hgrn[l], cache, win,
                                               page_table, cmp_s, tn=tn)
        xp = _outproj(_merge(ohg_p, *onsa_p, main_p, wh, wn), xp, wo, row(norm_post2[l]))
        xs = _outproj(_merge(ohg_s, *onsa_s, main_s, wh, wn), xs, wo, row(norm_post2[l]))

        kv_p.append(kvt_p[:N_KINDS * KVW].reshape(N_KINDS, nkv, hd, bp, tp).transpose(3, 4, 0, 1, 2))
        kv_s.append(kvt_s[:N_KINDS * KVW].reshape(N_KINDS, nkv, hd, nb, tn).transpose(3, 4, 0, 1, 2))
        wp = min(WINDOW, tp)
        win_p.append(kvt_p[R_KWIN:, tp - wp:].reshape(2, nkv, hd, bp, wp).transpose(3, 4, 0, 1, 2))
        win_s.append(wout.reshape(nb, 2, nkv, hd, wl).transpose(0, 4, 1, 2, 3))
        st_p.append(s_p[None])
        st_s.append(s_s)

        xp = _ffn(xp, row(norm_pre3[l]), row(norm_post3[l]), *ff2)
        xs = _ffn(xs, row(norm_pre3[l]), row(norm_post3[l]), *ff2)
    return (xp.reshape(bp, tp, d), xs.reshape(nb, tn, d), jnp.stack(kv_p), jnp.stack(kv_s),
            jnp.stack(win_p), jnp.stack(win_s), jnp.stack(st_p), jnp.stack(st_s))
```

```python
import functools

import numpy as np
import jax
import jax.numpy as jnp
from jax import lax
from jax.experimental import pallas as pl
from jax.experimental.pallas import tpu as pltpu

F32 = jnp.float32
BF16 = jnp.bfloat16

EPS = 1e-6
HG_HEADS = 8
HG_D = 128
NSA_HEADS = 16
NSA_KV_HEADS = 2
NSA_GROUP = NSA_HEADS // NSA_KV_HEADS
HEAD_DIM = 64
CMP_BLOCK = 32
CMP_STRIDE = 16
SLC_BLOCK = 64
SLC_SHIFT = 6
SLC_TOP = 16
WINDOW = 512
Q_BLOCK = 128
FORCE_SCORE = 1.0e4
NEG = -1.0e30
NEG_PICKED = -3.0e38
M_INIT = -1.0e20
LOG2E = float(np.log2(np.e))

LANE = 128
VMEM_LIMIT = 56 * 1024 * 1024

W_NSAQ, W_KV, W_NGATE, W_GA = 4096, 5120, 5888, 5936
N_NGATE = 3 * NSA_HEADS
KVW = NSA_KV_HEADS * HEAD_DIM
C_HGQ, C_HGF, C_HGI, C_HGG = 0, 1024, 2048, 3072
C_GA, C_GB = 4096, 6144
C_CMP = 8192
C_NGATE = 8448
D_MAIN = 8704
R_KSLC, R_VSLC, R_KWIN, R_VWIN = 2 * KVW, 3 * KVW, 4 * KVW, 5 * KVW
N_KV_ROWS = 6 * KVW


def _cparams(sem):
    return pltpu.CompilerParams(dimension_semantics=sem, vmem_limit_bytes=VMEM_LIMIT)


def _rms(x, g):
    return x * lax.rsqrt(jnp.mean(x * x, axis=-1, keepdims=True) + EPS) * g


def _silu(x):
    return x * jax.nn.sigmoid(x)


def _dot(a, b):
    return jnp.dot(a, b, preferred_element_type=F32)


def _dot_nt(a, b):
    return lax.dot_general(a, b, (((1,), (1,)), ((), ())), preferred_element_type=F32)


def _dot_tn(a, b):
    return lax.dot_general(a, b, (((0,), (0,)), ((), ())), preferred_element_type=F32)


def _split3(x):
    hi = x.astype(BF16)
    r = x - hi.astype(F32)
    mid = r.astype(BF16)
    lo = (r - mid.astype(F32)).astype(BF16)
    return hi, mid, lo


def _split2(x):
    hi = x.astype(BF16)
    return hi, (x - hi.astype(F32)).astype(BF16)


def _ffn_body(x_ref, npre_ref, npost_ref, wg_ref, wu_ref, wd_ref, o_ref, h_ref, acc_ref):
    f = pl.program_id(1)

    @pl.when(f == 0)
    def _():
        h_ref[...] = _rms(x_ref[...], npre_ref[...]).astype(BF16)
        acc_ref[...] = jnp.zeros_like(acc_ref)

    h = h_ref[...]
    a = _silu(_dot(h, wg_ref[...])) * _dot(h, wu_ref[...])
    acc_ref[...] += _dot(a.astype(BF16), wd_ref[...])

    @pl.when(f == pl.num_programs(1) - 1)
    def _():
        o_ref[...] = x_ref[...] + 0.5 * _rms(acc_ref[...], npost_ref[...])


def _ffn(x, npre, npost, wg, wu, wd, *, tm=512, tf=512):
    n, d = x.shape
    dff = wg.shape[1]
    tm = min(tm, n)
    return pl.pallas_call(
        _ffn_body,
        grid=(n // tm, dff // tf),
        in_specs=[
            pl.BlockSpec((tm, d), lambda i, f: (i, 0)),
            pl.BlockSpec((1, d), lambda i, f: (0, 0)),
            pl.BlockSpec((1, d), lambda i, f: (0, 0)),
            pl.BlockSpec((d, tf), lambda i, f: (0, f)),
            pl.BlockSpec((d, tf), lambda i, f: (0, f)),
            pl.BlockSpec((tf, d), lambda i, f: (f, 0)),
        ],
        out_specs=pl.BlockSpec((tm, d), lambda i, f: (i, 0)),
        out_shape=jax.ShapeDtypeStruct((n, d), F32),
        scratch_shapes=[pltpu.VMEM((tm, d), BF16), pltpu.VMEM((tm, d), F32)],
        compiler_params=_cparams(("parallel", "arbitrary")),
        name="ffn",
    )(x, npre, npost, wg, wu, wd)


def _inproj_body(x_ref, g_ref, w_ref, o_ref, h_ref, *, scale):
    @pl.when(pl.program_id(1) == 0)
    def _():
        h_ref[...] = _rms(x_ref[...], g_ref[...]).astype(BF16)

    o_ref[...] = (_dot(h_ref[...], w_ref[...]) * scale).astype(o_ref.dtype)


def _inproj(x, g, w, *, out_dtype=F32, scale=1.0, tm=512, tn=512):
    n, d = x.shape
    dout = w.shape[1]
    tm = min(tm, n)
    return pl.pallas_call(
        functools.partial(_inproj_body, scale=scale),
        grid=(n // tm, dout // tn),
        in_specs=[
            pl.BlockSpec((tm, d), lambda i, j: (i, 0)),
            pl.BlockSpec((1, d), lambda i, j: (0, 0)),
            pl.BlockSpec((d, tn), lambda i, j: (0, j)),
        ],
        out_specs=pl.BlockSpec((tm, tn), lambda i, j: (i, j)),
        out_shape=jax.ShapeDtypeStruct((n, dout), out_dtype),
        scratch_shapes=[pltpu.VMEM((tm, d), BF16)],
        compiler_params=_cparams(("parallel", "arbitrary")),
        name="inproj",
    )(x, g, w)


def _inproj_t_body(x_ref, g_ref, wt_ref, o_ref, ob_ref):
    y = _dot_nt(wt_ref[...], _rms(x_ref[...], g_ref[...]).astype(BF16))
    o_ref[...] = y
    ob_ref[...] = y.astype(BF16)


def _inproj_t(x, g, wt, *, tm=512):
    n, d = x.shape
    dout = wt.shape[0]
    tm = min(tm, n)
    return pl.pallas_call(
        _inproj_t_body,
        grid=(n // tm,),
        in_specs=[
            pl.BlockSpec((tm, d), lambda i: (i, 0)),
            pl.BlockSpec((1, d), lambda i: (0, 0)),
            pl.BlockSpec((dout, d), lambda i: (0, 0)),
        ],
        out_specs=[pl.BlockSpec((dout, tm), lambda i: (0, i)), pl.BlockSpec((dout, tm), lambda i: (0, i))],
        out_shape=[jax.ShapeDtypeStruct((dout, n), F32), jax.ShapeDtypeStruct((dout, n), BF16)],
        compiler_params=_cparams(("parallel",)),
        name="inproj_t",
    )(x, g, wt)


def _merge_body(ohg_ref, on0_ref, on1_ref, ga_ref, gb_ref, wh_ref, wn_ref, y_ref):
    onsa = jnp.concatenate([on0_ref[...], on1_ref[...]], axis=-1).astype(BF16)
    yh = _dot(ohg_ref[...].astype(BF16), wh_ref[...])
    yn = _dot(onsa, wn_ref[...])
    y = jax.nn.sigmoid(ga_ref[...]) * yh + jax.nn.sigmoid(gb_ref[...]) * yn
    y_ref[...] = y.astype(BF16)


def _merge(ohg, on0, on1, main, wh, wn, *, tm=512):
    n = ohg.shape[0]
    d = wh.shape[1]
    tm = min(tm, n)
    return pl.pallas_call(
        _merge_body,
        grid=(n // tm,),
        in_specs=[
            pl.BlockSpec((tm, ohg.shape[1]), lambda i: (i, 0)),
            pl.BlockSpec((tm, on0.shape[1]), lambda i: (i, 0)),
            pl.BlockSpec((tm, on1.shape[1]), lambda i: (i, 0)),
            pl.BlockSpec((tm, d), lambda i: (i, C_GA // d)),
            pl.BlockSpec((tm, d), lambda i: (i, C_GB // d)),
            pl.BlockSpec(wh.shape, lambda i: (0, 0)),
            pl.BlockSpec(wn.shape, lambda i: (0, 0)),
        ],
        out_specs=pl.BlockSpec((tm, d), lambda i: (i, 0)),
        out_shape=jax.ShapeDtypeStruct((n, d), BF16),
        compiler_params=_cparams(("parallel",)),
        name="merge",
    )(ohg, on0, on1, main, main, wh, wn)


def _outproj_body(y_ref, x_ref, w_ref, g_ref, o_ref):
    o_ref[...] = x_ref[...] + _rms(_dot(y_ref[...], w_ref[...]), g_ref[...])


def _outproj(y, x, w, g, *, tm=512):
    n, d = x.shape
    tm = min(tm, n)
    return pl.pallas_call(
        _outproj_body,
        grid=(n // tm,),
        in_specs=[
            pl.BlockSpec((tm, d), lambda i: (i, 0)),
            pl.BlockSpec((tm, d), lambda i: (i, 0)),
            pl.BlockSpec(w.shape, lambda i: (0, 0)),
            pl.BlockSpec((1, d), lambda i: (0, 0)),
        ],
        out_specs=pl.BlockSpec((tm, d), lambda i: (i, 0)),
        out_shape=jax.ShapeDtypeStruct((n, d), F32),
        compiler_params=_cparams(("parallel",)),
        name="outproj",
    )(y, x, w, g)


def _hgrn_range_matrix(c):
    t = np.arange(c)[:, None]
    u = np.arange(c)[None, :]
    blocks = [u <= t, u > t]
    m = c // 2
    while m >= 1:
        p = (t // (2 * m)) * 2 * m + m - 1
        upper = (t // m) % 2 == 1
        blocks.append(np.where(upper, (u > p) & (u <= t), (u > t) & (u <= p)))
        m //= 2
    return np.concatenate(blocks, axis=0).astype(np.float32)


def _hgrn_gates(q_raw, f_raw, lb):
    f = lb + (1.0 - lb) * jax.nn.sigmoid(f_raw)
    return _silu(q_raw), jnp.log(f), 1.0 - f


def _hgrn_finish(o, g_raw, gnorm):
    return _rms(o, gnorm) * _silu(g_raw)


def _hgrn_prompt_body(q_ref, f_ref, i_ref, g_ref, lb_ref, gn_ref, rm_ref, o_ref, sfin_ref, s_ref, *, c, hp):
    @pl.when(pl.program_id(1) == 0)
    def _():
        s_ref[...] = jnp.zeros(s_ref.shape, F32)

    cols = [slice(j * HG_D, (j + 1) * HG_D) for j in range(hp)]
    res = [_hgrn_chunk(q_ref[:, sl], f_ref[:, sl], i_ref[:, sl], lb_ref[:, sl], rm_ref[...], s_ref[j], c)
           for j, sl in enumerate(cols)]
    outs = [_hgrn_finish(o, g_ref[:, sl], gn_ref[...]) for (o, _), sl in zip(res, cols)]
    for j, sl in enumerate(cols):
        s_ref[j] = res[j][1]
        sfin_ref[j] = res[j][1]
        o_ref[:, sl] = outs[j]


def _hgrn_chunk(q_raw, f_raw, v, lb, rm, s, c):
    q, logf, k = _hgrn_gates(q_raw, f_raw, lb)
    vb = v.astype(BF16)
    hi, mid, lo = _split3(logf)
    e = _dot(rm, hi) + _dot(rm, mid) + _dot(rm, lo)
    b = e[0:c]
    o = _dot((q * jnp.exp(b)).astype(BF16), s.astype(BF16))
    row = lax.broadcasted_iota(jnp.int32, (c, c), 0)
    col = lax.broadcasted_iota(jnp.int32, (c, c), 1)
    x = row ^ col
    a = jnp.where(x == 0, _dot_nt(q.astype(BF16), k.astype(BF16)), 0.0)
    rowk = lax.broadcasted_iota(jnp.int32, (c, HG_D), 0)
    m = c // 2
    lvl = 0
    while m >= 1:
        w = jnp.exp(e[(2 + lvl) * c:(3 + lvl) * c])
        upper = (rowk & m) != 0
        ql = jnp.where(upper, q * w, 0.0).astype(BF16)
        kl = jnp.where(upper, 0.0, k * w).astype(BF16)
        a = a + jnp.where(x < 2 * m, _dot_nt(ql, kl), 0.0)
        m //= 2
        lvl += 1
    o = o + _dot(a.astype(BF16), vb)
    kd = (k * jnp.exp(e[c:2 * c])).astype(BF16)
    ecol = jnp.transpose(jnp.broadcast_to(jnp.exp(b[c - 1:c, :]), (HG_D, HG_D)))
    return o, ecol * s + _dot_tn(kd, vb)


def _hgrn_prompt(main, lb, gnorm, *, c=128, hp=4):
    t = main.shape[0]
    assert c == HG_D and HG_HEADS % hp == 0
    rm = jnp.asarray(_hgrn_range_matrix(c), BF16)
    w = hp * HG_D
    col = lambda base: (lambda h, ci: (ci, base // w + h))
    return pl.pallas_call(
        functools.partial(_hgrn_prompt_body, c=c, hp=hp),
        grid=(HG_HEADS // hp, t // c),
        in_specs=[
            pl.BlockSpec((c, w), col(C_HGQ)),
            pl.BlockSpec((c, w), col(C_HGF)),
            pl.BlockSpec((c, w), col(C_HGI)),
            pl.BlockSpec((c, w), col(C_HGG)),
            pl.BlockSpec((1, w), lambda h, ci: (0, h)),
            pl.BlockSpec((1, HG_D), lambda h, ci: (0, 0)),
            pl.BlockSpec(rm.shape, lambda h, ci: (0, 0)),
        ],
        out_specs=[
            pl.BlockSpec((c, w), lambda h, ci: (ci, h)),
            pl.BlockSpec((hp, HG_D, HG_D), lambda h, ci: (h, 0, 0)),
        ],
        out_shape=[
            jax.ShapeDtypeStruct((t, HG_HEADS * HG_D), F32),
            jax.ShapeDtypeStruct((HG_HEADS, HG_D, HG_D), F32),
        ],
        scratch_shapes=[pltpu.VMEM((hp, HG_D, HG_D), F32)],
        compiler_params=_cparams(("parallel", "arbitrary")),
        name="hgrn_prompt",
    )(main, main, main, main, lb, gnorm, rm)


def _pad16(x):
    return jnp.concatenate([x, jnp.zeros_like(x)], axis=0)


def _hgrn_sample_body(q_ref, f_ref, i_ref, g_ref, lb_ref, gn_ref, s_ref, o_ref, so_ref, *, tn):
    rows = 2 * tn
    row = lax.broadcasted_iota(jnp.int32, (rows, HG_D), 0)
    pos = row & (tn - 1)
    first = row < tn
    for h in range(HG_HEADS):
        sl = slice(h * HG_D, (h + 1) * HG_D)
        q, logf, k = _hgrn_gates(q_ref[:, sl], f_ref[:, sl], lb_ref[:, sl])
        v = i_ref[:, sl]
        b = logf
        d = 1
        while d < tn:
            b = b + jnp.where(pos >= d, pltpu.roll(b, d, 0), 0.0)
            d *= 2
        oi = jnp.sum(q * k, axis=-1, keepdims=True) * v
        for d in range(1, tn):
            ok = pos >= d
            w = jnp.exp(jnp.where(ok, b - pltpu.roll(b, d, 0), 0.0))
            a = jnp.sum(jnp.where(ok, q * pltpu.roll(k, d, 0) * w, 0.0), axis=-1, keepdims=True)
            oi = oi + a * pltpu.roll(v, d, 0)
        qe = _pad16(q * jnp.exp(b)).astype(BF16)
        blast = jnp.where(first, b[tn - 1:tn, :], b[rows - 1:rows, :])
        kd = k * jnp.exp(blast - b)
        v16 = _pad16(v).astype(BF16)
        o_inter = []
        for bi in range(2):
            s = s_ref[bi, h]
            o_inter.append(_dot(qe, s.astype(BF16))[0:rows])
            mine = first if bi == 0 else jnp.logical_not(first)
            kd16 = _pad16(jnp.where(mine, kd, 0.0)).astype(BF16)
            e_row = jnp.exp(b[(bi + 1) * tn - 1:(bi + 1) * tn, :])
            ecol = jnp.transpose(jnp.broadcast_to(e_row, (HG_D, HG_D)))
            so_ref[bi, h] = ecol * s + _dot_tn(kd16, v16)
        o = jnp.where(first, o_inter[0], o_inter[1]) + oi
        o_ref[:, sl] = _hgrn_finish(o, g_ref[:, sl], gn_ref[...])


def _hgrn_sample(main, lb, gnorm, state, *, tn):
    n = main.shape[0]
    nb = state.shape[0]
    assert n == nb * tn and tn == 4 and nb % 2 == 0
    w = HG_HEADS * HG_D
    rows = 2 * tn
    col = lambda base: (lambda i: (i, base // w))
    return pl.pallas_call(
        functools.partial(_hgrn_sample_body, tn=tn),
        grid=(nb // 2,),
        in_specs=[
            pl.BlockSpec((rows, w), col(C_HGQ)),
            pl.BlockSpec((rows, w), col(C_HGF)),
            pl.BlockSpec((rows, w), col(C_HGI)),
            pl.BlockSpec((rows, w), col(C_HGG)),
            pl.BlockSpec((1, w), lambda i: (0, 0)),
            pl.BlockSpec((1, HG_D), lambda i: (0, 0)),
            pl.BlockSpec((2, HG_HEADS, HG_D, HG_D), lambda i: (i, 0, 0, 0)),
        ],
        out_specs=[
            pl.BlockSpec((rows, w), lambda i: (i, 0)),
            pl.BlockSpec((2, HG_HEADS, HG_D, HG_D), lambda i: (i, 0, 0, 0)),
        ],
        out_shape=[
            jax.ShapeDtypeStruct((n, w), F32),
            jax.ShapeDtypeStruct(state.shape, F32),
        ],
        compiler_params=_cparams(("parallel",)),
        name="hgrn_sample",
    )(main, main, main, main, lb, gnorm, state)


def _alibi_slopes():
    return np.power(2.0, -8.0 * np.arange(1, NSA_HEADS + 1) / NSA_HEADS).astype(np.float32)


def _gelu_tanh(x):
    return 0.5 * x * (1.0 + jnp.tanh(np.float32(np.sqrt(2.0 / np.pi)) * (x + 0.044715 * (x * x * x))))


def _overlap_matrix(ncp, nlanes):
    cs = np.arange(ncp)[:, None] * CMP_STRIDE
    ss = np.arange(nlanes)[None, :] * SLC_BLOCK
    return ((cs <= ss + SLC_BLOCK - 1) & (cs + CMP_BLOCK - 1 >= ss)).astype(np.float32)


def _expand_matrix(nlanes, nkeys):
    return (np.arange(nkeys)[None, :] // SLC_BLOCK == np.arange(nlanes)[:, None]).astype(np.float32)


def _np_split3(x):
    parts = []
    r = np.asarray(x, np.float64)
    for _ in range(3):
        p = r.astype(np.float32).astype(BF16).astype(np.float64)
        parts.append(p.astype(np.float32))
        r = r - p
    return parts


def _softmax2_rows(s, mask):
    m = jnp.max(s, axis=-1, keepdims=True)
    e = jnp.where(mask, jnp.exp2(s - m), 0.0)
    l = jnp.sum(e, axis=-1, keepdims=True)
    return e * (1.0 / jnp.maximum(l, 1e-30))


def _select_blocks(score, qblk, n_top):
    blk = lax.broadcasted_iota(jnp.int32, score.shape, 1)
    forced = (blk == 0) | (blk == qblk) | (blk == qblk - 1)
    valid = blk <= qblk
    work = jnp.where(valid, jnp.where(forced, FORCE_SCORE, score), NEG)
    sel = jnp.zeros(score.shape, F32)
    for _ in range(n_top):
        pick = blk == jnp.argmax(work, axis=-1, keepdims=True)
        sel = jnp.where(pick, 1.0, sel)
        work = jnp.where(pick, NEG_PICKED, work)
    return jnp.where(valid, sel, 0.0)


def _compress_body(g_ref, pos_ref, w1_ref, w2_ref, o_ref):
    g = g_ref[0]
    ylo = _dot((g + pos_ref[0, 0]).astype(BF16), w1_ref[0, 0])
    yhi = _dot((g + pos_ref[0, 1]).astype(BF16), w1_ref[0, 1])
    pre = ylo + pltpu.roll(yhi, g.shape[0] - 1, 0)
    o_ref[0] = _dot(_gelu_tanh(pre).astype(BF16), w2_ref[0]).astype(BF16)


def _compress_prompt(groups, pos, w1, w2):
    na, ng, gw = groups.shape
    hid = w1.shape[-1]
    kv = lambda a: a // NSA_KV_HEADS
    return pl.pallas_call(
        _compress_body,
        grid=(na,),
        in_specs=[
            pl.BlockSpec((1, ng, gw), lambda a: (a, 0, 0)),
            pl.BlockSpec((1, 2, 1, gw), lambda a: (kv(a), 0, 0, 0)),
            pl.BlockSpec((1, 2, gw, hid), lambda a: (kv(a), 0, 0, 0)),
            pl.BlockSpec((1, hid, HEAD_DIM), lambda a: (kv(a), 0, 0)),
        ],
        out_specs=pl.BlockSpec((1, ng, HEAD_DIM), lambda a: (a, 0, 0)),
        out_shape=jax.ShapeDtypeStruct((na, ng, HEAD_DIM), BF16),
        compiler_params=_cparams(("parallel",)),
        name="compress_prompt",
    )(groups, pos, w1, w2)


SLC_CHUNK = 512
CHUNK_BLOCK_SHIFT = 3
N_SLOPE_COLS = 6
STRIP = Q_BLOCK


def _nsa_prompt_body(q_ref, gate_ref, sl_ref, kca_ref, vc_ref, ka_ref, vsa_ref, kwa_ref, vwa_ref, ov_ref,
                     o_ref, s_ref, p_ref, bias_ref, psum_ref, m_ref, alpha_ref, acc_ref, *, n, n_top):
    g_n = NSA_GROUP
    qb = Q_BLOCK
    nrow = g_n * qb
    nstrip = nrow // STRIP
    i = pl.program_id(0)
    start = i * qb
    q = q_ref[...]
    qs = jnp.concatenate([q[:, HEAD_DIM * g:HEAD_DIM * (g + 1)] for g in range(g_n)], axis=0)
    qc = jnp.concatenate([qs, sl_ref[...]], axis=1)

    def strip_rows(k):
        r0 = k * STRIP
        t0 = r0 % qb
        return slice(r0, r0 + STRIP), slice(t0, t0 + STRIP)

    def for_strips(fn):
        for k in range(nstrip):
            fn(k, None)

    def softmax_strip(s):
        m = jnp.maximum(jnp.max(s, axis=-1, keepdims=True), M_INIT)
        e = jnp.exp2(s - m)
        return e * (1.0 / jnp.maximum(jnp.sum(e, axis=-1, keepdims=True), 1e-30))

    ncp = kca_ref.shape[1]
    t_c = lax.broadcasted_iota(jnp.int32, (qb, ncp), 0)
    c_c = lax.broadcasted_iota(jnp.int32, (qb, ncp), 1)
    bias_ref[:, 0:ncp] = jnp.where(t_c - CMP_STRIDE * c_c + (start - (CMP_BLOCK - 1)) >= 0, 0.0, NEG)
    s_ref[:, 0:ncp] = _dot(qc, kca_ref[...])
    psum_ref[...] = jnp.zeros(psum_ref.shape, F32)

    def cmp_strip(k, carry):
        rows, trows = strip_rows(k)
        p = softmax_strip(s_ref[rows, 0:ncp] + bias_ref[trows, 0:ncp])
        psum_ref[trows, :] += p
        p_ref[rows, 0:ncp] = p.astype(BF16)
        return carry

    for_strips(cmp_strip)
    o_cmp = _dot(p_ref[:, 0:ncp], vc_ref[...])

    ph, plo = _split2(psum_ref[...])
    ov = ov_ref[...]
    score = _dot(ph, ov) + _dot(plo, ov)
    t_b = lax.broadcasted_iota(jnp.int32, score.shape, 0)
    qblk = jnp.right_shift(start + t_b, SLC_SHIFT)
    sel = _select_blocks(score, qblk, n_top)

    ck = SLC_CHUNK
    notsel = (1.0 - sel).astype(BF16)
    qa = jnp.concatenate([qc, jnp.concatenate([notsel] * g_n, axis=0)], axis=1)
    m_ref[...] = jnp.full(m_ref.shape, M_INIT, F32)
    acc_ref[...] = jnp.zeros(acc_ref.shape, F32)

    def attend(k0, causal):
        s_ref[:, 0:ck] = _dot(qa, ka_ref[:, pl.ds(k0, ck)])

        def strip(k, carry):
            rows, trows = strip_rows(k)
            s = s_ref[rows, 0:ck]
            if causal:
                s = s + bias_ref[trows, 0:ck]
            m_old = m_ref[rows, :]
            m_new = jnp.maximum(m_old, jnp.max(s, axis=-1, keepdims=True))
            alpha_ref[rows, :] = jnp.exp2(m_old - m_new)
            m_ref[rows, :] = m_new
            p_ref[rows, 0:ck] = jnp.exp2(s - m_new).astype(BF16)
            return carry

        for_strips(strip)
        acc_ref[...] = alpha_ref[...] * acc_ref[...] + _dot_nt(p_ref[:, 0:ck], vsa_ref[:, pl.ds(k0, ck)])

    blk_any = jnp.max(sel, axis=0, keepdims=True)
    lane = lax.broadcasted_iota(jnp.int32, blk_any.shape, 1)
    c_last = lax.div(start, ck)

    def chunk(c, carry):
        used = jnp.max(jnp.where(jnp.right_shift(lane, CHUNK_BLOCK_SHIFT) == c, blk_any, 0.0))

        @pl.when(used > 0.5)
        def _():
            attend(pl.multiple_of(c * ck, ck), causal=False)

        return carry

    lax.fori_loop(0, c_last, chunk, 0)
    k_last = pl.multiple_of(c_last * ck, ck)
    t_s = lax.broadcasted_iota(jnp.int32, (qb, ck), 0)
    s_s = lax.broadcasted_iota(jnp.int32, (qb, ck), 1)
    bias_ref[:, 0:ck] = jnp.where(t_s - s_s + (start - k_last) >= 0, 0.0, NEG)
    attend(k_last, causal=True)

    nw = WINDOW + qb
    t_w = lax.broadcasted_iota(jnp.int32, (qb, nw), 0)
    w_w = lax.broadcasted_iota(jnp.int32, (qb, nw), 1)
    dist_w = t_w + WINDOW - w_w
    mask_w = (dist_w >= 0) & (dist_w < WINDOW) & (w_w >= WINDOW - start)
    bias_ref[:, 0:nw] = jnp.where(mask_w, 0.0, NEG)
    w0 = pl.multiple_of(start, qb)
    s_ref[:, 0:nw] = _dot(qc, kwa_ref[:, pl.ds(w0, nw)])

    def win_strip(k, carry):
        rows, trows = strip_rows(k)
        s = s_ref[rows, 0:nw] + bias_ref[trows, 0:nw]
        m = jnp.max(s, axis=-1, keepdims=True)
        p_ref[rows, 0:nw] = jnp.exp2(s - m).astype(BF16)
        return carry

    for_strips(win_strip)
    win = _dot_nt(p_ref[:, 0:nw], vwa_ref[:, pl.ds(w0, nw)])

    gate = jax.nn.sigmoid(gate_ref[...])
    hd = HEAD_DIM

    def normalised(x):
        return x[:, 0:hd] * (1.0 / jnp.maximum(x[:, hd:hd + 1], 1e-30))

    outs = []
    for g in range(g_n):
        c0 = 3 * (n * g_n + g)
        rows = slice(g * qb, (g + 1) * qb)
        outs.append(gate[:, c0:c0 + 1] * o_cmp[rows] + gate[:, c0 + 1:c0 + 2] * normalised(acc_ref[rows, :])
                    + gate[:, c0 + 2:c0 + 3] * normalised(win[rows]))
    o_ref[...] = jnp.concatenate(outs, axis=-1)


def _nsa_prompt(main, q16, kvtb, kwtp, kct, vc, n):
    t = main.shape[0]
    ncp = kct.shape[1]
    ns = t // SLC_BLOCK
    assert ns <= LANE and t % SLC_CHUNK == 0
    g_n, hd = NSA_GROUP, HEAD_DIM
    nrow = g_n * Q_BLOCK
    nw = WINDOW + Q_BLOCK
    ov = jnp.asarray(_overlap_matrix(ncp, LANE), BF16)
    sl = np.zeros((g_n, Q_BLOCK, hd), np.float32)
    parts = _np_split3(_alibi_slopes()[n * g_n:(n + 1) * g_n].astype(np.float64) * LOG2E)
    for j in range(N_SLOPE_COLS):
        sl[:, :, j] = parts[j % 3][:, None]
    sl = jnp.asarray(sl.reshape(nrow, hd), BF16)

    def with_pos_rows(kt, pos):
        rows = np.zeros((hd, pos.shape[0]), np.float32)
        rows[0:3] = (pos // SLC_BLOCK) * SLC_BLOCK
        rows[3:6] = pos % SLC_BLOCK
        return jnp.concatenate([kt, jnp.asarray(rows, BF16)], axis=0)

    ka = jnp.concatenate([with_pos_rows(kvtb[R_KSLC + n * hd:R_KSLC + (n + 1) * hd], np.arange(t)),
                          jnp.asarray(NEG * _expand_matrix(LANE, t), BF16)], axis=0)
    kca = with_pos_rows(kct, np.arange(ncp) * CMP_STRIDE + (CMP_BLOCK - 1))
    kwa = with_pos_rows(kwtp[n * hd:(n + 1) * hd], np.arange(kwtp.shape[1]))

    def with_ones_row(vt):
        rows = np.zeros((hd, vt.shape[1]), np.float32)
        rows[0] = 1.0
        return jnp.concatenate([vt, jnp.asarray(rows, BF16)], axis=0)

    vsa = with_ones_row(kvtb[R_VSLC + n * hd:R_VSLC + (n + 1) * hd])
    vwa = with_ones_row(kwtp[KVW + n * hd:KVW + (n + 1) * hd])
    wq = g_n * hd
    whole = lambda a: pl.BlockSpec(a.shape, lambda i: (0,) * a.ndim)
    return pl.pallas_call(
        functools.partial(_nsa_prompt_body, n=n, n_top=min(SLC_TOP, ns)),
        grid=(t // Q_BLOCK,),
        in_specs=[
            pl.BlockSpec((Q_BLOCK, wq), lambda i: (i, n)),
            pl.BlockSpec((Q_BLOCK, LANE), lambda i: (i, C_NGATE // LANE)),
            whole(sl), whole(kca), whole(vc), whole(ka), whole(vsa), whole(kwa), whole(vwa), whole(ov),
        ],
        out_specs=pl.BlockSpec((Q_BLOCK, wq), lambda i: (i, 0)),
        out_shape=jax.ShapeDtypeStruct((t, wq), F32),
        scratch_shapes=[
            pltpu.VMEM((nrow, nw), F32),
            pltpu.VMEM((nrow, nw), BF16),
            pltpu.VMEM((Q_BLOCK, nw), F32),
            pltpu.VMEM((Q_BLOCK, ncp), F32),
            pltpu.VMEM((nrow, 1), F32),
            pltpu.VMEM((nrow, 1), F32),
            pltpu.VMEM((nrow, 2 * hd), F32),
        ],
        compiler_params=_cparams(("parallel",)),
        name=f"nsa_prompt_{n}",
    )(q16, main, sl, kca, vc, ka, vsa, kwa, vwa, ov)


N_KINDS = 4
NEW_LANES = 8
SAMPLE_PER_STEP = 2


def _nsa_sample_body(pt_ref, cache_ref, q_ref, new_ref, win_ref, gate_ref, slope_ref,
                     posk_ref, w1k_ref, w2k_ref, posv_ref, w1v_ref, w2v_ref, ov_ref, ex_ref, same_ref,
                     o_ref, wout_ref, buf_ref, rows_ref, newp_ref, sem, *, tn, n_top):
    b = pl.program_id(0)
    nsteps = pl.num_programs(0)
    nbs, npages = buf_ref.shape[1], buf_ref.shape[2]
    slot = b % 2

    def page_copy(step, sl, e, j):
        return pltpu.make_async_copy(cache_ref.at[pt_ref[step * nbs + e, j]], buf_ref.at[sl, e, j], sem.at[sl])

    def all_pages(step, sl, fn):
        for e in range(nbs):
            for j in range(npages):
                fn(page_copy(step, sl, e, j))

    @pl.when(b == 0)
    def _():
        all_pages(0, 0, lambda c: c.start())
        newp_ref[...] = jnp.zeros(newp_ref.shape, F32)

    @pl.when(b + 1 < nsteps)
    def _():
        all_pages(b + 1, 1 - slot, lambda c: c.start())

    all_pages(b, slot, lambda c: c.wait())
    for e in range(nbs):
        one = pl.ds(e, 1)
        _nsa_sample_one(buf_ref.at[slot, e], q_ref.at[one], new_ref.at[one], win_ref.at[one], gate_ref.at[one],
                        slope_ref, posk_ref, w1k_ref, w2k_ref, posv_ref, w1v_ref, w2v_ref, ov_ref, ex_ref,
                        same_ref, o_ref.at[one], wout_ref.at[one], rows_ref.at[e], newp_ref.at[e],
                        tn=tn, n_top=n_top)


def _nsa_sample_one(pg, q_ref, new_ref, win_ref, gate_ref, slope_ref,
                    posk_ref, w1k_ref, w2k_ref, posv_ref, w1v_ref, w2v_ref, ov_ref, ex_ref, same_ref,
                    o_ref, wout_ref, rows_ref, newp_ref, *, tn, n_top):
    npages, page = pg.shape[0], pg.shape[3]
    past_len = npages * page

    for a in range(4):
        newp_ref[a, :, 0:NEW_LANES] = new_ref[0, (2 + a) * KVW:(3 + a) * KVW, :]

    ncp = past_len // CMP_STRIDE
    for a in range(2):
        for j in range(npages):
            rows_ref[a, j * page:(j + 1) * page, :] = jnp.transpose(pg[j, a])
        rows_ref[a, past_len:past_len + LANE, :] = jnp.zeros((LANE, KVW), F32)

    def compress(a, pos_ref, w1_ref, w2_ref):
        acc = jnp.zeros((ncp, w1_ref.shape[2]), F32)
        for p in range(CMP_BLOCK):
            x = rows_ref[a, pl.ds(p, ncp, stride=CMP_STRIDE), :] + pos_ref[p]
            acc = acc + _dot(x.astype(BF16), w1_ref[p])
        return _dot(_gelu_tanh(acc).astype(BF16), w2_ref[...]).astype(BF16)

    kc = compress(0, posk_ref, w1k_ref, w2k_ref)
    vc = compress(1, posv_ref, w1v_ref, w2v_ref)

    rows = q_ref.shape[1]
    q = q_ref[0]
    slope = slope_ref[:, 0:1]
    r_i = lax.broadcasted_iota(jnp.int32, (rows, 1), 0)
    t_i = r_i & (tn - 1)
    qpos = past_len + t_i
    low = r_i < rows // 2

    def pick_head(o):
        return jnp.where(low, o[:, 0:HEAD_DIM], o[:, HEAD_DIM:KVW])

    cpos = CMP_STRIDE * lax.broadcasted_iota(jnp.int32, (rows, ncp), 1) + (CMP_BLOCK - 1)
    dist_c = (qpos - cpos).astype(F32)
    mask_c = dist_c >= 0.0
    s = jnp.where(mask_c, _dot_nt(q, kc) - slope * dist_c, NEG)
    p_c = _softmax2_rows(s, mask_c)
    o_cmp = pick_head(_dot(p_c.astype(BF16), vc))

    same = same_ref[...]
    ph, plo = _split2(p_c)
    psum = _dot(same, ph) + _dot(same, plo)
    ph, plo = _split2(psum)
    ov = ov_ref[...]
    score = _dot(ph, ov) + _dot(plo, ov)
    sel = _select_blocks(score, jnp.right_shift(qpos, SLC_SHIFT), n_top)

    u_i = lax.broadcasted_iota(jnp.int32, (rows, LANE), 1)
    new_ok = u_i <= t_i
    new_blk = past_len // SLC_BLOCK
    keep = _dot(sel.astype(BF16), ex_ref[...]) > 0.5
    spos = lax.broadcasted_iota(jnp.int32, (rows, past_len), 1)
    s_past = jnp.concatenate([_dot(q, pg[j, 2].astype(BF16)) for j in range(npages)], axis=1)
    s_past = jnp.where(keep, s_past - slope * (qpos - spos).astype(F32), NEG)
    keep_new = new_ok & (sel[:, new_blk:new_blk + 1] > 0.5)
    s_new = jnp.where(keep_new, _dot(q, newp_ref[0].astype(BF16)) - slope * (t_i - u_i).astype(F32), NEG)
    m = jnp.maximum(jnp.max(s_past, axis=-1, keepdims=True), jnp.max(s_new, axis=-1, keepdims=True))
    e_past = jnp.where(keep, jnp.exp2(s_past - m), 0.0)
    e_new = jnp.where(keep_new, jnp.exp2(s_new - m), 0.0)
    l = jnp.sum(e_past, axis=-1, keepdims=True) + jnp.sum(e_new, axis=-1, keepdims=True)
    e_past = e_past.astype(BF16)
    o_sel = _dot_nt(e_new.astype(BF16), newp_ref[1].astype(BF16))
    for j in range(npages):
        o_sel = o_sel + _dot_nt(e_past[:, j * page:(j + 1) * page], pg[j, 3].astype(BF16))
    o_sel = pick_head(o_sel * (1.0 / jnp.maximum(l, 1e-30)))

    wl = win_ref.shape[3]
    w_i = lax.broadcasted_iota(jnp.int32, (rows, wl), 1)
    dist_wi = (wl + t_i) - w_i
    mask_w = (dist_wi < WINDOW) & (w_i >= wl - past_len)
    s_win = jnp.where(mask_w, _dot(q, win_ref[0, 0].astype(BF16)) - slope * dist_wi.astype(F32), NEG)
    s_new = jnp.where(new_ok, _dot(q, newp_ref[2].astype(BF16)) - slope * (t_i - u_i).astype(F32), NEG)
    m = jnp.maximum(jnp.max(s_win, axis=-1, keepdims=True), jnp.max(s_new, axis=-1, keepdims=True))
    e_win = jnp.where(mask_w, jnp.exp2(s_win - m), 0.0)
    e_new = jnp.where(new_ok, jnp.exp2(s_new - m), 0.0)
    l = jnp.sum(e_win, axis=-1, keepdims=True) + jnp.sum(e_new, axis=-1, keepdims=True)
    o_win = (_dot_nt(e_win.astype(BF16), win_ref[0, 1].astype(BF16))
             + _dot_nt(e_new.astype(BF16), newp_ref[3].astype(BF16)))
    o_win = pick_head(o_win * (1.0 / jnp.maximum(l, 1e-30)))

    gate = jax.nn.sigmoid(gate_ref[0])
    o_ref[0] = gate[:, 0:1] * o_cmp + gate[:, 1:2] * o_sel + gate[:, 2:3] * o_win

    lane_w = lax.broadcasted_iota(jnp.int32, (KVW, LANE), 1)
    for a in range(2):
        shifted = pltpu.roll(win_ref[0, a], wl - tn, 1)
        tail = pltpu.roll(newp_ref[2 + a], LANE - tn, 1)
        wout_ref[0, a, :, 0:wl - LANE] = shifted[:, 0:wl - LANE]
        wout_ref[0, a, :, wl - LANE:wl] = jnp.where(lane_w >= LANE - tn, tail, shifted[:, wl - LANE:wl])


def _nsa_sample(page_table, cache, qrows, newt, win, gates, cmpk, cmpv, *, tn):
    nb, npages = page_table.shape
    page = cache.shape[3]
    past_len = npages * page
    wl = win.shape[3]
    rows = qrows.shape[1]
    assert tn == 4 and page == LANE and wl % LANE == 0 and wl > LANE and past_len >= wl
    ncp = past_len // CMP_STRIDE
    ns = -(-(past_len + tn) // SLC_BLOCK)
    assert ncp <= LANE and ns <= LANE
    r = np.arange(rows)
    h = r // tn
    slope = jnp.asarray(np.broadcast_to((_alibi_slopes()[h].astype(np.float64) * LOG2E)[:, None], (rows, LANE)), F32)
    nt = h // NSA_GROUP * tn + r % tn
    same = jnp.asarray((nt[:, None] == nt[None, :]).astype(np.float32), BF16)
    ov = jnp.asarray(_overlap_matrix(ncp, LANE), BF16)
    ex = jnp.asarray(_expand_matrix(LANE, past_len), BF16)
    posk, w1k, w2k = cmpk
    posv, w1v, w2v = cmpv
    nbs = SAMPLE_PER_STEP
    assert nb % nbs == 0
    const = lambda a: pl.BlockSpec(a.shape, lambda b, pt: (0,) * a.ndim)
    per_b = lambda a: pl.BlockSpec((nbs,) + a.shape[1:], lambda b, pt: (b,) + (0,) * (a.ndim - 1))
    grid_spec = pltpu.PrefetchScalarGridSpec(
        num_scalar_prefetch=1,
        grid=(nb // nbs,),
        in_specs=[
            pl.BlockSpec(memory_space=pl.ANY),
            per_b(qrows), per_b(newt), per_b(win), per_b(gates), const(slope),
            const(posk), const(w1k), const(w2k), const(posv), const(w1v), const(w2v),
            const(ov), const(ex), const(same),
        ],
        out_specs=[pl.BlockSpec((nbs, rows, HEAD_DIM), lambda b, pt: (b, 0, 0)), per_b(win)],
        scratch_shapes=[
            pltpu.VMEM((2, nbs, npages, N_KINDS, KVW, page), F32),
            pltpu.VMEM((nbs, 2, past_len + LANE, KVW), F32),
            pltpu.VMEM((nbs, 4, KVW, LANE), F32),
            pltpu.SemaphoreType.DMA((2,)),
        ],
    )
    return pl.pallas_call(
        functools.partial(_nsa_sample_body, tn=tn, n_top=min(SLC_TOP, ns)),
        grid_spec=grid_spec,
        out_shape=[jax.ShapeDtypeStruct((nb, rows, HEAD_DIM), F32), jax.ShapeDtypeStruct(win.shape, F32)],
        compiler_params=_cparams(("arbitrary",)),
        name="nsa_sample",
    )(page_table, cache, qrows, newt, win, gates, slope, posk, w1k, w2k, posv, w1v, w2v, ov, ex, same)


def _prep_ffn(wg, wu, wd, tf=512):
    dff = wg.shape[1]
    pad = -dff % tf
    return (jnp.pad(wg, ((0, 0), (0, pad))).astype(BF16), jnp.pad(wu, ((0, 0), (0, pad))).astype(BF16),
            jnp.pad(wd, ((0, pad), (0, 0))).astype(BF16))


def _prep_w_in(w_in):
    d = w_in.shape[0]
    cols = [w_in[:, :W_NSAQ], w_in[:, W_GA:], w_in[:, W_KV:W_KV + 2 * KVW], w_in[:, W_NGATE:W_GA]]
    used = sum(c.shape[1] for c in cols)
    w_main = jnp.concatenate(cols + [jnp.zeros((d, D_MAIN - used), w_in.dtype)], axis=1).astype(BF16)
    w_q = w_in[:, W_NSAQ:W_KV].astype(BF16)
    w_kvt = w_in[:, W_KV:W_NGATE].T.astype(BF16)
    return w_main, w_q, w_kvt


def _prep_cmp_prompt(pos_k, w1_k, w2_k, pos_v, w1_v, w2_v):
    half = CMP_STRIDE * HEAD_DIM
    pos = jnp.stack([pos_k.reshape(2, 1, half), pos_v.reshape(2, 1, half)])
    w1 = jnp.stack([w1_k.reshape(2, half, -1), w1_v.reshape(2, half, -1)]).astype(BF16)
    w2 = jnp.stack([w2_k, w2_v]).astype(BF16)
    return pos, w1, w2


def _prep_cmp_sample(pos, w1, w2):
    hid = w1.shape[1]
    w1p = w1.reshape(CMP_BLOCK, HEAD_DIM, hid)
    z1 = jnp.zeros_like(w1p)
    w1b = jnp.concatenate([jnp.concatenate([w1p, z1], axis=2), jnp.concatenate([z1, w1p], axis=2)], axis=1)
    z2 = jnp.zeros_like(w2)
    w2b = jnp.concatenate([jnp.concatenate([w2, z2], axis=1), jnp.concatenate([z2, w2], axis=1)], axis=0)
    posb = jnp.concatenate([pos, pos], axis=1)[:, None, :]
    return posb, w1b.astype(BF16), w2b.astype(BF16)


def _project(x, g, w_main, w_q, w_kvt):
    main = _inproj(x, g, w_main)
    q16 = _inproj(x, g, w_q, out_dtype=BF16, scale=HEAD_DIM ** -0.5 * LOG2E)
    kvt, kvtb = _inproj_t(x, g, w_kvt)
    return main, q16, kvt, kvtb


def _mix_prompt(main, q16, kvtb, lb, gnorm, cmp_w):
    t = main.shape[0]
    o_hg, s_fin = _hgrn_prompt(main, lb, gnorm)
    nkv, hd = NSA_KV_HEADS, HEAD_DIM
    groups = main[:, C_CMP:C_CMP + 2 * KVW].reshape(t // CMP_STRIDE, CMP_STRIDE, 2, nkv, hd)
    groups = groups.transpose(2, 3, 0, 1, 4).reshape(2 * nkv, t // CMP_STRIDE, CMP_STRIDE * hd)
    cmp = _compress_prompt(groups, *cmp_w)
    kct = cmp[:nkv].transpose(0, 2, 1)
    vc = cmp[nkv:]
    kwtp = jnp.pad(kvtb[R_KWIN:N_KV_ROWS], ((0, 0), (WINDOW, 0)))
    o_nsa = [_nsa_prompt(main, q16, kvtb, kwtp, kct[n], vc[n], n) for n in range(nkv)]
    return o_hg, o_nsa, s_fin


def _mix_sample(main, q16, kvt, lb, gnorm, state, cache, win, page_table, cmp_w, *, tn):
    nb = state.shape[0]
    o_hg, s_new = _hgrn_sample(main, lb, gnorm, state, tn=tn)
    g_n, nkv, hd = NSA_GROUP, NSA_KV_HEADS, HEAD_DIM
    q = q16.reshape(nb, tn, nkv, g_n, hd).transpose(0, 2, 3, 1, 4)
    eye = jnp.eye(nkv, dtype=BF16)
    qrows = (q[:, :, :, :, None, :] * eye[None, :, None, None, :, None]).reshape(nb, nkv * g_n * tn, nkv * hd)
    gates = main[:, C_NGATE:C_NGATE + N_NGATE].reshape(nb, tn, NSA_HEADS, 3).transpose(0, 2, 1, 3)
    gates = jnp.pad(gates.reshape(nb, NSA_HEADS * tn, 3), ((0, 0), (0, 0), (0, LANE - 3)))
    newt = jnp.pad(kvt.reshape(N_KV_ROWS, nb, tn).transpose(1, 0, 2), ((0, 0), (0, 0), (0, NEW_LANES - tn)))
    o, wout = _nsa_sample(page_table, cache, qrows, newt, win, gates, *cmp_w, tn=tn)
    o = o.reshape(nb, nkv, g_n, tn, hd).transpose(0, 3, 1, 2, 4).reshape(nb * tn, nkv, g_n * hd)
    return o_hg, [o[:, n] for n in range(nkv)], s_new, wout


def kernel(x_prompt, x_sample, cache_kv, cache_win, state_hgrn, page_table, norm_pre1, norm_post1, ff1_gate, ff1_up, ff1_down, norm_pre2, norm_post2, w_in, hg_lb, hg_gnorm, cmp_pos_k, cmp_w1_k, cmp_w2_k, cmp_pos_v, cmp_w1_v, cmp_w2_v, w_proj_hg, w_proj_nsa, w_out, norm_pre3, norm_post3, ff2_gate, ff2_up, ff2_down):
    depth = norm_pre1.shape[0]
    bp, tp, d = x_prompt.shape
    nb, tn, _ = x_sample.shape
    assert bp == 1
    nkv, hd = NSA_KV_HEADS, HEAD_DIM
    lb_all = jnp.cumsum(jax.nn.softmax(hg_lb.astype(F32), axis=0), axis=0)[:depth]
    xp = x_prompt.reshape(tp, d)
    xs = x_sample.reshape(nb * tn, d)
    kv_p, kv_s, win_p, win_s, st_p, st_s = [], [], [], [], [], []
    row = lambda a: a.reshape(1, -1)
    for l in range(depth):
        ff1 = _prep_ffn(ff1_gate[l], ff1_up[l], ff1_down[l])
        ff2 = _prep_ffn(ff2_gate[l], ff2_up[l], ff2_down[l])
        w_in_l = _prep_w_in(w_in[l])
        wh, wn, wo = w_proj_hg[l].astype(BF16), w_proj_nsa[l].astype(BF16), w_out[l].astype(BF16)
        cmp_p = _prep_cmp_prompt(cmp_pos_k[l], cmp_w1_k[l], cmp_w2_k[l], cmp_pos_v[l], cmp_w1_v[l], cmp_w2_v[l])
        cmp_s = (_prep_cmp_sample(cmp_pos_k[l], cmp_w1_k[l], cmp_w2_k[l]),
                 _prep_cmp_sample(cmp_pos_v[l], cmp_w1_v[l], cmp_w2_v[l]))
        lb, gnorm = row(lb_all[l]), row(hg_gnorm[l])
        n_pool, page = cache_kv.shape[1], cache_kv.shape[2]
        cache = cache_kv[l].transpose(0, 2, 3, 4, 1).reshape(n_pool, N_KINDS, KVW, page)
        wl = cache_win.shape[2]
        win = cache_win[l].transpose(0, 2, 3, 4, 1).reshape(nb, 2, KVW, wl)

        xp = _ffn(xp, row(norm_pre1[l]), row(norm_post1[l]), *ff1)
        xs = _ffn(xs, row(norm_pre1[l]), row(norm_post1[l]), *ff1)

        main_p, q_p, kvt_p, kvtb_p = _project(xp, row(norm_pre2[l]), *w_in_l)
        main_s, q_s, kvt_s, _ = _project(xs, row(norm_pre2[l]), *w_in_l)
        ohg_p, onsa_p, s_p = _mix_prompt(main_p, q_p, kvtb_p, lb, gnorm, cmp_p)
        ohg_s, onsa_s, s_s, wout = _mix_sample(main_s, q_s, kvt_s, lb, gnorm, state_hgrn[l], cache, win,
                                               page_table, cmp_s, tn=tn)
        xp = _outproj(_merge(ohg_p, *onsa_p, main_p, wh, wn), xp, wo, row(norm_post2[l]))
        xs = _outproj(_merge(ohg_s, *onsa_s, main_s, wh, wn), xs, wo, row(norm_post2[l]))

        kv_p.append(kvt_p[:N_KINDS * KVW].reshape(N_KINDS, nkv, hd, bp, tp).transpose(3, 4, 0, 1, 2))
        kv_s.append(kvt_s[:N_KINDS * KVW].reshape(N_KINDS, nkv, hd, nb, tn).transpose(3, 4, 0, 1, 2))
        wp = min(WINDOW, tp)
        win_p.append(kvt_p[R_KWIN:, tp - wp:].reshape(2, nkv, hd, bp, wp).transpose(3, 4, 0, 1, 2))
        win_s.append(wout.reshape(nb, 2, nkv, hd, wl).transpose(0, 4, 1, 2, 3))
        st_p.append(s_p[None])
        st_s.append(s_s)

        xp = _ffn(xp, row(norm_pre3[l]), row(norm_post3[l]), *ff2)
        xs = _ffn(xs, row(norm_pre3[l]), row(norm_post3[l]), *ff2)
    return (xp.reshape(bp, tp, d), xs.reshape(nb, tn, d), jnp.stack(kv_p), jnp.stack(kv_s),
            jnp.stack(win_p), jnp.stack(win_s), jnp.stack(st_p), jnp.stack(st_s))
```

```python
import functools

import numpy as np
import jax
import jax.numpy as jnp
from jax import lax
from jax.experimental import pallas as pl
from jax.experimental.pallas import tpu as pltpu

F32 = jnp.float32
BF16 = jnp.bfloat16

EPS = 1e-6
HG_HEADS = 8
HG_D = 128
NSA_HEADS = 16
NSA_KV_HEADS = 2
NSA_GROUP = NSA_HEADS // NSA_KV_HEADS
HEAD_DIM = 64
CMP_BLOCK = 32
CMP_STRIDE = 16
SLC_BLOCK = 64
SLC_SHIFT = 6
SLC_TOP = 16
WINDOW = 512
Q_BLOCK = 128
FORCE_SCORE = 1.0e4
NEG = -1.0e30
NEG_PICKED = -3.0e38
M_INIT = -1.0e20
LOG2E = float(np.log2(np.e))

LANE = 128
VMEM_LIMIT = 56 * 1024 * 1024

W_NSAQ, W_KV, W_NGATE, W_GA = 4096, 5120, 5888, 5936
N_NGATE = 3 * NSA_HEADS
KVW = NSA_KV_HEADS * HEAD_DIM
C_HGQ, C_HGF, C_HGI, C_HGG = 0, 1024, 2048, 3072
C_GA, C_GB = 4096, 6144
C_CMP = 8192
C_NGATE = 8448
D_MAIN = 8704
R_KSLC, R_VSLC, R_KWIN, R_VWIN = 2 * KVW, 3 * KVW, 4 * KVW, 5 * KVW
N_KV_ROWS = 6 * KVW


def _cparams(sem):
    return pltpu.CompilerParams(dimension_semantics=sem, vmem_limit_bytes=VMEM_LIMIT)


def _rms(x, g):
    return x * lax.rsqrt(jnp.mean(x * x, axis=-1, keepdims=True) + EPS) * g


def _silu(x):
    return x * jax.nn.sigmoid(x)


def _dot(a, b):
    return jnp.dot(a, b, preferred_element_type=F32)


def _dot_nt(a, b):
    return lax.dot_general(a, b, (((1,), (1,)), ((), ())), preferred_element_type=F32)


def _dot_tn(a, b):
    return lax.dot_general(a, b, (((0,), (0,)), ((), ())), preferred_element_type=F32)


def _split3(x):
    hi = x.astype(BF16)
    r = x - hi.astype(F32)
    mid = r.astype(BF16)
    lo = (r - mid.astype(F32)).astype(BF16)
    return hi, mid, lo


def _split2(x):
    hi = x.astype(BF16)
    return hi, (x - hi.astype(F32)).astype(BF16)


def _ffn_body(x_ref, npre_ref, npost_ref, wg_ref, wu_ref, wd_ref, o_ref, h_ref, acc_ref):
    f = pl.program_id(1)

    @pl.when(f == 0)
    def _():
        h_ref[...] = _rms(x_ref[...], npre_ref[...]).astype(BF16)
        acc_ref[...] = jnp.zeros_like(acc_ref)

    h = h_ref[...]
    a = _silu(_dot(h, wg_ref[...])) * _dot(h, wu_ref[...])
    acc_ref[...] += _dot(a.astype(BF16), wd_ref[...])

    @pl.when(f == pl.num_programs(1) - 1)
    def _():
        o_ref[...] = x_ref[...] + 0.5 * _rms(acc_ref[...], npost_ref[...])


def _ffn(x, npre, npost, wg, wu, wd, *, tm=512, tf=512):
    n, d = x.shape
    dff = wg.shape[1]
    tm = min(tm, n)
    return pl.pallas_call(
        _ffn_body,
        grid=(n // tm, dff // tf),
        in_specs=[
            pl.BlockSpec((tm, d), lambda i, f: (i, 0)),
            pl.BlockSpec((1, d), lambda i, f: (0, 0)),
            pl.BlockSpec((1, d), lambda i, f: (0, 0)),
            pl.BlockSpec((d, tf), lambda i, f: (0, f)),
            pl.BlockSpec((d, tf), lambda i, f: (0, f)),
            pl.BlockSpec((tf, d), lambda i, f: (f, 0)),
        ],
        out_specs=pl.BlockSpec((tm, d), lambda i, f: (i, 0)),
        out_shape=jax.ShapeDtypeStruct((n, d), F32),
        scratch_shapes=[pltpu.VMEM((tm, d), BF16), pltpu.VMEM((tm, d), F32)],
        compiler_params=_cparams(("parallel", "arbitrary")),
        name="ffn",
    )(x, npre, npost, wg, wu, wd)


def _inproj_body(x_ref, g_ref, w_ref, o_ref, h_ref, *, scale):
    @pl.when(pl.program_id(1) == 0)
    def _():
        h_ref[...] = _rms(x_ref[...], g_ref[...]).astype(BF16)

    o_ref[...] = (_dot(h_ref[...], w_ref[...]) * scale).astype(o_ref.dtype)


def _inproj(x, g, w, *, out_dtype=F32, scale=1.0, tm=512, tn=512):
    n, d = x.shape
    dout = w.shape[1]
    tm = min(tm, n)
    return pl.pallas_call(
        functools.partial(_inproj_body, scale=scale),
        grid=(n // tm, dout // tn),
        in_specs=[
            pl.BlockSpec((tm, d), lambda i, j: (i, 0)),
            pl.BlockSpec((1, d), lambda i, j: (0, 0)),
            pl.BlockSpec((d, tn), lambda i, j: (0, j)),
        ],
        out_specs=pl.BlockSpec((tm, tn), lambda i, j: (i, j)),
        out_shape=jax.ShapeDtypeStruct((n, dout), out_dtype),
        scratch_shapes=[pltpu.VMEM((tm, d), BF16)],
        compiler_params=_cparams(("parallel", "arbitrary")),
        name="inproj",
    )(x, g, w)


def _inproj_t_body(x_ref, g_ref, wt_ref, o_ref, ob_ref):
    y = _dot_nt(wt_ref[...], _rms(x_ref[...], g_ref[...]).astype(BF16))
    o_ref[...] = y
    ob_ref[...] = y.astype(BF16)


def _inproj_t(x, g, wt, *, tm=512):
    n, d = x.shape
    dout = wt.shape[0]
    tm = min(tm, n)
    return pl.pallas_call(
        _inproj_t_body,
        grid=(n // tm,),
        in_specs=[
            pl.BlockSpec((tm, d), lambda i: (i, 0)),
            pl.BlockSpec((1, d), lambda i: (0, 0)),
            pl.BlockSpec((dout, d), lambda i: (0, 0)),
        ],
        out_specs=[pl.BlockSpec((dout, tm), lambda i: (0, i)), pl.BlockSpec((dout, tm), lambda i: (0, i))],
        out_shape=[jax.ShapeDtypeStruct((dout, n), F32), jax.ShapeDtypeStruct((dout, n), BF16)],
        compiler_params=_cparams(("parallel",)),
        name="inproj_t",
    )(x, g, wt)


def _merge_body(ohg_ref, on0_ref, on1_ref, ga_ref, gb_ref, wh_ref, wn_ref, y_ref):
    onsa = jnp.concatenate([on0_ref[...], on1_ref[...]], axis=-1).astype(BF16)
    yh = _dot(ohg_ref[...].astype(BF16), wh_ref[...])
    yn = _dot(onsa, wn_ref[...])
    y = jax.nn.sigmoid(ga_ref[...]) * yh + jax.nn.sigmoid(gb_ref[...]) * yn
    y_ref[...] = y.astype(BF16)


def _merge(ohg, on0, on1, main, wh, wn, *, tm=512):
    n = ohg.shape[0]
    d = wh.shape[1]
    tm = min(tm, n)
    return pl.pallas_call(
        _merge_body,
        grid=(n // tm,),
        in_specs=[
            pl.BlockSpec((tm, ohg.shape[1]), lambda i: (i, 0)),
            pl.BlockSpec((tm, on0.shape[1]), lambda i: (i, 0)),
            pl.BlockSpec((tm, on1.shape[1]), lambda i: (i, 0)),
            pl.BlockSpec((tm, d), lambda i: (i, C_GA // d)),
            pl.BlockSpec((tm, d), lambda i: (i, C_GB // d)),
            pl.BlockSpec(wh.shape, lambda i: (0, 0)),
            pl.BlockSpec(wn.shape, lambda i: (0, 0)),
        ],
        out_specs=pl.BlockSpec((tm, d), lambda i: (i, 0)),
        out_shape=jax.ShapeDtypeStruct((n, d), BF16),
        compiler_params=_cparams(("parallel",)),
        name="merge",
    )(ohg, on0, on1, main, main, wh, wn)


def _outproj_body(y_ref, x_ref, w_ref, g_ref, o_ref):
    o_ref[...] = x_ref[...] + _rms(_dot(y_ref[...], w_ref[...]), g_ref[...])


def _outproj(y, x, w, g, *, tm=512):
    n, d = x.shape
    tm = min(tm, n)
    return pl.pallas_call(
        _outproj_body,
        grid=(n // tm,),
        in_specs=[
            pl.BlockSpec((tm, d), lambda i: (i, 0)),
            pl.BlockSpec((tm, d), lambda i: (i, 0)),
            pl.BlockSpec(w.shape, lambda i: (0, 0)),
            pl.BlockSpec((1, d), lambda i: (0, 0)),
        ],
        out_specs=pl.BlockSpec((tm, d), lambda i: (i, 0)),
        out_shape=jax.ShapeDtypeStruct((n, d), F32),
        compiler_params=_cparams(("parallel",)),
        name="outproj",
    )(y, x, w, g)


def _hgrn_range_matrix(c):
    t = np.arange(c)[:, None]
    u = np.arange(c)[None, :]
    blocks = [u <= t, u > t]
    m = c // 2
    while m >= 1:
        p = (t // (2 * m)) * 2 * m + m - 1
        upper = (t // m) % 2 == 1
        blocks.append(np.where(upper, (u > p) & (u <= t), (u > t) & (u <= p)))
        m //= 2
    return np.concatenate(blocks, axis=0).astype(np.float32)


def _hgrn_gates(q_raw, f_raw, lb):
    f = lb + (1.0 - lb) * jax.nn.sigmoid(f_raw)
    return _silu(q_raw), jnp.log(f), 1.0 - f


def _hgrn_finish(o, g_raw, gnorm):
    return _rms(o, gnorm) * _silu(g_raw)


def _hgrn_prompt_body(q_ref, f_ref, i_ref, g_ref, lb_ref, gn_ref, rm_ref, o_ref, sfin_ref, s_ref, *, c, hp):
    @pl.when(pl.program_id(1) == 0)
    def _():
        s_ref[...] = jnp.zeros(s_ref.shape, F32)

    cols = [slice(j * HG_D, (j + 1) * HG_D) for j in range(hp)]
    res = [_hgrn_chunk(q_ref[:, sl], f_ref[:, sl], i_ref[:, sl], lb_ref[:, sl], rm_ref[...], s_ref[j], c)
           for j, sl in enumerate(cols)]
    outs = [_hgrn_finish(o, g_ref[:, sl], gn_ref[...]) for (o, _), sl in zip(res, cols)]
    for j, sl in enumerate(cols):
        s_ref[j] = res[j][1]
        sfin_ref[j] = res[j][1]
        o_ref[:, sl] = outs[j]


def _hgrn_chunk(q_raw, f_raw, v, lb, rm, s, c):
    q, logf, k = _hgrn_gates(q_raw, f_raw, lb)
    vb = v.astype(BF16)
    hi, mid, lo = _split3(logf)
    e = _dot(rm, hi) + _dot(rm, mid) + _dot(rm, lo)
    b = e[0:c]
    o = _dot((q * jnp.exp(b)).astype(BF16), s.astype(BF16))
    row = lax.broadcasted_iota(jnp.int32, (c, c), 0)
    col = lax.broadcasted_iota(jnp.int32, (c, c), 1)
    x = row ^ col
    a = jnp.where(x == 0, _dot_nt(q.astype(BF16), k.astype(BF16)), 0.0)
    rowk = lax.broadcasted_iota(jnp.int32, (c, HG_D), 0)
    m = c // 2
    lvl = 0
    while m >= 1:
        w = jnp.exp(e[(2 + lvl) * c:(3 + lvl) * c])
        upper = (rowk & m) != 0
        ql = jnp.where(upper, q * w, 0.0).astype(BF16)
        kl = jnp.where(upper, 0.0, k * w).astype(BF16)
        a = a + jnp.where(x < 2 * m, _dot_nt(ql, kl), 0.0)
        m //= 2
        lvl += 1
    o = o + _dot(a.astype(BF16), vb)
    kd = (k * jnp.exp(e[c:2 * c])).astype(BF16)
    ecol = jnp.transpose(jnp.broadcast_to(jnp.exp(b[c - 1:c, :]), (HG_D, HG_D)))
    return o, ecol * s + _dot_tn(kd, vb)


def _hgrn_prompt(main, lb, gnorm, *, c=128, hp=8):
    t = main.shape[0]
    assert c == HG_D and HG_HEADS % hp == 0
    rm = jnp.asarray(_hgrn_range_matrix(c), BF16)
    w = hp * HG_D
    col = lambda base: (lambda h, ci: (ci, base // w + h))
    return pl.pallas_call(
        functools.partial(_hgrn_prompt_body, c=c, hp=hp),
        grid=(HG_HEADS // hp, t // c),
        in_specs=[
            pl.BlockSpec((c, w), col(C_HGQ)),
            pl.BlockSpec((c, w), col(C_HGF)),
            pl.BlockSpec((c, w), col(C_HGI)),
            pl.BlockSpec((c, w), col(C_HGG)),
            pl.BlockSpec((1, w), lambda h, ci: (0, h)),
            pl.BlockSpec((1, HG_D), lambda h, ci: (0, 0)),
            pl.BlockSpec(rm.shape, lambda h, ci: (0, 0)),
        ],
        out_specs=[
            pl.BlockSpec((c, w), lambda h, ci: (ci, h)),
            pl.BlockSpec((hp, HG_D, HG_D), lambda h, ci: (h, 0, 0)),
        ],
        out_shape=[
            jax.ShapeDtypeStruct((t, HG_HEADS * HG_D), F32),
            jax.ShapeDtypeStruct((HG_HEADS, HG_D, HG_D), F32),
        ],
        scratch_shapes=[pltpu.VMEM((hp, HG_D, HG_D), F32)],
        compiler_params=_cparams(("parallel", "arbitrary")),
        name="hgrn_prompt",
    )(main, main, main, main, lb, gnorm, rm)


def _pad16(x):
    return jnp.concatenate([x, jnp.zeros_like(x)], axis=0)


def _hgrn_sample_body(q_ref, f_ref, i_ref, g_ref, lb_ref, gn_ref, s_ref, o_ref, so_ref, *, tn):
    rows = 2 * tn
    row = lax.broadcasted_iota(jnp.int32, (rows, HG_D), 0)
    pos = row & (tn - 1)
    first = row < tn
    for h in range(HG_HEADS):
        sl = slice(h * HG_D, (h + 1) * HG_D)
        q, logf, k = _hgrn_gates(q_ref[:, sl], f_ref[:, sl], lb_ref[:, sl])
        v = i_ref[:, sl]
        b = logf
        d = 1
        while d < tn:
            b = b + jnp.where(pos >= d, pltpu.roll(b, d, 0), 0.0)
            d *= 2
        oi = jnp.sum(q * k, axis=-1, keepdims=True) * v
        for d in range(1, tn):
            ok = pos >= d
            w = jnp.exp(jnp.where(ok, b - pltpu.roll(b, d, 0), 0.0))
            a = jnp.sum(jnp.where(ok, q * pltpu.roll(k, d, 0) * w, 0.0), axis=-1, keepdims=True)
            oi = oi + a * pltpu.roll(v, d, 0)
        qe = _pad16(q * jnp.exp(b)).astype(BF16)
        blast = jnp.where(first, b[tn - 1:tn, :], b[rows - 1:rows, :])
        kd = k * jnp.exp(blast - b)
        v16 = _pad16(v).astype(BF16)
        o_inter = []
        for bi in range(2):
            s = s_ref[bi, h]
            o_inter.append(_dot(qe, s.astype(BF16))[0:rows])
            mine = first if bi == 0 else jnp.logical_not(first)
            kd16 = _pad16(jnp.where(mine, kd, 0.0)).astype(BF16)
            e_row = jnp.exp(b[(bi + 1) * tn - 1:(bi + 1) * tn, :])
            ecol = jnp.transpose(jnp.broadcast_to(e_row, (HG_D, HG_D)))
            so_ref[bi, h] = ecol * s + _dot_tn(kd16, v16)
        o = jnp.where(first, o_inter[0], o_inter[1]) + oi
        o_ref[:, sl] = _hgrn_finish(o, g_ref[:, sl], gn_ref[...])


def _hgrn_sample(main, lb, gnorm, state, *, tn):
    n = main.shape[0]
    nb = state.shape[0]
    assert n == nb * tn and tn == 4 and nb % 2 == 0
    w = HG_HEADS * HG_D
    rows = 2 * tn
    col = lambda base: (lambda i: (i, base // w))
    return pl.pallas_call(
        functools.partial(_hgrn_sample_body, tn=tn),
        grid=(nb // 2,),
        in_specs=[
            pl.BlockSpec((rows, w), col(C_HGQ)),
            pl.BlockSpec((rows, w), col(C_HGF)),
            pl.BlockSpec((rows, w), col(C_HGI)),
            pl.BlockSpec((rows, w), col(C_HGG)),
            pl.BlockSpec((1, w), lambda i: (0, 0)),
            pl.BlockSpec((1, HG_D), lambda i: (0, 0)),
            pl.BlockSpec((2, HG_HEADS, HG_D, HG_D), lambda i: (i, 0, 0, 0)),
        ],
        out_specs=[
            pl.BlockSpec((rows, w), lambda i: (i, 0)),
            pl.BlockSpec((2, HG_HEADS, HG_D, HG_D), lambda i: (i, 0, 0, 0)),
        ],
        out_shape=[
            jax.ShapeDtypeStruct((n, w), F32),
            jax.ShapeDtypeStruct(state.shape, F32),
        ],
        compiler_params=_cparams(("parallel",)),
        name="hgrn_sample",
    )(main, main, main, main, lb, gnorm, state)


def _alibi_slopes():
    return np.power(2.0, -8.0 * np.arange(1, NSA_HEADS + 1) / NSA_HEADS).astype(np.float32)


def _gelu_tanh(x):
    return 0.5 * x * (1.0 + jnp.tanh(np.float32(np.sqrt(2.0 / np.pi)) * (x + 0.044715 * (x * x * x))))


def _overlap_matrix(ncp, nlanes):
    cs = np.arange(ncp)[:, None] * CMP_STRIDE
    ss = np.arange(nlanes)[None, :] * SLC_BLOCK
    return ((cs <= ss + SLC_BLOCK - 1) & (cs + CMP_BLOCK - 1 >= ss)).astype(np.float32)


def _expand_matrix(nlanes, nkeys):
    return (np.arange(nkeys)[None, :] // SLC_BLOCK == np.arange(nlanes)[:, None]).astype(np.float32)


def _np_split3(x):
    parts = []
    r = np.asarray(x, np.float64)
    for _ in range(3):
        p = r.astype(np.float32).astype(BF16).astype(np.float64)
        parts.append(p.astype(np.float32))
        r = r - p
    return parts


def _softmax2_rows(s, mask):
    m = jnp.max(s, axis=-1, keepdims=True)
    e = jnp.where(mask, jnp.exp2(s - m), 0.0)
    l = jnp.sum(e, axis=-1, keepdims=True)
    return e * (1.0 / jnp.maximum(l, 1e-30))


def _select_blocks(score, qblk, n_top):
    blk = lax.broadcasted_iota(jnp.int32, score.shape, 1)
    forced = (blk == 0) | (blk == qblk) | (blk == qblk - 1)
    valid = blk <= qblk
    work = jnp.where(valid, jnp.where(forced, FORCE_SCORE, score), NEG)
    nrow, nblk = score.shape
    if nrow < nblk:
        work = jnp.concatenate([work, jnp.full((nblk - nrow, nblk), NEG, F32)], axis=0)
    wt = jnp.transpose(work)
    blk_t = lax.broadcasted_iota(jnp.int32, wt.shape, 0)
    sel_t = jnp.zeros(wt.shape, F32)
    for _ in range(n_top):
        m = jnp.max(wt, axis=0, keepdims=True)
        first = jnp.min(jnp.where(wt == m, blk_t, nblk), axis=0, keepdims=True)
        pick = blk_t == first
        sel_t = jnp.where(pick, 1.0, sel_t)
        wt = jnp.where(pick, NEG_PICKED, wt)
    return jnp.where(valid, jnp.transpose(sel_t)[0:nrow], 0.0)


def _compress_body(g_ref, pos_ref, w1_ref, w2_ref, o_ref):
    g = g_ref[0]
    ylo = _dot((g + pos_ref[0, 0]).astype(BF16), w1_ref[0, 0])
    yhi = _dot((g + pos_ref[0, 1]).astype(BF16), w1_ref[0, 1])
    pre = ylo + pltpu.roll(yhi, g.shape[0] - 1, 0)
    o_ref[0] = _dot(_gelu_tanh(pre).astype(BF16), w2_ref[0]).astype(BF16)


def _compress_prompt(groups, pos, w1, w2):
    na, ng, gw = groups.shape
    hid = w1.shape[-1]
    kv = lambda a: a // NSA_KV_HEADS
    return pl.pallas_call(
        _compress_body,
        grid=(na,),
        in_specs=[
            pl.BlockSpec((1, ng, gw), lambda a: (a, 0, 0)),
            pl.BlockSpec((1, 2, 1, gw), lambda a: (kv(a), 0, 0, 0)),
            pl.BlockSpec((1, 2, gw, hid), lambda a: (kv(a), 0, 0, 0)),
            pl.BlockSpec((1, hid, HEAD_DIM), lambda a: (kv(a), 0, 0)),
        ],
        out_specs=pl.BlockSpec((1, ng, HEAD_DIM), lambda a: (a, 0, 0)),
        out_shape=jax.ShapeDtypeStruct((na, ng, HEAD_DIM), BF16),
        compiler_params=_cparams(("parallel",)),
        name="compress_prompt",
    )(groups, pos, w1, w2)


SLC_CHUNK = 512
CHUNK_BLOCK_SHIFT = 3
N_SLOPE_COLS = 6
STRIP = Q_BLOCK


def _nsa_prompt_body(q_ref, gate_ref, sl_ref, kca_ref, vc_ref, ka_ref, vsa_ref, kwa_ref, vwa_ref, ov_ref,
                     o_ref, s_ref, p_ref, bias_ref, psum_ref, sw_ref, pw_ref, bw_ref, s2_ref, mx_ref, bias2_ref,
                     m_ref, alpha_ref, acc_ref, todo_ref, *, n, n_top):
    g_n = NSA_GROUP
    qb = Q_BLOCK
    nrow = g_n * qb
    nstrip = nrow // STRIP
    i = pl.program_id(0)
    start = i * qb
    q = q_ref[...]
    qs = jnp.concatenate([q[:, HEAD_DIM * g:HEAD_DIM * (g + 1)] for g in range(g_n)], axis=0)
    qc = jnp.concatenate([qs, sl_ref[...]], axis=1)

    def strip_rows(k):
        r0 = k * STRIP
        t0 = r0 % qb
        return slice(r0, r0 + STRIP), slice(t0, t0 + STRIP)

    def for_strips(fn):
        for k in range(nstrip):
            fn(k, None)

    def softmax_strip(s):
        m = jnp.maximum(jnp.max(s, axis=-1, keepdims=True), M_INIT)
        e = jnp.exp2(s - m)
        return e * (1.0 / jnp.maximum(jnp.sum(e, axis=-1, keepdims=True), 1e-30))

    ncp = kca_ref.shape[1]
    t_c = lax.broadcasted_iota(jnp.int32, (qb, ncp), 0)
    c_c = lax.broadcasted_iota(jnp.int32, (qb, ncp), 1)
    bias_ref[:, 0:ncp] = jnp.where(t_c - CMP_STRIDE * c_c + (start - (CMP_BLOCK - 1)) >= 0, 0.0, NEG)
    s_ref[:, 0:ncp] = _dot(qc, kca_ref[...])
    psum_ref[...] = jnp.zeros(psum_ref.shape, F32)

    def cmp_strip(k, carry):
        rows, trows = strip_rows(k)
        p = softmax_strip(s_ref[rows, 0:ncp] + bias_ref[trows, 0:ncp])
        psum_ref[trows, :] += p
        p_ref[rows, 0:ncp] = p.astype(BF16)
        return carry

    for_strips(cmp_strip)
    o_cmp = _dot(p_ref[:, 0:ncp], vc_ref[...])

    nw = WINDOW + qb
    t_w = lax.broadcasted_iota(jnp.int32, (qb, nw), 0)
    w_w = lax.broadcasted_iota(jnp.int32, (qb, nw), 1)
    dist_w = t_w + WINDOW - w_w
    mask_w = (dist_w >= 0) & (dist_w < WINDOW) & (w_w >= WINDOW - start)
    bw_ref[...] = jnp.where(mask_w, 0.0, NEG)
    w0 = pl.multiple_of(start, qb)
    sw_ref[...] = _dot(qc, kwa_ref[:, pl.ds(w0, nw)])

    def win_strip(k, carry):
        rows, trows = strip_rows(k)
        s = sw_ref[rows, :] + bw_ref[trows, :]
        m = jnp.max(s, axis=-1, keepdims=True)
        pw_ref[rows, :] = jnp.exp2(s - m).astype(BF16)
        return carry

    for_strips(win_strip)
    win = _dot_nt(pw_ref[...], vwa_ref[:, pl.ds(w0, nw)])

    ph, plo = _split2(psum_ref[...])
    ov = ov_ref[...]
    score = _dot(ph, ov) + _dot(plo, ov)
    t_b = lax.broadcasted_iota(jnp.int32, score.shape, 0)
    qblk = jnp.right_shift(start + t_b, SLC_SHIFT)
    sel = _select_blocks(score, qblk, n_top)

    ck = SLC_CHUNK
    notsel = (1.0 - sel).astype(BF16)
    qa = jnp.concatenate([qc, jnp.concatenate([notsel] * g_n, axis=0)], axis=1)
    m_ref[...] = jnp.full(m_ref.shape, M_INIT, F32)
    acc_ref[...] = jnp.zeros(acc_ref.shape, F32)
    c_last = lax.div(start, ck)
    t_s = lax.broadcasted_iota(jnp.int32, (qb, ck), 0)
    s_s = lax.broadcasted_iota(jnp.int32, (qb, ck), 1)
    bias2_ref[0] = jnp.zeros((qb, ck), F32)
    bias2_ref[1] = jnp.where(t_s - s_s + (start - c_last * ck) >= 0, 0.0, NEG)

    blk_any = jnp.max(sel, axis=0, keepdims=True)
    lane = lax.broadcasted_iota(jnp.int32, blk_any.shape, 1)
    cnt = jnp.int32(0)
    for c in range(todo_ref.shape[0] - 2):
        used = jnp.max(jnp.where(jnp.right_shift(lane, CHUNK_BLOCK_SHIFT) == c, blk_any, 0.0)) > 0.5
        todo_ref[cnt] = c
        cnt = cnt + jnp.logical_and(used, c < c_last).astype(jnp.int32)
    todo_ref[cnt] = c_last
    todo_ref[cnt + 1] = c_last

    def scores(c, slot):
        k0 = pl.multiple_of(c * ck, ck)
        last = (c == c_last).astype(jnp.int32)
        s = _dot(qa, ka_ref[:, pl.ds(k0, ck)])
        for k in range(nstrip):
            rows, _ = strip_rows(k)
            sk = s[rows] + bias2_ref[last]
            s2_ref[slot, rows, :] = sk
            mx_ref[slot, rows, :] = jnp.broadcast_to(jnp.max(sk, axis=-1, keepdims=True), (qb, LANE))

    def accumulate(c, slot):
        k0 = pl.multiple_of(c * ck, ck)
        for k in range(nstrip):
            rows, _ = strip_rows(k)
            m_old = m_ref[rows, :]
            m_new = jnp.maximum(m_old, mx_ref[slot, rows, :])
            alpha_ref[rows, :] = jnp.exp2(m_old - m_new)
            m_ref[rows, :] = m_new
            p_ref[rows, 0:ck] = jnp.exp2(s2_ref[slot, rows, :] - jnp.tile(m_new, (1, ck // LANE))).astype(BF16)
        acc_ref[...] = alpha_ref[...] * acc_ref[...] + _dot_nt(p_ref[:, 0:ck], vsa_ref[:, pl.ds(k0, ck)])

    scores(todo_ref[0], 0)
    ntrip = cnt + 1

    def visit_pair(i, carry):
        j = 2 * i
        scores(todo_ref[j + 1], 1)
        accumulate(todo_ref[j], 0)
        scores(todo_ref[j + 2], 0)
        accumulate(todo_ref[j + 1], 1)
        return carry

    lax.fori_loop(0, ntrip // 2, visit_pair, 0)

    @pl.when(ntrip % 2 == 1)
    def _():
        accumulate(todo_ref[ntrip - 1], 0)

    gate = jax.nn.sigmoid(gate_ref[...])
    hd = HEAD_DIM

    def normalised(x):
        return x[:, 0:hd] * (1.0 / jnp.maximum(x[:, hd:hd + 1], 1e-30))

    outs = []
    for g in range(g_n):
        c0 = 3 * (n * g_n + g)
        rows = slice(g * qb, (g + 1) * qb)
        outs.append(gate[:, c0:c0 + 1] * o_cmp[rows] + gate[:, c0 + 1:c0 + 2] * normalised(acc_ref[rows, :])
                    + gate[:, c0 + 2:c0 + 3] * normalised(win[rows]))
    o_ref[...] = jnp.concatenate(outs, axis=-1)


def _nsa_prompt(main, q16, kvtb, kwtp, kct, vc, n):
    t = main.shape[0]
    ncp = kct.shape[1]
    ns = t // SLC_BLOCK
    assert ns <= LANE and t % SLC_CHUNK == 0
    g_n, hd = NSA_GROUP, HEAD_DIM
    nrow = g_n * Q_BLOCK
    nw = WINDOW + Q_BLOCK
    ov = jnp.asarray(_overlap_matrix(ncp, LANE), BF16)
    sl = np.zeros((g_n, Q_BLOCK, hd), np.float32)
    parts = _np_split3(_alibi_slopes()[n * g_n:(n + 1) * g_n].astype(np.float64) * LOG2E)
    for j in range(N_SLOPE_COLS):
        sl[:, :, j] = parts[j % 3][:, None]
    sl = jnp.asarray(sl.reshape(nrow, hd), BF16)

    def with_pos_rows(kt, pos):
        rows = np.zeros((hd, pos.shape[0]), np.float32)
        rows[0:3] = (pos // SLC_BLOCK) * SLC_BLOCK
        rows[3:6] = pos % SLC_BLOCK
        return jnp.concatenate([kt, jnp.asarray(rows, BF16)], axis=0)

    ka = jnp.concatenate([with_pos_rows(kvtb[R_KSLC + n * hd:R_KSLC + (n + 1) * hd], np.arange(t)),
                          jnp.asarray(NEG * _expand_matrix(LANE, t), BF16)], axis=0)
    kca = with_pos_rows(kct, np.arange(ncp) * CMP_STRIDE + (CMP_BLOCK - 1))
    kwa = with_pos_rows(kwtp[n * hd:(n + 1) * hd], np.arange(kwtp.shape[1]))

    def with_ones_row(vt):
        rows = np.zeros((hd, vt.shape[1]), np.float32)
        rows[0] = 1.0
        return jnp.concatenate([vt, jnp.asarray(rows, BF16)], axis=0)

    vsa = with_ones_row(kvtb[R_VSLC + n * hd:R_VSLC + (n + 1) * hd])
    vwa = with_ones_row(kwtp[KVW + n * hd:KVW + (n + 1) * hd])
    wq = g_n * hd
    whole = lambda a: pl.BlockSpec(a.shape, lambda i: (0,) * a.ndim)
    return pl.pallas_call(
        functools.partial(_nsa_prompt_body, n=n, n_top=min(SLC_TOP, ns)),
        grid=(t // Q_BLOCK,),
        in_specs=[
            pl.BlockSpec((Q_BLOCK, wq), lambda i: (i, n)),
            pl.BlockSpec((Q_BLOCK, LANE), lambda i: (i, C_NGATE // LANE)),
            whole(sl), whole(kca), whole(vc), whole(ka), whole(vsa), whole(kwa), whole(vwa), whole(ov),
        ],
        out_specs=pl.BlockSpec((Q_BLOCK, wq), lambda i: (i, 0)),
        out_shape=jax.ShapeDtypeStruct((t, wq), F32),
        scratch_shapes=[
            pltpu.VMEM((nrow, ncp), F32),
            pltpu.VMEM((nrow, max(ncp, SLC_CHUNK)), BF16),
            pltpu.VMEM((Q_BLOCK, ncp), F32),
            pltpu.VMEM((Q_BLOCK, ncp), F32),
            pltpu.VMEM((nrow, nw), F32),
            pltpu.VMEM((nrow, nw), BF16),
            pltpu.VMEM((Q_BLOCK, nw), F32),
            pltpu.VMEM((2, nrow, SLC_CHUNK), F32),
            pltpu.VMEM((2, nrow, LANE), F32),
            pltpu.VMEM((2, Q_BLOCK, SLC_CHUNK), F32),
            pltpu.VMEM((nrow, LANE), F32),
            pltpu.VMEM((nrow, LANE), F32),
            pltpu.VMEM((nrow, 2 * hd), F32),
            pltpu.SMEM((t // SLC_CHUNK + 2,), jnp.int32),
        ],
        compiler_params=_cparams(("parallel",)),
        name=f"nsa_prompt_{n}",
    )(q16, main, sl, kca, vc, ka, vsa, kwa, vwa, ov)


N_KINDS = 4
NEW_LANES = 8
SAMPLE_PER_STEP = 2


def _nsa_sample_body(pt_ref, cache_ref, q_ref, new_ref, win_ref, gate_ref, slope_ref,
                     posk_ref, w1k_ref, w2k_ref, posv_ref, w1v_ref, w2v_ref, ov_ref, ex_ref, same_ref,
                     o_ref, wout_ref, buf_ref, rows_ref, newp_ref, sem, *, tn, n_top):
    b = pl.program_id(0)
    nsteps = pl.num_programs(0)
    nbs, npages = buf_ref.shape[1], buf_ref.shape[2]
    slot = b % 2

    def page_copy(step, sl, e, j):
        return pltpu.make_async_copy(cache_ref.at[pt_ref[step * nbs + e, j]], buf_ref.at[sl, e, j], sem.at[sl])

    def all_pages(step, sl, fn):
        for e in range(nbs):
            for j in range(npages):
                fn(page_copy(step, sl, e, j))

    @pl.when(b == 0)
    def _():
        all_pages(0, 0, lambda c: c.start())
        newp_ref[...] = jnp.zeros(newp_ref.shape, F32)

    @pl.when(b + 1 < nsteps)
    def _():
        all_pages(b + 1, 1 - slot, lambda c: c.start())

    all_pages(b, slot, lambda c: c.wait())
    for e in range(nbs):
        one = pl.ds(e, 1)
        _nsa_sample_one(buf_ref.at[slot, e], q_ref.at[one], new_ref.at[one], win_ref.at[one], gate_ref.at[one],
                        slope_ref, posk_ref, w1k_ref, w2k_ref, posv_ref, w1v_ref, w2v_ref, ov_ref, ex_ref,
                        same_ref, o_ref.at[one], wout_ref.at[one], rows_ref.at[e], newp_ref.at[e],
                        tn=tn, n_top=n_top)


def _nsa_sample_one(pg, q_ref, new_ref, win_ref, gate_ref, slope_ref,
                    posk_ref, w1k_ref, w2k_ref, posv_ref, w1v_ref, w2v_ref, ov_ref, ex_ref, same_ref,
                    o_ref, wout_ref, rows_ref, newp_ref, *, tn, n_top):
    npages, page = pg.shape[0], pg.shape[3]
    past_len = npages * page

    for a in range(4):
        newp_ref[a, :, 0:NEW_LANES] = new_ref[0, (2 + a) * KVW:(3 + a) * KVW, :]

    ncp = past_len // CMP_STRIDE
    for a in range(2):
        for j in range(npages):
            rows_ref[a, j * page:(j + 1) * page, :] = jnp.transpose(pg[j, a])
        rows_ref[a, past_len:past_len + LANE, :] = jnp.zeros((LANE, KVW), F32)

    def compress(a, pos_ref, w1_ref, w2_ref):
        acc = jnp.zeros((ncp, w1_ref.shape[2]), F32)
        for p in range(CMP_BLOCK):
            x = rows_ref[a, pl.ds(p, ncp, stride=CMP_STRIDE), :] + pos_ref[p]
            acc = acc + _dot(x.astype(BF16), w1_ref[p])
        return _dot(_gelu_tanh(acc).astype(BF16), w2_ref[...]).astype(BF16)

    kc = compress(0, posk_ref, w1k_ref, w2k_ref)
    vc = compress(1, posv_ref, w1v_ref, w2v_ref)

    rows = q_ref.shape[1]
    q = q_ref[0]
    slope = slope_ref[:, 0:1]
    r_i = lax.broadcasted_iota(jnp.int32, (rows, 1), 0)
    t_i = r_i & (tn - 1)
    qpos = past_len + t_i
    low = r_i < rows // 2

    def pick_head(o):
        return jnp.where(low, o[:, 0:HEAD_DIM], o[:, HEAD_DIM:KVW])

    cpos = CMP_STRIDE * lax.broadcasted_iota(jnp.int32, (rows, ncp), 1) + (CMP_BLOCK - 1)
    dist_c = (qpos - cpos).astype(F32)
    mask_c = dist_c >= 0.0
    s = jnp.where(mask_c, _dot_nt(q, kc) - slope * dist_c, NEG)
    p_c = _softmax2_rows(s, mask_c)
    o_cmp = pick_head(_dot(p_c.astype(BF16), vc))

    same = same_ref[...]
    ph, plo = _split2(p_c)
    psum = _dot(same, ph) + _dot(same, plo)
    ph, plo = _split2(psum)
    ov = ov_ref[...]
    score = _dot(ph, ov) + _dot(plo, ov)
    sel = _select_blocks(score, jnp.right_shift(qpos, SLC_SHIFT), n_top)

    u_i = lax.broadcasted_iota(jnp.int32, (rows, LANE), 1)
    new_ok = u_i <= t_i
    new_blk = past_len // SLC_BLOCK
    keep = _dot(sel.astype(BF16), ex_ref[...]) > 0.5
    spos = lax.broadcasted_iota(jnp.int32, (rows, past_len), 1)
    s_past = jnp.concatenate([_dot(q, pg[j, 2].astype(BF16)) for j in range(npages)], axis=1)
    s_past = jnp.where(keep, s_past - slope * (qpos - spos).astype(F32), NEG)
    keep_new = new_ok & (sel[:, new_blk:new_blk + 1] > 0.5)
    s_new = jnp.where(keep_new, _dot(q, newp_ref[0].astype(BF16)) - slope * (t_i - u_i).astype(F32), NEG)
    m = jnp.maximum(jnp.max(s_past, axis=-1, keepdims=True), jnp.max(s_new, axis=-1, keepdims=True))
    e_past = jnp.where(keep, jnp.exp2(s_past - m), 0.0)
    e_new = jnp.where(keep_new, jnp.exp2(s_new - m), 0.0)
    l = jnp.sum(e_past, axis=-1, keepdims=True) + jnp.sum(e_new, axis=-1, keepdims=True)
    e_past = e_past.astype(BF16)
    o_sel = _dot_nt(e_new.astype(BF16), newp_ref[1].astype(BF16))
    for j in range(npages):
        o_sel = o_sel + _dot_nt(e_past[:, j * page:(j + 1) * page], pg[j, 3].astype(BF16))
    o_sel = pick_head(o_sel * (1.0 / jnp.maximum(l, 1e-30)))

    wl = win_ref.shape[3]
    w_i = lax.broadcasted_iota(jnp.int32, (rows, wl), 1)
    dist_wi = (wl + t_i) - w_i
    mask_w = (dist_wi < WINDOW) & (w_i >= wl - past_len)
    s_win = jnp.where(mask_w, _dot(q, win_ref[0, 0].astype(BF16)) - slope * dist_wi.astype(F32), NEG)
    s_new = jnp.where(new_ok, _dot(q, newp_ref[2].astype(BF16)) - slope * (t_i - u_i).astype(F32), NEG)
    m = jnp.maximum(jnp.max(s_win, axis=-1, keepdims=True), jnp.max(s_new, axis=-1, keepdims=True))
    e_win = jnp.where(mask_w, jnp.exp2(s_win - m), 0.0)
    e_new = jnp.where(new_ok, jnp.exp2(s_new - m), 0.0)
    l = jnp.sum(e_win, axis=-1, keepdims=True) + jnp.sum(e_new, axis=-1, keepdims=True)
    o_win = (_dot_nt(e_win.astype(BF16), win_ref[0, 1].astype(BF16))
             + _dot_nt(e_new.astype(BF16), newp_ref[3].astype(BF16)))
    o_win = pick_head(o_win * (1.0 / jnp.maximum(l, 1e-30)))

    gate = jax.nn.sigmoid(gate_ref[0])
    o_ref[0] = gate[:, 0:1] * o_cmp + gate[:, 1:2] * o_sel + gate[:, 2:3] * o_win

    lane_w = lax.broadcasted_iota(jnp.int32, (KVW, LANE), 1)
    for a in range(2):
        shifted = pltpu.roll(win_ref[0, a], wl - tn, 1)
        tail = pltpu.roll(newp_ref[2 + a], LANE - tn, 1)
        wout_ref[0, a, :, 0:wl - LANE] = shifted[:, 0:wl - LANE]
        wout_ref[0, a, :, wl - LANE:wl] = jnp.where(lane_w >= LANE - tn, tail, shifted[:, wl - LANE:wl])


def _nsa_sample(page_table, cache, qrows, newt, win, gates, cmpk, cmpv, *, tn):
    nb, npages = page_table.shape
    page = cache.shape[3]
    past_len = npages * page
    wl = win.shape[3]
    rows = qrows.shape[1]
    assert tn == 4 and page == LANE and wl % LANE == 0 and wl > LANE and past_len >= wl
    ncp = past_len // CMP_STRIDE
    ns = -(-(past_len + tn) // SLC_BLOCK)
    assert ncp <= LANE and ns <= LANE
    r = np.arange(rows)
    h = r // tn
    slope = jnp.asarray(np.broadcast_to((_alibi_slopes()[h].astype(np.float64) * LOG2E)[:, None], (rows, LANE)), F32)
    nt = h // NSA_GROUP * tn + r % tn
    same = jnp.asarray((nt[:, None] == nt[None, :]).astype(np.float32), BF16)
    ov = jnp.asarray(_overlap_matrix(ncp, LANE), BF16)
    ex = jnp.asarray(_expand_matrix(LANE, past_len), BF16)
    posk, w1k, w2k = cmpk
    posv, w1v, w2v = cmpv
    nbs = SAMPLE_PER_STEP
    assert nb % nbs == 0
    const = lambda a: pl.BlockSpec(a.shape, lambda b, pt: (0,) * a.ndim)
    per_b = lambda a: pl.BlockSpec((nbs,) + a.shape[1:], lambda b, pt: (b,) + (0,) * (a.ndim - 1))
    grid_spec = pltpu.PrefetchScalarGridSpec(
        num_scalar_prefetch=1,
        grid=(nb // nbs,),
        in_specs=[
            pl.BlockSpec(memory_space=pl.ANY),
            per_b(qrows), per_b(newt), per_b(win), per_b(gates), const(slope),
            const(posk), const(w1k), const(w2k), const(posv), const(w1v), const(w2v),
            const(ov), const(ex), const(same),
        ],
        out_specs=[pl.BlockSpec((nbs, rows, HEAD_DIM), lambda b, pt: (b, 0, 0)), per_b(win)],
        scratch_shapes=[
            pltpu.VMEM((2, nbs, npages, N_KINDS, KVW, page), F32),
            pltpu.VMEM((nbs, 2, past_len + LANE, KVW), F32),
            pltpu.VMEM((nbs, 4, KVW, LANE), F32),
            pltpu.SemaphoreType.DMA((2,)),
        ],
    )
    return pl.pallas_call(
        functools.partial(_nsa_sample_body, tn=tn, n_top=min(SLC_TOP, ns)),
        grid_spec=grid_spec,
        out_shape=[jax.ShapeDtypeStruct((nb, rows, HEAD_DIM), F32), jax.ShapeDtypeStruct(win.shape, F32)],
        compiler_params=_cparams(("arbitrary",)),
        name="nsa_sample",
    )(page_table, cache, qrows, newt, win, gates, slope, posk, w1k, w2k, posv, w1v, w2v, ov, ex, same)


def _prep_ffn(wg, wu, wd, tf=512):
    dff = wg.shape[1]
    pad = -dff % tf
    return (jnp.pad(wg, ((0, 0), (0, pad))).astype(BF16), jnp.pad(wu, ((0, 0), (0, pad))).astype(BF16),
            jnp.pad(wd, ((0, pad), (0, 0))).astype(BF16))


def _prep_w_in(w_in):
    d = w_in.shape[0]
    cols = [w_in[:, :W_NSAQ], w_in[:, W_GA:], w_in[:, W_KV:W_KV + 2 * KVW], w_in[:, W_NGATE:W_GA]]
    used = sum(c.shape[1] for c in cols)
    w_main = jnp.concatenate(cols + [jnp.zeros((d, D_MAIN - used), w_in.dtype)], axis=1).astype(BF16)
    w_q = w_in[:, W_NSAQ:W_KV].astype(BF16)
    w_kvt = w_in[:, W_KV:W_NGATE].T.astype(BF16)
    return w_main, w_q, w_kvt


def _prep_cmp_prompt(pos_k, w1_k, w2_k, pos_v, w1_v, w2_v):
    half = CMP_STRIDE * HEAD_DIM
    pos = jnp.stack([pos_k.reshape(2, 1, half), pos_v.reshape(2, 1, half)])
    w1 = jnp.stack([w1_k.reshape(2, half, -1), w1_v.reshape(2, half, -1)]).astype(BF16)
    w2 = jnp.stack([w2_k, w2_v]).astype(BF16)
    return pos, w1, w2


def _prep_cmp_sample(pos, w1, w2):
    hid = w1.shape[1]
    w1p = w1.reshape(CMP_BLOCK, HEAD_DIM, hid)
    z1 = jnp.zeros_like(w1p)
    w1b = jnp.concatenate([jnp.concatenate([w1p, z1], axis=2), jnp.concatenate([z1, w1p], axis=2)], axis=1)
    z2 = jnp.zeros_like(w2)
    w2b = jnp.concatenate([jnp.concatenate([w2, z2], axis=1), jnp.concatenate([z2, w2], axis=1)], axis=0)
    posb = jnp.concatenate([pos, pos], axis=1)[:, None, :]
    return posb, w1b.astype(BF16), w2b.astype(BF16)


def _project(x, g, w_main, w_q, w_kvt):
    main = _inproj(x, g, w_main, tm=1024)
    q16 = _inproj(x, g, w_q, out_dtype=BF16, scale=HEAD_DIM ** -0.5 * LOG2E)
    kvt, kvtb = _inproj_t(x, g, w_kvt)
    return main, q16, kvt, kvtb


def _mix_prompt(main, q16, kvtb, lb, gnorm, cmp_w):
    t = main.shape[0]
    o_hg, s_fin = _hgrn_prompt(main, lb, gnorm)
    nkv, hd = NSA_KV_HEADS, HEAD_DIM
    groups = main[:, C_CMP:C_CMP + 2 * KVW].reshape(t // CMP_STRIDE, CMP_STRIDE, 2, nkv, hd)
    groups = groups.transpose(2, 3, 0, 1, 4).reshape(2 * nkv, t // CMP_STRIDE, CMP_STRIDE * hd)
    cmp = _compress_prompt(groups, *cmp_w)
    kct = cmp[:nkv].transpose(0, 2, 1)
    vc = cmp[nkv:]
    kwtp = jnp.pad(kvtb[R_KWIN:N_KV_ROWS], ((0, 0), (WINDOW, 0)))
    o_nsa = [_nsa_prompt(main, q16, kvtb, kwtp, kct[n], vc[n], n) for n in range(nkv)]
    return o_hg, o_nsa, s_fin


def _mix_sample(main, q16, kvt, lb, gnorm, state, cache, win, page_table, cmp_w, *, tn):
    nb = state.shape[0]
    o_hg, s_new = _hgrn_sample(main, lb, gnorm, state, tn=tn)
    g_n, nkv, hd = NSA_GROUP, NSA_KV_HEADS, HEAD_DIM
    q = q16.reshape(nb, tn, nkv, g_n, hd).transpose(0, 2, 3, 1, 4)
    eye = jnp.eye(nkv, dtype=BF16)
    qrows = (q[:, :, :, :, None, :] * eye[None, :, None, None, :, None]).reshape(nb, nkv * g_n * tn, nkv * hd)
    gates = main[:, C_NGATE:C_NGATE + N_NGATE].reshape(nb, tn, NSA_HEADS, 3).transpose(0, 2, 1, 3)
    gates = jnp.pad(gates.reshape(nb, NSA_HEADS * tn, 3), ((0, 0), (0, 0), (0, LANE - 3)))
    newt = jnp.pad(kvt.reshape(N_KV_ROWS, nb, tn).transpose(1, 0, 2), ((0, 0), (0, 0), (0, NEW_LANES - tn)))
    o, wout = _nsa_sample(page_table, cache, qrows, newt, win, gates, *cmp_w, tn=tn)
    o = o.reshape(nb, nkv, g_n, tn, hd).transpose(0, 3, 1, 2, 4).reshape(nb * tn, nkv, g_n * hd)
    return o_hg, [o[:, n] for n in range(nkv)], s_new, wout


def kernel(x_prompt, x_sample, cache_kv, cache_win, state_hgrn, page_table, norm_pre1, norm_post1, ff1_gate, ff1_up, ff1_down, norm_pre2, norm_post2, w_in, hg_lb, hg_gnorm, cmp_pos_k, cmp_w1_k, cmp_w2_k, cmp_pos_v, cmp_w1_v, cmp_w2_v, w_proj_hg, w_proj_nsa, w_out, norm_pre3, norm_post3, ff2_gate, ff2_up, ff2_down):
    depth = norm_pre1.shape[0]
    bp, tp, d = x_prompt.shape
    nb, tn, _ = x_sample.shape
    assert bp == 1
    nkv, hd = NSA_KV_HEADS, HEAD_DIM
    lb_all = jnp.cumsum(jax.nn.softmax(hg_lb.astype(F32), axis=0), axis=0)[:depth]
    xp = x_prompt.reshape(tp, d)
    xs = x_sample.reshape(nb * tn, d)
    kv_p, kv_s, win_p, win_s, st_p, st_s = [], [], [], [], [], []
    row = lambda a: a.reshape(1, -1)
    for l in range(depth):
        ff1 = _prep_ffn(ff1_gate[l], ff1_up[l], ff1_down[l])
        ff2 = _prep_ffn(ff2_gate[l], ff2_up[l], ff2_down[l])
        w_in_l = _prep_w_in(w_in[l])
        wh, wn, wo = w_proj_hg[l].astype(BF16), w_proj_nsa[l].astype(BF16), w_out[l].astype(BF16)
        cmp_p = _prep_cmp_prompt(cmp_pos_k[l], cmp_w1_k[l], cmp_w2_k[l], cmp_pos_v[l], cmp_w1_v[l], cmp_w2_v[l])
        cmp_s = (_prep_cmp_sample(cmp_pos_k[l], cmp_w1_k[l], cmp_w2_k[l]),
                 _prep_cmp_sample(cmp_pos_v[l], cmp_w1_v[l], cmp_w2_v[l]))
        lb, gnorm = row(lb_all[l]), row(hg_gnorm[l])
        n_pool, page = cache_kv.shape[1], cache_kv.shape[2]
        cache = cache_kv[l].transpose(0, 2, 3, 4, 1).reshape(n_pool, N_KINDS, KVW, page)
        wl = cache_win.shape[2]
        win = cache_win[l].transpose(0, 2, 3, 4, 1).reshape(nb, 2, KVW, wl)

        xp = _ffn(xp, row(norm_pre1[l]), row(norm_post1[l]), *ff1)
        xs = _ffn(xs, row(norm_pre1[l]), row(norm_post1[l]), *ff1)

        main_p, q_p, kvt_p, kvtb_p = _project(xp, row(norm_pre2[l]), *w_in_l)
        main_s, q_s, kvt_s, _ = _project(xs, row(norm_pre2[l]), *w_in_l)
        ohg_p, onsa_p, s_p = _mix_prompt(main_p, q_p, kvtb_p, lb, gnorm, cmp_p)
        ohg_s, onsa_s, s_s, wout = _mix_sample(main_s, q_s, kvt_s, lb, gnorm, state_hgrn[l], cache, win,
                                               page_table, cmp_s, tn=tn)
        xp = _outproj(_merge(ohg_p, *onsa_p, main_p, wh, wn), xp, wo, row(norm_post2[l]))
        xs = _outproj(_merge(ohg_s, *onsa_s, main_s, wh, wn), xs, wo, row(norm_post2[l]))

        kv_p.append(kvt_p[:N_KINDS * KVW].reshape(N_KINDS, nkv, hd, bp, tp).transpose(3, 4, 0, 1, 2))
        kv_s.append(kvt_s[:N_KINDS * KVW].reshape(N_KINDS, nkv, hd, nb, tn).transpose(3, 4, 0, 1, 2))
        wp = min(WINDOW, tp)
        win_p.append(kvt_p[R_KWIN:, tp - wp:].reshape(2, nkv, hd, bp, wp).transpose(3, 4, 0, 1, 2))
        win_s.append(wout.reshape(nb, 2, nkv, hd, wl).transpose(0, 4, 1, 2, 3))
        st_p.append(s_p[None])
        st_s.append(s_s)

        xp = _ffn(xp, row(norm_pre3[l]), row(norm_post3[l]), *ff2)
        xs = _ffn(xs, row(norm_pre3[l]), row(norm_post3[l]), *ff2)
    return (xp.reshape(bp, tp, d), xs.reshape(nb, tn, d), jnp.stack(kv_p), jnp.stack(kv_s),
            jnp.stack(win_p), jnp.stack(win_s), jnp.stack(st_p), jnp.stack(st_s))
```

```python
import functools

import numpy as np
import jax
import jax.numpy as jnp
from jax import lax
from jax.experimental import pallas as pl
from jax.experimental.pallas import tpu as pltpu

F32 = jnp.float32
BF16 = jnp.bfloat16

EPS = 1e-6
HG_HEADS = 8
HG_D = 128
NSA_HEADS = 16
NSA_KV_HEADS = 2
NSA_GROUP = NSA_HEADS // NSA_KV_HEADS
HEAD_DIM = 64
CMP_BLOCK = 32
CMP_STRIDE = 16
SLC_BLOCK = 64
SLC_SHIFT = 6
SLC_TOP = 16
WINDOW = 512
Q_BLOCK = 128
FORCE_SCORE = 1.0e4
NEG = -1.0e30
NEG_PICKED = -3.0e38
M_INIT = -1.0e20
LOG2E = float(np.log2(np.e))

LANE = 128
VMEM_LIMIT = 56 * 1024 * 1024

W_NSAQ, W_KV, W_NGATE, W_GA = 4096, 5120, 5888, 5936
N_NGATE = 3 * NSA_HEADS
KVW = NSA_KV_HEADS * HEAD_DIM
C_HGQ, C_HGF, C_HGI, C_HGG = 0, 1024, 2048, 3072
C_GA, C_GB = 4096, 6144
C_CMP = 8192
C_NGATE = 8448
D_MAIN = 8704
R_KSLC, R_VSLC, R_KWIN, R_VWIN = 2 * KVW, 3 * KVW, 4 * KVW, 5 * KVW
N_KV_ROWS = 6 * KVW


def _cparams(sem):
    return pltpu.CompilerParams(dimension_semantics=sem, vmem_limit_bytes=VMEM_LIMIT)


def _rms(x, g):
    return x * lax.rsqrt(jnp.mean(x * x, axis=-1, keepdims=True) + EPS) * g


def _silu(x):
    return x * jax.nn.sigmoid(x)


def _dot(a, b):
    return jnp.dot(a, b, preferred_element_type=F32)


def _dot_nt(a, b):
    return lax.dot_general(a, b, (((1,), (1,)), ((), ())), preferred_element_type=F32)


def _dot_tn(a, b):
    return lax.dot_general(a, b, (((0,), (0,)), ((), ())), preferred_element_type=F32)


def _split3(x):
    hi = x.astype(BF16)
    r = x - hi.astype(F32)
    mid = r.astype(BF16)
    lo = (r - mid.astype(F32)).astype(BF16)
    return hi, mid, lo


def _split2(x):
    hi = x.astype(BF16)
    return hi, (x - hi.astype(F32)).astype(BF16)


def _ffn_body(x_ref, npre_ref, npost_ref, wg_ref, wu_ref, wd_ref, wgr_ref, wur_ref, wdr_ref, o_ref, h_ref, acc_ref):
    f = pl.program_id(1)

    @pl.when(f == 0)
    def _():
        h_ref[...] = _rms(x_ref[...], npre_ref[...]).astype(BF16)
        acc_ref[...] = jnp.zeros_like(acc_ref)

    def partial_down(wg, wu, wd):
        h = h_ref[...]
        a = _silu(_dot(h, wg[...])) * _dot(h, wu[...])
        return _dot(a.astype(BF16), wd[...])

    acc_ref[...] += partial_down(wg_ref, wu_ref, wd_ref)

    @pl.when(f == pl.num_programs(1) - 1)
    def _():
        y = acc_ref[...] + partial_down(wgr_ref, wur_ref, wdr_ref)
        o_ref[...] = x_ref[...] + 0.5 * _rms(y, npost_ref[...])


def _ffn(x, npre, npost, wg, wu, wd, wgr, wur, wdr, *, tm=512, tf=512):
    n, d = x.shape
    dff = wg.shape[1]
    rem = wgr.shape[1]
    assert rem > 0 and (dff - rem) % tf == 0 and rem % LANE == 0
    tm = min(tm, n)
    return pl.pallas_call(
        _ffn_body,
        grid=(n // tm, (dff - rem) // tf),
        in_specs=[
            pl.BlockSpec((tm, d), lambda i, f: (i, 0)),
            pl.BlockSpec((1, d), lambda i, f: (0, 0)),
            pl.BlockSpec((1, d), lambda i, f: (0, 0)),
            pl.BlockSpec((d, tf), lambda i, f: (0, f)),
            pl.BlockSpec((d, tf), lambda i, f: (0, f)),
            pl.BlockSpec((tf, d), lambda i, f: (f, 0)),
            pl.BlockSpec((d, rem), lambda i, f: (0, 0)),
            pl.BlockSpec((d, rem), lambda i, f: (0, 0)),
            pl.BlockSpec((rem, d), lambda i, f: (0, 0)),
        ],
        out_specs=pl.BlockSpec((tm, d), lambda i, f: (i, 0)),
        out_shape=jax.ShapeDtypeStruct((n, d), F32),
        scratch_shapes=[pltpu.VMEM((tm, d), BF16), pltpu.VMEM((tm, d), F32)],
        compiler_params=_cparams(("parallel", "arbitrary")),
        name="ffn",
    )(x, npre, npost, wg, wu, wd, wgr, wur, wdr)


def _inproj_body(x_ref, g_ref, wt_ref, o_ref, h_ref, *, scale):
    @pl.when(pl.program_id(1) == 0)
    def _():
        h_ref[...] = _rms(x_ref[...], g_ref[...]).astype(BF16)

    o_ref[...] = (_dot_nt(h_ref[...], wt_ref[...]) * scale).astype(o_ref.dtype)


def _inproj(x, g, wt, *, out_dtype=F32, scale=1.0, tm=512, tn=512):
    n, d = x.shape
    dout = wt.shape[0]
    tm = min(tm, n)
    return pl.pallas_call(
        functools.partial(_inproj_body, scale=scale),
        grid=(n // tm, dout // tn),
        in_specs=[
            pl.BlockSpec((tm, d), lambda i, j: (i, 0)),
            pl.BlockSpec((1, d), lambda i, j: (0, 0)),
            pl.BlockSpec((tn, d), lambda i, j: (j, 0)),
        ],
        out_specs=pl.BlockSpec((tm, tn), lambda i, j: (i, j)),
        out_shape=jax.ShapeDtypeStruct((n, dout), out_dtype),
        scratch_shapes=[pltpu.VMEM((tm, d), BF16)],
        compiler_params=_cparams(("parallel", "arbitrary")),
        name="inproj",
    )(x, g, wt)


def _inproj_t_body(x_ref, g_ref, wt_ref, o_ref, ob_ref):
    y = _dot_nt(wt_ref[...], _rms(x_ref[...], g_ref[...]).astype(BF16))
    o_ref[...] = y
    ob_ref[...] = y.astype(BF16)


def _inproj_t(x, g, wt, *, tm=512):
    n, d = x.shape
    dout = wt.shape[0]
    tm = min(tm, n)
    return pl.pallas_call(
        _inproj_t_body,
        grid=(n // tm,),
        in_specs=[
            pl.BlockSpec((tm, d), lambda i: (i, 0)),
            pl.BlockSpec((1, d), lambda i: (0, 0)),
            pl.BlockSpec((dout, d), lambda i: (0, 0)),
        ],
        out_specs=[pl.BlockSpec((dout, tm), lambda i: (0, i)), pl.BlockSpec((dout, tm), lambda i: (0, i))],
        out_shape=[jax.ShapeDtypeStruct((dout, n), F32), jax.ShapeDtypeStruct((dout, n), BF16)],
        compiler_params=_cparams(("parallel",)),
        name="inproj_t",
    )(x, g, wt)


def _merge_body(ohg_ref, on0_ref, on1_ref, ga_ref, gb_ref, wh_ref, wn_ref, y_ref):
    onsa = jnp.concatenate([on0_ref[...], on1_ref[...]], axis=-1).astype(BF16)
    yh = _dot(ohg_ref[...].astype(BF16), wh_ref[...])
    yn = _dot(onsa, wn_ref[...])
    y = jax.nn.sigmoid(ga_ref[...]) * yh + jax.nn.sigmoid(gb_ref[...]) * yn
    y_ref[...] = y.astype(BF16)


def _merge(ohg, on0, on1, main, wh, wn, *, tm=512):
    n = ohg.shape[0]
    d = wh.shape[1]
    tm = min(tm, n)
    return pl.pallas_call(
        _merge_body,
        grid=(n // tm,),
        in_specs=[
            pl.BlockSpec((tm, ohg.shape[1]), lambda i: (i, 0)),
            pl.BlockSpec((tm, on0.shape[1]), lambda i: (i, 0)),
            pl.BlockSpec((tm, on1.shape[1]), lambda i: (i, 0)),
            pl.BlockSpec((tm, d), lambda i: (i, C_GA // d)),
            pl.BlockSpec((tm, d), lambda i: (i, C_GB // d)),
            pl.BlockSpec(wh.shape, lambda i: (0, 0)),
            pl.BlockSpec(wn.shape, lambda i: (0, 0)),
        ],
        out_specs=pl.BlockSpec((tm, d), lambda i: (i, 0)),
        out_shape=jax.ShapeDtypeStruct((n, d), BF16),
        compiler_params=_cparams(("parallel",)),
        name="merge",
    )(ohg, on0, on1, main, main, wh, wn)


def _outproj_body(y_ref, x_ref, w_ref, g_ref, o_ref):
    o_ref[...] = x_ref[...] + _rms(_dot(y_ref[...], w_ref[...]), g_ref[...])


def _outproj(y, x, w, g, *, tm=512):
    n, d = x.shape
    tm = min(tm, n)
    return pl.pallas_call(
        _outproj_body,
        grid=(n // tm,),
        in_specs=[
            pl.BlockSpec((tm, d), lambda i: (i, 0)),
            pl.BlockSpec((tm, d), lambda i: (i, 0)),
            pl.BlockSpec(w.shape, lambda i: (0, 0)),
            pl.BlockSpec((1, d), lambda i: (0, 0)),
        ],
        out_specs=pl.BlockSpec((tm, d), lambda i: (i, 0)),
        out_shape=jax.ShapeDtypeStruct((n, d), F32),
        compiler_params=_cparams(("parallel",)),
        name="outproj",
    )(y, x, w, g)


def _hgrn_range_matrix(c):
    t = np.arange(c)[:, None]
    u = np.arange(c)[None, :]
    blocks = [u <= t, u > t]
    m = c // 2
    while m >= 1:
        p = (t // (2 * m)) * 2 * m + m - 1
        upper = (t // m) % 2 == 1
        blocks.append(np.where(upper, (u > p) & (u <= t), (u > t) & (u <= p)))
        m //= 2
    return np.concatenate(blocks, axis=0).astype(np.float32)


def _hgrn_gates(q_raw, f_raw, lb):
    f = lb + (1.0 - lb) * jax.nn.sigmoid(f_raw)
    return _silu(q_raw), jnp.log(f), 1.0 - f


def _hgrn_finish(o, g_raw, gnorm):
    return _rms(o, gnorm) * _silu(g_raw)


def _hgrn_prompt_body(q_ref, f_ref, i_ref, g_ref, lb_ref, gn_ref, rm_ref, o_ref, sfin_ref, s_ref, *, c, hp):
    @pl.when(pl.program_id(1) == 0)
    def _():
        s_ref[...] = jnp.zeros(s_ref.shape, F32)

    cols = [slice(j * HG_D, (j + 1) * HG_D) for j in range(hp)]
    res = [_hgrn_chunk(q_ref[:, sl], f_ref[:, sl], i_ref[:, sl], lb_ref[:, sl], rm_ref[...], s_ref[j], c)
           for j, sl in enumerate(cols)]
    outs = [_hgrn_finish(o, g_ref[:, sl], gn_ref[...]) for (o, _), sl in zip(res, cols)]
    for j, sl in enumerate(cols):
        s_ref[j] = res[j][1]
        sfin_ref[j] = res[j][1]
        o_ref[:, sl] = outs[j]


def _hgrn_chunk(q_raw, f_raw, v, lb, rm, s, c):
    q, logf, k = _hgrn_gates(q_raw, f_raw, lb)
    vb = v.astype(BF16)
    hi, mid, lo = _split3(logf)
    r3 = _dot(rm, jnp.concatenate([hi, mid, lo], axis=1))
    e = r3[:, 0:HG_D] + r3[:, HG_D:2 * HG_D] + r3[:, 2 * HG_D:3 * HG_D]
    b = e[0:c]
    o = _dot((q * jnp.exp(b)).astype(BF16), s.astype(BF16))
    row = lax.broadcasted_iota(jnp.int32, (c, c), 0)
    col = lax.broadcasted_iota(jnp.int32, (c, c), 1)
    x = row ^ col
    a = jnp.where(x == 0, _dot_nt(q.astype(BF16), k.astype(BF16)), 0.0)
    rowk = lax.broadcasted_iota(jnp.int32, (c, HG_D), 0)
    m = c // 2
    lvl = 0
    while m >= 1:
        w = jnp.exp(e[(2 + lvl) * c:(3 + lvl) * c])
        upper = (rowk & m) != 0
        ql = jnp.where(upper, q * w, 0.0).astype(BF16)
        kl = jnp.where(upper, 0.0, k * w).astype(BF16)
        a = a + jnp.where(x < 2 * m, _dot_nt(ql, kl), 0.0)
        m //= 2
        lvl += 1
    o = o + _dot(a.astype(BF16), vb)
    kd = (k * jnp.exp(e[c:2 * c])).astype(BF16)
    ecol = jnp.transpose(jnp.broadcast_to(jnp.exp(b[c - 1:c, :]), (HG_D, HG_D)))
    return o, ecol * s + _dot_tn(kd, vb)


def _hgrn_prompt(main, lb, gnorm, *, c=128, hp=8):
    t = main.shape[0]
    assert c == HG_D and HG_HEADS % hp == 0
    rm = jnp.asarray(_hgrn_range_matrix(c), BF16)
    w = hp * HG_D
    col = lambda base: (lambda h, ci: (ci, base // w + h))
    return pl.pallas_call(
        functools.partial(_hgrn_prompt_body, c=c, hp=hp),
        grid=(HG_HEADS // hp, t // c),
        in_specs=[
            pl.BlockSpec((c, w), col(C_HGQ)),
            pl.BlockSpec((c, w), col(C_HGF)),
            pl.BlockSpec((c, w), col(C_HGI)),
            pl.BlockSpec((c, w), col(C_HGG)),
            pl.BlockSpec((1, w), lambda h, ci: (0, h)),
            pl.BlockSpec((1, HG_D), lambda h, ci: (0, 0)),
            pl.BlockSpec(rm.shape, lambda h, ci: (0, 0)),
        ],
        out_specs=[
            pl.BlockSpec((c, w), lambda h, ci: (ci, h)),
            pl.BlockSpec((hp, HG_D, HG_D), lambda h, ci: (h, 0, 0)),
        ],
        out_shape=[
            jax.ShapeDtypeStruct((t, HG_HEADS * HG_D), F32),
            jax.ShapeDtypeStruct((HG_HEADS, HG_D, HG_D), F32),
        ],
        scratch_shapes=[pltpu.VMEM((hp, HG_D, HG_D), F32)],
        compiler_params=_cparams(("parallel", "arbitrary")),
        name="hgrn_prompt",
    )(main, main, main, main, lb, gnorm, rm)


def _pad16(x):
    return jnp.concatenate([x, jnp.zeros_like(x)], axis=0)


def _hgrn_sample_body(q_ref, f_ref, i_ref, g_ref, lb_ref, gn_ref, s_ref, o_ref, so_ref, *, tn):
    rows = 2 * tn
    row = lax.broadcasted_iota(jnp.int32, (rows, HG_D), 0)
    pos = row & (tn - 1)
    first = row < tn
    for h in range(HG_HEADS):
        sl = slice(h * HG_D, (h + 1) * HG_D)
        q, logf, k = _hgrn_gates(q_ref[:, sl], f_ref[:, sl], lb_ref[:, sl])
        v = i_ref[:, sl]
        b = logf
        d = 1
        while d < tn:
            b = b + jnp.where(pos >= d, pltpu.roll(b, d, 0), 0.0)
            d *= 2
        oi = jnp.sum(q * k, axis=-1, keepdims=True) * v
        for d in range(1, tn):
            ok = pos >= d
            w = jnp.exp(jnp.where(ok, b - pltpu.roll(b, d, 0), 0.0))
            a = jnp.sum(jnp.where(ok, q * pltpu.roll(k, d, 0) * w, 0.0), axis=-1, keepdims=True)
            oi = oi + a * pltpu.roll(v, d, 0)
        qe = _pad16(q * jnp.exp(b)).astype(BF16)
        blast = jnp.where(first, b[tn - 1:tn, :], b[rows - 1:rows, :])
        kd = k * jnp.exp(blast - b)
        v16 = _pad16(v).astype(BF16)
        o_inter = []
        for bi in range(2):
            s = s_ref[bi, h]
            o_inter.append(_dot(qe, s.astype(BF16))[0:rows])
            mine = first if bi == 0 else jnp.logical_not(first)
            kd16 = _pad16(jnp.where(mine, kd, 0.0)).astype(BF16)
            e_row = jnp.exp(b[(bi + 1) * tn - 1:(bi + 1) * tn, :])
            ecol = jnp.transpose(jnp.broadcast_to(e_row, (HG_D, HG_D)))
            so_ref[bi, h] = ecol * s + _dot_tn(kd16, v16)
        o = jnp.where(first, o_inter[0], o_inter[1]) + oi
        o_ref[:, sl] = _hgrn_finish(o, g_ref[:, sl], gn_ref[...])


def _hgrn_sample(main, lb, gnorm, state, *, tn):
    n = main.shape[0]
    nb = state.shape[0]
    assert n == nb * tn and tn == 4 and nb % 2 == 0
    w = HG_HEADS * HG_D
    rows = 2 * tn
    col = lambda base: (lambda i: (i, base // w))
    return pl.pallas_call(
        functools.partial(_hgrn_sample_body, tn=tn),
        grid=(nb // 2,),
        in_specs=[
            pl.BlockSpec((rows, w), col(C_HGQ)),
            pl.BlockSpec((rows, w), col(C_HGF)),
            pl.BlockSpec((rows, w), col(C_HGI)),
            pl.BlockSpec((rows, w), col(C_HGG)),
            pl.BlockSpec((1, w), lambda i: (0, 0)),
            pl.BlockSpec((1, HG_D), lambda i: (0, 0)),
            pl.BlockSpec((2, HG_HEADS, HG_D, HG_D), lambda i: (i, 0, 0, 0)),
        ],
        out_specs=[
            pl.BlockSpec((rows, w), lambda i: (i, 0)),
            pl.BlockSpec((2, HG_HEADS, HG_D, HG_D), lambda i: (i, 0, 0, 0)),
        ],
        out_shape=[
            jax.ShapeDtypeStruct((n, w), F32),
            jax.ShapeDtypeStruct(state.shape, F32),
        ],
        compiler_params=_cparams(("parallel",)),
        name="hgrn_sample",
    )(main, main, main, main, lb, gnorm, state)


def _alibi_slopes():
    return np.power(2.0, -8.0 * np.arange(1, NSA_HEADS + 1) / NSA_HEADS).astype(np.float32)


def _gelu_tanh(x):
    return 0.5 * x * (1.0 + jnp.tanh(np.float32(np.sqrt(2.0 / np.pi)) * (x + 0.044715 * (x * x * x))))


def _overlap_matrix(ncp, nlanes):
    cs = np.arange(ncp)[:, None] * CMP_STRIDE
    ss = np.arange(nlanes)[None, :] * SLC_BLOCK
    return ((cs <= ss + SLC_BLOCK - 1) & (cs + CMP_BLOCK - 1 >= ss)).astype(np.float32)


def _expand_matrix(nlanes, nkeys):
    return (np.arange(nkeys)[None, :] // SLC_BLOCK == np.arange(nlanes)[:, None]).astype(np.float32)


def _np_split3(x):
    parts = []
    r = np.asarray(x, np.float64)
    for _ in range(3):
        p = r.astype(np.float32).astype(BF16).astype(np.float64)
        parts.append(p.astype(np.float32))
        r = r - p
    return parts


def _softmax2_rows(s, mask):
    m = jnp.max(s, axis=-1, keepdims=True)
    e = jnp.where(mask, jnp.exp2(s - m), 0.0)
    l = jnp.sum(e, axis=-1, keepdims=True)
    return e * (1.0 / jnp.maximum(l, 1e-30))


def _select_blocks(score, qblk, n_top):
    blk = lax.broadcasted_iota(jnp.int32, score.shape, 1)
    forced = (blk == 0) | (blk == qblk) | (blk == qblk - 1)
    valid = blk <= qblk
    work = jnp.where(valid, jnp.where(forced, FORCE_SCORE, score), NEG)
    nrow, nblk = score.shape
    if nrow < nblk:
        work = jnp.concatenate([work, jnp.full((nblk - nrow, nblk), NEG, F32)], axis=0)
    wt = jnp.transpose(work)
    blk_t = lax.broadcasted_iota(jnp.int32, wt.shape, 0)
    sel_t = jnp.zeros(wt.shape, F32)
    for _ in range(n_top):
        m = jnp.max(wt, axis=0, keepdims=True)
        first = jnp.min(jnp.where(wt == m, blk_t, nblk), axis=0, keepdims=True)
        pick = blk_t == first
        sel_t = jnp.where(pick, 1.0, sel_t)
        wt = jnp.where(pick, NEG_PICKED, wt)
    return jnp.where(valid, jnp.transpose(sel_t)[0:nrow], 0.0)


def _compress_body(g_ref, pos_ref, w1_ref, w2_ref, o_ref):
    g = g_ref[0]
    ylo = _dot((g + pos_ref[0, 0]).astype(BF16), w1_ref[0, 0])
    yhi = _dot((g + pos_ref[0, 1]).astype(BF16), w1_ref[0, 1])
    pre = ylo + pltpu.roll(yhi, g.shape[0] - 1, 0)
    o_ref[0] = _dot(_gelu_tanh(pre).astype(BF16), w2_ref[0]).astype(BF16)


def _compress_prompt(groups, pos, w1, w2):
    na, ng, gw = groups.shape
    hid = w1.shape[-1]
    kv = lambda a: a // NSA_KV_HEADS
    return pl.pallas_call(
        _compress_body,
        grid=(na,),
        in_specs=[
            pl.BlockSpec((1, ng, gw), lambda a: (a, 0, 0)),
            pl.BlockSpec((1, 2, 1, gw), lambda a: (kv(a), 0, 0, 0)),
            pl.BlockSpec((1, 2, gw, hid), lambda a: (kv(a), 0, 0, 0)),
            pl.BlockSpec((1, hid, HEAD_DIM), lambda a: (kv(a), 0, 0)),
        ],
        out_specs=pl.BlockSpec((1, ng, HEAD_DIM), lambda a: (a, 0, 0)),
        out_shape=jax.ShapeDtypeStruct((na, ng, HEAD_DIM), BF16),
        compiler_params=_cparams(("parallel",)),
        name="compress_prompt",
    )(groups, pos, w1, w2)


SLC_CHUNK = 512
CHUNK_BLOCK_SHIFT = 3
N_SLOPE_COLS = 6
STRIP = Q_BLOCK


def _nsa_prompt_body(q_ref, gate_ref, sl_ref, kca_ref, vc_ref, ka_ref, vsa_ref, kwa_ref, vwa_ref, ov_ref,
                     o_ref, s_ref, p_ref, bias_ref, psum_ref, sw_ref, pw_ref, bw_ref, s2_ref, mx_ref, bias2_ref,
                     m_ref, alpha_ref, acc_ref, todo_ref, *, n, n_top):
    g_n = NSA_GROUP
    qb = Q_BLOCK
    nrow = g_n * qb
    nstrip = nrow // STRIP
    i = pl.program_id(0)
    start = i * qb
    q = q_ref[...]
    qs = jnp.concatenate([q[:, HEAD_DIM * g:HEAD_DIM * (g + 1)] for g in range(g_n)], axis=0)
    qc = jnp.concatenate([qs, sl_ref[...]], axis=1)

    def strip_rows(k):
        r0 = k * STRIP
        t0 = r0 % qb
        return slice(r0, r0 + STRIP), slice(t0, t0 + STRIP)

    def for_strips(fn):
        for k in range(nstrip):
            fn(k, None)

    def softmax_strip(s):
        m = jnp.maximum(jnp.max(s, axis=-1, keepdims=True), M_INIT)
        e = jnp.exp2(s - m)
        return e * (1.0 / jnp.maximum(jnp.sum(e, axis=-1, keepdims=True), 1e-30))

    ncp = kca_ref.shape[1]
    t_c = lax.broadcasted_iota(jnp.int32, (qb, ncp), 0)
    c_c = lax.broadcasted_iota(jnp.int32, (qb, ncp), 1)
    bias_ref[:, 0:ncp] = jnp.where(t_c - CMP_STRIDE * c_c + (start - (CMP_BLOCK - 1)) >= 0, 0.0, NEG)
    s_ref[:, 0:ncp] = _dot(qc, kca_ref[...])
    psum_ref[...] = jnp.zeros(psum_ref.shape, F32)

    def cmp_strip(k, carry):
        rows, trows = strip_rows(k)
        p = softmax_strip(s_ref[rows, 0:ncp] + bias_ref[trows, 0:ncp])
        psum_ref[trows, :] += p
        p_ref[rows, 0:ncp] = p.astype(BF16)
        return carry

    for_strips(cmp_strip)
    o_cmp = _dot(p_ref[:, 0:ncp], vc_ref[...])

    nw = WINDOW + qb
    t_w = lax.broadcasted_iota(jnp.int32, (qb, nw), 0)
    w_w = lax.broadcasted_iota(jnp.int32, (qb, nw), 1)
    dist_w = t_w + WINDOW - w_w
    mask_w = (dist_w >= 0) & (dist_w < WINDOW) & (w_w >= WINDOW - start)
    bw_ref[...] = jnp.where(mask_w, 0.0, NEG)
    w0 = pl.multiple_of(start, qb)
    sw_ref[...] = _dot(qc, kwa_ref[:, pl.ds(w0, nw)])

    def win_strip(k, carry):
        rows, trows = strip_rows(k)
        s = sw_ref[rows, :] + bw_ref[trows, :]
        m = jnp.max(s, axis=-1, keepdims=True)
        pw_ref[rows, :] = jnp.exp2(s - m).astype(BF16)
        return carry

    for_strips(win_strip)
    win = _dot_nt(pw_ref[...], vwa_ref[:, pl.ds(w0, nw)])

    ph, plo = _split2(psum_ref[...])
    ov = ov_ref[...]
    score = _dot(ph, ov) + _dot(plo, ov)
    t_b = lax.broadcasted_iota(jnp.int32, score.shape, 0)
    qblk = jnp.right_shift(start + t_b, SLC_SHIFT)
    sel = _select_blocks(score, qblk, n_top)

    ck = SLC_CHUNK
    notsel = (1.0 - sel).astype(BF16)
    qa = jnp.concatenate([qc, jnp.concatenate([notsel] * g_n, axis=0)], axis=1)
    m_ref[...] = jnp.full(m_ref.shape, M_INIT, F32)
    acc_ref[...] = jnp.zeros(acc_ref.shape, F32)
    c_last = lax.div(start, ck)
    t_s = lax.broadcasted_iota(jnp.int32, (qb, ck), 0)
    s_s = lax.broadcasted_iota(jnp.int32, (qb, ck), 1)
    bias2_ref[0] = jnp.zeros((qb, ck), F32)
    bias2_ref[1] = jnp.where(t_s - s_s + (start - c_last * ck) >= 0, 0.0, NEG)

    blk_any = jnp.max(sel, axis=0, keepdims=True)
    lane = lax.broadcasted_iota(jnp.int32, blk_any.shape, 1)
    cnt = jnp.int32(0)
    for c in range(todo_ref.shape[0] - 2):
        used = jnp.max(jnp.where(jnp.right_shift(lane, CHUNK_BLOCK_SHIFT) == c, blk_any, 0.0)) > 0.5
        todo_ref[cnt] = c
        cnt = cnt + jnp.logical_and(used, c < c_last).astype(jnp.int32)
    todo_ref[cnt] = c_last
    todo_ref[cnt + 1] = c_last

    def scores(c, slot):
        k0 = pl.multiple_of(c * ck, ck)
        last = (c == c_last).astype(jnp.int32)
        s = _dot(qa, ka_ref[:, pl.ds(k0, ck)])
        for k in range(nstrip):
            rows, _ = strip_rows(k)
            sk = s[rows] + bias2_ref[last]
            s2_ref[slot, rows, :] = sk
            mx_ref[slot, rows, :] = jnp.broadcast_to(jnp.max(sk, axis=-1, keepdims=True), (qb, LANE))

    def accumulate(c, slot):
        k0 = pl.multiple_of(c * ck, ck)
        for k in range(nstrip):
            rows, _ = strip_rows(k)
            m_old = m_ref[rows, :]
            m_new = jnp.maximum(m_old, mx_ref[slot, rows, :])
            alpha_ref[rows, :] = jnp.exp2(m_old - m_new)
            m_ref[rows, :] = m_new
            p_ref[rows, 0:ck] = jnp.exp2(s2_ref[slot, rows, :] - jnp.tile(m_new, (1, ck // LANE))).astype(BF16)
        acc_ref[...] = alpha_ref[...] * acc_ref[...] + _dot_nt(p_ref[:, 0:ck], vsa_ref[:, pl.ds(k0, ck)])

    scores(todo_ref[0], 0)
    ntrip = cnt + 1

    def visit_pair(i, carry):
        j = 2 * i
        scores(todo_ref[j + 1], 1)
        accumulate(todo_ref[j], 0)
        scores(todo_ref[j + 2], 0)
        accumulate(todo_ref[j + 1], 1)
        return carry

    lax.fori_loop(0, ntrip // 2, visit_pair, 0)

    @pl.when(ntrip % 2 == 1)
    def _():
        accumulate(todo_ref[ntrip - 1], 0)

    gate = jax.nn.sigmoid(gate_ref[...])
    hd = HEAD_DIM

    def normalised(x):
        return x[:, 0:hd] * (1.0 / jnp.maximum(x[:, hd:hd + 1], 1e-30))

    outs = []
    for g in range(g_n):
        c0 = 3 * (n * g_n + g)
        rows = slice(g * qb, (g + 1) * qb)
        outs.append(gate[:, c0:c0 + 1] * o_cmp[rows] + gate[:, c0 + 1:c0 + 2] * normalised(acc_ref[rows, :])
                    + gate[:, c0 + 2:c0 + 3] * normalised(win[rows]))
    o_ref[...] = jnp.concatenate(outs, axis=-1)


def _nsa_prompt(main, q16, kvtb, kwtp, kct, vc, n):
    t = main.shape[0]
    ncp = kct.shape[1]
    ns = t // SLC_BLOCK
    assert ns <= LANE and t % SLC_CHUNK == 0
    g_n, hd = NSA_GROUP, HEAD_DIM
    nrow = g_n * Q_BLOCK
    nw = WINDOW + Q_BLOCK
    ov = jnp.asarray(_overlap_matrix(ncp, LANE), BF16)
    sl = np.zeros((g_n, Q_BLOCK, hd), np.float32)
    parts = _np_split3(_alibi_slopes()[n * g_n:(n + 1) * g_n].astype(np.float64) * LOG2E)
    for j in range(N_SLOPE_COLS):
        sl[:, :, j] = parts[j % 3][:, None]
    sl = jnp.asarray(sl.reshape(nrow, hd), BF16)

    def with_pos_rows(kt, pos):
        rows = np.zeros((hd, pos.shape[0]), np.float32)
        rows[0:3] = (pos // SLC_BLOCK) * SLC_BLOCK
        rows[3:6] = pos % SLC_BLOCK
        return jnp.concatenate([kt, jnp.asarray(rows, BF16)], axis=0)

    ka = jnp.concatenate([with_pos_rows(kvtb[R_KSLC + n * hd:R_KSLC + (n + 1) * hd], np.arange(t)),
                          jnp.asarray(NEG * _expand_matrix(LANE, t), BF16)], axis=0)
    kca = with_pos_rows(kct, np.arange(ncp) * CMP_STRIDE + (CMP_BLOCK - 1))
    kwa = with_pos_rows(kwtp[n * hd:(n + 1) * hd], np.arange(kwtp.shape[1]))

    def with_ones_row(vt):
        rows = np.zeros((hd, vt.shape[1]), np.float32)
        rows[0] = 1.0
        return jnp.concatenate([vt, jnp.asarray(rows, BF16)], axis=0)

    vsa = with_ones_row(kvtb[R_VSLC + n * hd:R_VSLC + (n + 1) * hd])
    vwa = with_ones_row(kwtp[KVW + n * hd:KVW + (n + 1) * hd])
    wq = g_n * hd
    whole = lambda a: pl.BlockSpec(a.shape, lambda i: (0,) * a.ndim)
    return pl.pallas_call(
        functools.partial(_nsa_prompt_body, n=n, n_top=min(SLC_TOP, ns)),
        grid=(t // Q_BLOCK,),
        in_specs=[
            pl.BlockSpec((Q_BLOCK, wq), lambda i: (i, n)),
            pl.BlockSpec((Q_BLOCK, LANE), lambda i: (i, C_NGATE // LANE)),
            whole(sl), whole(kca), whole(vc), whole(ka), whole(vsa), whole(kwa), whole(vwa), whole(ov),
        ],
        out_specs=pl.BlockSpec((Q_BLOCK, wq), lambda i: (i, 0)),
        out_shape=jax.ShapeDtypeStruct((t, wq), F32),
        scratch_shapes=[
            pltpu.VMEM((nrow, ncp), F32),
            pltpu.VMEM((nrow, max(ncp, SLC_CHUNK)), BF16),
            pltpu.VMEM((Q_BLOCK, ncp), F32),
            pltpu.VMEM((Q_BLOCK, ncp), F32),
            pltpu.VMEM((nrow, nw), F32),
            pltpu.VMEM((nrow, nw), BF16),
            pltpu.VMEM((Q_BLOCK, nw), F32),
            pltpu.VMEM((2, nrow, SLC_CHUNK), F32),
            pltpu.VMEM((2, nrow, LANE), F32),
            pltpu.VMEM((2, Q_BLOCK, SLC_CHUNK), F32),
            pltpu.VMEM((nrow, LANE), F32),
            pltpu.VMEM((nrow, LANE), F32),
            pltpu.VMEM((nrow, 2 * hd), F32),
            pltpu.SMEM((t // SLC_CHUNK + 2,), jnp.int32),
        ],
        compiler_params=_cparams(("parallel",)),
        name=f"nsa_prompt_{n}",
    )(q16, main, sl, kca, vc, ka, vsa, kwa, vwa, ov)


N_KINDS = 4
SAMPLE_PER_STEP = 2


def _nsa_sample_body(pt_ref, cache_ref, q_ref, new_ref, win_ref, gate_ref, slope_ref,
                     posk_ref, w1k_ref, w2k_ref, posv_ref, w1v_ref, w2v_ref, ov_ref, ex_ref, same_ref,
                     o_ref, wout_ref, buf_ref, rows_ref, newp_ref, sem, *, tn, n_top):
    b = pl.program_id(0)
    nsteps = pl.num_programs(0)
    nbs, npages = buf_ref.shape[1], buf_ref.shape[2]
    slot = b % 2

    def page_copy(step, sl, e, j):
        return pltpu.make_async_copy(cache_ref.at[pt_ref[step * nbs + e, j]], buf_ref.at[sl, e, j], sem.at[sl])

    def all_pages(step, sl, fn):
        for e in range(nbs):
            for j in range(npages):
                fn(page_copy(step, sl, e, j))

    @pl.when(b == 0)
    def _():
        all_pages(0, 0, lambda c: c.start())
        newp_ref[...] = jnp.zeros(newp_ref.shape, F32)

    @pl.when(b + 1 < nsteps)
    def _():
        all_pages(b + 1, 1 - slot, lambda c: c.start())

    all_pages(b, slot, lambda c: c.wait())
    for e in range(nbs):
        one = pl.ds(e, 1)
        _nsa_sample_one(buf_ref.at[slot, e], q_ref.at[one], new_ref.at[one], win_ref.at[one], gate_ref.at[one],
                        slope_ref, posk_ref, w1k_ref, w2k_ref, posv_ref, w1v_ref, w2v_ref, ov_ref, ex_ref,
                        same_ref, o_ref.at[one], wout_ref.at[one], rows_ref.at[e], newp_ref.at[e],
                        tn=tn, n_top=n_top)


def _nsa_sample_one(pg, q_ref, new_ref, win_ref, gate_ref, slope_ref,
                    posk_ref, w1k_ref, w2k_ref, posv_ref, w1v_ref, w2v_ref, ov_ref, ex_ref, same_ref,
                    o_ref, wout_ref, rows_ref, newp_ref, *, tn, n_top):
    npages, page = pg.shape[0], pg.shape[3]
    past_len = npages * page

    for a in range(4):
        newp_ref[a, :, 0:tn] = new_ref[0, (2 + a) * KVW:(3 + a) * KVW, :]

    ncp = past_len // CMP_STRIDE
    for a in range(2):
        for j in range(npages):
            rows_ref[a, j * page:(j + 1) * page, :] = jnp.transpose(pg[j, a])
        rows_ref[a, past_len:past_len + LANE, :] = jnp.zeros((LANE, KVW), F32)

    def compress(a, pos_ref, w1_ref, w2_ref):
        x = jnp.concatenate([(rows_ref[a, pl.ds(p, ncp, stride=CMP_STRIDE), :] + pos_ref[p]).astype(BF16)
                             for p in range(CMP_BLOCK)], axis=1)
        return _dot(_gelu_tanh(_dot(x, w1_ref[...])).astype(BF16), w2_ref[...]).astype(BF16)

    kc = compress(0, posk_ref, w1k_ref, w2k_ref)
    vc = compress(1, posv_ref, w1v_ref, w2v_ref)

    rows = q_ref.shape[1]
    q = q_ref[0]
    slope = slope_ref[:, 0:1]
    r_i = lax.broadcasted_iota(jnp.int32, (rows, 1), 0)
    t_i = r_i & (tn - 1)
    qpos = past_len + t_i
    low = r_i < rows // 2

    def pick_head(o):
        return jnp.where(low, o[:, 0:HEAD_DIM], o[:, HEAD_DIM:KVW])

    cpos = CMP_STRIDE * lax.broadcasted_iota(jnp.int32, (rows, ncp), 1) + (CMP_BLOCK - 1)
    dist_c = (qpos - cpos).astype(F32)
    mask_c = dist_c >= 0.0
    s = jnp.where(mask_c, _dot_nt(q, kc) - slope * dist_c, NEG)
    p_c = _softmax2_rows(s, mask_c)
    o_cmp = pick_head(_dot(p_c.astype(BF16), vc))

    same = same_ref[...]
    ph, plo = _split2(p_c)
    psum = _dot(same, ph) + _dot(same, plo)
    ph, plo = _split2(psum)
    ov = ov_ref[...]
    score = _dot(ph, ov) + _dot(plo, ov)
    sel = _select_blocks(score, jnp.right_shift(qpos, SLC_SHIFT), n_top)

    u_i = lax.broadcasted_iota(jnp.int32, (rows, LANE), 1)
    new_ok = u_i <= t_i
    new_blk = past_len // SLC_BLOCK
    keep = _dot(sel.astype(BF16), ex_ref[...]) > 0.5
    spos = lax.broadcasted_iota(jnp.int32, (rows, past_len), 1)
    ks = jnp.concatenate([pg[j, 2].astype(BF16) for j in range(npages)], axis=1)
    vs = jnp.concatenate([pg[j, 3].astype(BF16) for j in range(npages)], axis=1)
    s_past = _dot(q, ks)
    s_past = jnp.where(keep, s_past - slope * (qpos - spos).astype(F32), NEG)
    keep_new = new_ok & (sel[:, new_blk:new_blk + 1] > 0.5)
    s_new = jnp.where(keep_new, _dot(q, newp_ref[0].astype(BF16)) - slope * (t_i - u_i).astype(F32), NEG)
    m = jnp.maximum(jnp.max(s_past, axis=-1, keepdims=True), jnp.max(s_new, axis=-1, keepdims=True))
    e_past = jnp.where(keep, jnp.exp2(s_past - m), 0.0)
    e_new = jnp.where(keep_new, jnp.exp2(s_new - m), 0.0)
    l = jnp.sum(e_past, axis=-1, keepdims=True) + jnp.sum(e_new, axis=-1, keepdims=True)
    o_sel = _dot_nt(e_new.astype(BF16), newp_ref[1].astype(BF16)) + _dot_nt(e_past.astype(BF16), vs)
    o_sel = pick_head(o_sel * (1.0 / jnp.maximum(l, 1e-30)))

    wl = win_ref.shape[3]
    w_i = lax.broadcasted_iota(jnp.int32, (rows, wl), 1)
    dist_wi = (wl + t_i) - w_i
    mask_w = (dist_wi < WINDOW) & (w_i >= wl - past_len)
    s_win = jnp.where(mask_w, _dot(q, win_ref[0, 0].astype(BF16)) - slope * dist_wi.astype(F32), NEG)
    s_new = jnp.where(new_ok, _dot(q, newp_ref[2].astype(BF16)) - slope * (t_i - u_i).astype(F32), NEG)
    m = jnp.maximum(jnp.max(s_win, axis=-1, keepdims=True), jnp.max(s_new, axis=-1, keepdims=True))
    e_win = jnp.where(mask_w, jnp.exp2(s_win - m), 0.0)
    e_new = jnp.where(new_ok, jnp.exp2(s_new - m), 0.0)
    l = jnp.sum(e_win, axis=-1, keepdims=True) + jnp.sum(e_new, axis=-1, keepdims=True)
    o_win = (_dot_nt(e_win.astype(BF16), win_ref[0, 1].astype(BF16))
             + _dot_nt(e_new.astype(BF16), newp_ref[3].astype(BF16)))
    o_win = pick_head(o_win * (1.0 / jnp.maximum(l, 1e-30)))

    gate = jax.nn.sigmoid(gate_ref[0])
    o_ref[0] = gate[:, 0:1] * o_cmp + gate[:, 1:2] * o_sel + gate[:, 2:3] * o_win

    lane_w = lax.broadcasted_iota(jnp.int32, (KVW, LANE), 1)
    for a in range(2):
        shifted = pltpu.roll(win_ref[0, a], wl - tn, 1)
        tail = pltpu.roll(newp_ref[2 + a], LANE - tn, 1)
        wout_ref[0, a, :, 0:wl - LANE] = shifted[:, 0:wl - LANE]
        wout_ref[0, a, :, wl - LANE:wl] = jnp.where(lane_w >= LANE - tn, tail, shifted[:, wl - LANE:wl])


def _nsa_sample(page_table, cache, qrows, newt, win, gates, cmpk, cmpv, *, tn):
    nb, npages = page_table.shape
    page = cache.shape[3]
    past_len = npages * page
    wl = win.shape[3]
    rows = qrows.shape[1]
    assert tn == 4 and page == LANE and wl % LANE == 0 and wl > LANE and past_len >= wl
    ncp = past_len // CMP_STRIDE
    ns = -(-(past_len + tn) // SLC_BLOCK)
    assert ncp <= LANE and ns <= LANE
    r = np.arange(rows)
    h = r // tn
    slope = jnp.asarray(np.broadcast_to((_alibi_slopes()[h].astype(np.float64) * LOG2E)[:, None], (rows, LANE)), F32)
    nt = h // NSA_GROUP * tn + r % tn
    same = jnp.asarray((nt[:, None] == nt[None, :]).astype(np.float32), BF16)
    ov = jnp.asarray(_overlap_matrix(ncp, LANE), BF16)
    ex = jnp.asarray(_expand_matrix(LANE, past_len), BF16)
    posk, w1k, w2k = cmpk
    posv, w1v, w2v = cmpv
    nbs = SAMPLE_PER_STEP
    assert nb % nbs == 0
    const = lambda a: pl.BlockSpec(a.shape, lambda b, pt: (0,) * a.ndim)
    per_b = lambda a: pl.BlockSpec((nbs,) + a.shape[1:], lambda b, pt: (b,) + (0,) * (a.ndim - 1))
    grid_spec = pltpu.PrefetchScalarGridSpec(
        num_scalar_prefetch=1,
        grid=(nb // nbs,),
        in_specs=[
            pl.BlockSpec(memory_space=pl.ANY),
            per_b(qrows), per_b(newt), per_b(win), per_b(gates), const(slope),
            const(posk), const(w1k), const(w2k), const(posv), const(w1v), const(w2v),
            const(ov), const(ex), const(same),
        ],
        out_specs=[pl.BlockSpec((nbs, rows, HEAD_DIM), lambda b, pt: (b, 0, 0)), per_b(win)],
        scratch_shapes=[
            pltpu.VMEM((2, nbs, npages, N_KINDS, KVW, page), F32),
            pltpu.VMEM((nbs, 2, past_len + LANE, KVW), F32),
            pltpu.VMEM((nbs, 4, KVW, LANE), F32),
            pltpu.SemaphoreType.DMA((2,)),
        ],
    )
    return pl.pallas_call(
        functools.partial(_nsa_sample_body, tn=tn, n_top=min(SLC_TOP, ns)),
        grid_spec=grid_spec,
        out_shape=[jax.ShapeDtypeStruct((nb, rows, HEAD_DIM), F32), jax.ShapeDtypeStruct(win.shape, F32)],
        compiler_params=_cparams(("arbitrary",)),
        name="nsa_sample",
    )(page_table, cache, qrows, newt, win, gates, slope, posk, w1k, w2k, posv, w1v, w2v, ov, ex, same)


def _prep_ffn(wg, wu, wd, tf=512):
    wg, wu, wd = wg.astype(BF16), wu.astype(BF16), wd.astype(BF16)
    main = wg.shape[1] // tf * tf
    return wg, wu, wd, wg[:, main:], wu[:, main:], wd[main:]


def _prep_w_in(w_in):
    wt = w_in.T
    d = wt.shape[1]
    rows = [wt[:W_NSAQ], wt[W_GA:], wt[W_KV:W_KV + 2 * KVW], wt[W_NGATE:W_GA]]
    used = sum(r.shape[0] for r in rows)
    w_main = jnp.concatenate(rows + [jnp.zeros((D_MAIN - used, d), wt.dtype)], axis=0).astype(BF16)
    return w_main, wt[W_NSAQ:W_KV].astype(BF16), wt[W_KV:W_NGATE].astype(BF16)


def _prep_cmp_prompt(pos_k, w1_k, w2_k, pos_v, w1_v, w2_v):
    half = CMP_STRIDE * HEAD_DIM
    pos = jnp.stack([pos_k.reshape(2, 1, half), pos_v.reshape(2, 1, half)])
    w1 = jnp.stack([w1_k.reshape(2, half, -1), w1_v.reshape(2, half, -1)]).astype(BF16)
    w2 = jnp.stack([w2_k, w2_v]).astype(BF16)
    return pos, w1, w2


def _prep_cmp_sample(pos, w1, w2):
    hid = w1.shape[1]
    w1p = w1.reshape(CMP_BLOCK, HEAD_DIM, hid)
    z1 = jnp.zeros_like(w1p)
    w1b = jnp.concatenate([jnp.concatenate([w1p, z1], axis=2), jnp.concatenate([z1, w1p], axis=2)], axis=1)
    z2 = jnp.zeros_like(w2)
    w2b = jnp.concatenate([jnp.concatenate([w2, z2], axis=1), jnp.concatenate([z2, w2], axis=1)], axis=0)
    posb = jnp.concatenate([pos, pos], axis=1)[:, None, :]
    return posb, w1b.reshape(CMP_BLOCK * KVW, 2 * hid).astype(BF16), w2b.astype(BF16)


def _project(x, g, w_main, w_q, w_kvt):
    main = _inproj(x, g, w_main, tm=1024)
    q16 = _inproj(x, g, w_q, out_dtype=BF16, scale=HEAD_DIM ** -0.5 * LOG2E)
    kvt, kvtb = _inproj_t(x, g, w_kvt)
    return main, q16, kvt, kvtb


def _mix_prompt(main, q16, kvtb, lb, gnorm, cmp_w):
    t = main.shape[0]
    o_hg, s_fin = _hgrn_prompt(main, lb, gnorm)
    nkv, hd = NSA_KV_HEADS, HEAD_DIM
    groups = main[:, C_CMP:C_CMP + 2 * KVW].reshape(t // CMP_STRIDE, CMP_STRIDE, 2, nkv, hd)
    groups = groups.transpose(2, 3, 0, 1, 4).reshape(2 * nkv, t // CMP_STRIDE, CMP_STRIDE * hd)
    cmp = _compress_prompt(groups, *cmp_w)
    kct = cmp[:nkv].transpose(0, 2, 1)
    vc = cmp[nkv:]
    kwtp = jnp.pad(kvtb[R_KWIN:N_KV_ROWS], ((0, 0), (WINDOW, 0)))
    o_nsa = [_nsa_prompt(main, q16, kvtb, kwtp, kct[n], vc[n], n) for n in range(nkv)]
    return o_hg, o_nsa, s_fin


def _mix_sample(main, q16, kvt, lb, gnorm, state, cache, win, page_table, cmp_w, *, tn):
    nb = state.shape[0]
    o_hg, s_new = _hgrn_sample(main, lb, gnorm, state, tn=tn)
    g_n, nkv, hd = NSA_GROUP, NSA_KV_HEADS, HEAD_DIM
    q = q16.reshape(nb, tn, nkv, g_n, hd).transpose(0, 2, 3, 1, 4)
    eye = jnp.eye(nkv, dtype=BF16)
    qrows = (q[:, :, :, :, None, :] * eye[None, :, None, None, :, None]).reshape(nb, nkv * g_n * tn, nkv * hd)
    gates = main[:, C_NGATE:C_NGATE + N_NGATE].reshape(nb, tn, NSA_HEADS, 3).transpose(0, 2, 1, 3)
    gates = jnp.pad(gates.reshape(nb, NSA_HEADS * tn, 3), ((0, 0), (0, 0), (0, LANE - 3)))
    newt = kvt.reshape(N_KV_ROWS, nb, tn).transpose(1, 0, 2)
    o, wout = _nsa_sample(page_table, cache, qrows, newt, win, gates, *cmp_w, tn=tn)
    o = o.reshape(nb, nkv, g_n, tn, hd).transpose(0, 3, 1, 2, 4).reshape(nb * tn, nkv, g_n * hd)
    return o_hg, [o[:, n] for n in range(nkv)], s_new, wout


def kernel(x_prompt, x_sample, cache_kv, cache_win, state_hgrn, page_table, norm_pre1, norm_post1, ff1_gate, ff1_up, ff1_down, norm_pre2, norm_post2, w_in, hg_lb, hg_gnorm, cmp_pos_k, cmp_w1_k, cmp_w2_k, cmp_pos_v, cmp_w1_v, cmp_w2_v, w_proj_hg, w_proj_nsa, w_out, norm_pre3, norm_post3, ff2_gate, ff2_up, ff2_down):
    depth = norm_pre1.shape[0]
    bp, tp, d = x_prompt.shape
    nb, tn, _ = x_sample.shape
    assert bp == 1
    nkv, hd = NSA_KV_HEADS, HEAD_DIM
    lb_all = jnp.cumsum(jax.nn.softmax(hg_lb.astype(F32), axis=0), axis=0)[:depth]
    xp = x_prompt.reshape(tp, d)
    xs = x_sample.reshape(nb * tn, d)
    kv_p, kv_s, win_p, win_s, st_p, st_s = [], [], [], [], [], []
    row = lambda a: a.reshape(1, -1)
    for l in range(depth):
        ff1 = _prep_ffn(ff1_gate[l], ff1_up[l], ff1_down[l])
        ff2 = _prep_ffn(ff2_gate[l], ff2_up[l], ff2_down[l])
        w_in_l = _prep_w_in(w_in[l])
        wh, wn, wo = w_proj_hg[l].astype(BF16), w_proj_nsa[l].astype(BF16), w_out[l].astype(BF16)
        cmp_p = _prep_cmp_prompt(cmp_pos_k[l], cmp_w1_k[l], cmp_w2_k[l], cmp_pos_v[l], cmp_w1_v[l], cmp_w2_v[l])
        cmp_s = (_prep_cmp_sample(cmp_pos_k[l], cmp_w1_k[l], cmp_w2_k[l]),
                 _prep_cmp_sample(cmp_pos_v[l], cmp_w1_v[l], cmp_w2_v[l]))
        lb, gnorm = row(lb_all[l]), row(hg_gnorm[l])
        n_pool, page = cache_kv.shape[1], cache_kv.shape[2]
        cache = cache_kv[l].transpose(0, 2, 3, 4, 1).reshape(n_pool, N_KINDS, KVW, page)
        wl = cache_win.shape[2]
        win = cache_win[l].transpose(0, 2, 3, 4, 1).reshape(nb, 2, KVW, wl)

        xp = _ffn(xp, row(norm_pre1[l]), row(norm_post1[l]), *ff1)
        xs = _ffn(xs, row(norm_pre1[l]), row(norm_post1[l]), *ff1)

        main_p, q_p, kvt_p, kvtb_p = _project(xp, row(norm_pre2[l]), *w_in_l)
        main_s, q_s, kvt_s, _ = _project(xs, row(norm_pre2[l]), *w_in_l)
        ohg_p, onsa_p, s_p = _mix_prompt(main_p, q_p, kvtb_p, lb, gnorm, cmp_p)
        ohg_s, onsa_s, s_s, wout = _mix_sample(main_s, q_s, kvt_s, lb, gnorm, state_hgrn[l], cache, win,
                                               page_table, cmp_s, tn=tn)
        xp = _outproj(_merge(ohg_p, *onsa_p, main_p, wh, wn), xp, wo, row(norm_post2[l]))
        xs = _outproj(_merge(ohg_s, *onsa_s, main_s, wh, wn), xs, wo, row(norm_post2[l]))

        kv_p.append(kvt_p[:N_KINDS * KVW].reshape(N_KINDS, nkv, hd, bp, tp).transpose(3, 4, 0, 1, 2))
        kv_s.append(kvt_s[:N_KINDS * KVW].reshape(N_KINDS, nkv, hd, nb, tn).transpose(3, 4, 0, 1, 2))
        wp = min(WINDOW, tp)
        win_p.append(kvt_p[R_KWIN:, tp - wp:].reshape(2, nkv, hd, bp, wp).transpose(3, 4, 0, 1, 2))
        win_s.append(wout.reshape(nb, 2, nkv, hd, wl).transpose(0, 4, 1, 2, 3))
        st_p.append(s_p[None])
        st_s.append(s_s)

        xp = _ffn(xp, row(norm_pre3[l]), row(norm_post3[l]), *ff2)
        xs = _ffn(xs, row(norm_pre3[l]), row(norm_post3[l]), *ff2)
    return (xp.reshape(bp, tp, d), xs.reshape(nb, tn, d), jnp.stack(kv_p), jnp.stack(kv_s),
            jnp.stack(win_p), jnp.stack(win_s), jnp.stack(st_p), jnp.stack(st_s))
```

```python
import functools

import numpy as np
import jax
import jax.numpy as jnp
from jax import lax
from jax.experimental import pallas as pl
from jax.experimental.pallas import tpu as pltpu

F32 = jnp.float32
BF16 = jnp.bfloat16

EPS = 1e-6
HG_HEADS = 8
HG_D = 128
NSA_HEADS = 16
NSA_KV_HEADS = 2
NSA_GROUP = NSA_HEADS // NSA_KV_HEADS
HEAD_DIM = 64
CMP_BLOCK = 32
CMP_STRIDE = 16
SLC_BLOCK = 64
SLC_SHIFT = 6
SLC_TOP = 16
WINDOW = 512
Q_BLOCK = 128
FORCE_SCORE = 1.0e4
NEG = -1.0e30
NEG_PICKED = -3.0e38
M_INIT = -1.0e20
LOG2E = float(np.log2(np.e))

LANE = 128
VMEM_LIMIT = 56 * 1024 * 1024

W_NSAQ, W_KV, W_NGATE, W_GA = 4096, 5120, 5888, 5936
N_NGATE = 3 * NSA_HEADS
KVW = NSA_KV_HEADS * HEAD_DIM
C_HGQ, C_HGF, C_HGI, C_HGG = 0, 1024, 2048, 3072
C_GA, C_GB = 4096, 6144
C_CMP = 8192
C_NGATE = 8448
C_Q = 8704
D_MAIN = 9728
Q_SCALE = HEAD_DIM ** -0.5 * LOG2E
R_KSLC, R_VSLC, R_KWIN, R_VWIN = 2 * KVW, 3 * KVW, 4 * KVW, 5 * KVW
N_KV_ROWS = 6 * KVW


def _cparams(sem):
    return pltpu.CompilerParams(dimension_semantics=sem, vmem_limit_bytes=VMEM_LIMIT)


def _rms(x, g):
    return x * lax.rsqrt(jnp.mean(x * x, axis=-1, keepdims=True) + EPS) * g


def _silu(x):
    return x * jax.nn.sigmoid(x)


def _dot(a, b):
    return jnp.dot(a, b, preferred_element_type=F32)


def _dot_nt(a, b):
    return lax.dot_general(a, b, (((1,), (1,)), ((), ())), preferred_element_type=F32)


def _dot_tn(a, b):
    return lax.dot_general(a, b, (((0,), (0,)), ((), ())), preferred_element_type=F32)


def _split3(x):
    hi = x.astype(BF16)
    r = x - hi.astype(F32)
    mid = r.astype(BF16)
    lo = (r - mid.astype(F32)).astype(BF16)
    return hi, mid, lo


def _split2(x):
    hi = x.astype(BF16)
    return hi, (x - hi.astype(F32)).astype(BF16)


def _ffn_body(x_ref, npre_ref, npost_ref, wg_ref, wu_ref, wd_ref, wgr_ref, wur_ref, wdr_ref, o_ref, h_ref, acc_ref):
    f = pl.program_id(1)

    @pl.when(f == 0)
    def _():
        h_ref[...] = _rms(x_ref[...], npre_ref[...]).astype(BF16)
        acc_ref[...] = jnp.zeros_like(acc_ref)

    def partial_down(wg, wu, wd):
        h = h_ref[...]
        a = _silu(_dot(h, wg[...])) * _dot(h, wu[...])
        return _dot(a.astype(BF16), wd[...])

    acc_ref[...] += partial_down(wg_ref, wu_ref, wd_ref)

    @pl.when(f == pl.num_programs(1) - 1)
    def _():
        y = acc_ref[...] + partial_down(wgr_ref, wur_ref, wdr_ref)
        o_ref[...] = x_ref[...] + 0.5 * _rms(y, npost_ref[...])


def _ffn(x, npre, npost, wg, wu, wd, wgr, wur, wdr, *, tm=512, tf=512):
    n, d = x.shape
    dff = wg.shape[1]
    rem = wgr.shape[1]
    assert rem > 0 and (dff - rem) % tf == 0 and rem % LANE == 0
    tm = min(tm, n)
    return pl.pallas_call(
        _ffn_body,
        grid=(n // tm, (dff - rem) // tf),
        in_specs=[
            pl.BlockSpec((tm, d), lambda i, f: (i, 0)),
            pl.BlockSpec((1, d), lambda i, f: (0, 0)),
            pl.BlockSpec((1, d), lambda i, f: (0, 0)),
            pl.BlockSpec((d, tf), lambda i, f: (0, f)),
            pl.BlockSpec((d, tf), lambda i, f: (0, f)),
            pl.BlockSpec((tf, d), lambda i, f: (f, 0)),
            pl.BlockSpec((d, rem), lambda i, f: (0, 0)),
            pl.BlockSpec((d, rem), lambda i, f: (0, 0)),
            pl.BlockSpec((rem, d), lambda i, f: (0, 0)),
        ],
        out_specs=pl.BlockSpec((tm, d), lambda i, f: (i, 0)),
        out_shape=jax.ShapeDtypeStruct((n, d), F32),
        scratch_shapes=[pltpu.VMEM((tm, d), BF16), pltpu.VMEM((tm, d), F32)],
        compiler_params=_cparams(("parallel", "arbitrary")),
        name="ffn",
    )(x, npre, npost, wg, wu, wd, wgr, wur, wdr)


def _inproj_body(x_ref, g_ref, wt_ref, o_ref, h_ref):
    @pl.when(pl.program_id(1) == 0)
    def _():
        h_ref[...] = _rms(x_ref[...], g_ref[...]).astype(BF16)

    o_ref[...] = _dot_nt(h_ref[...], wt_ref[...])


def _inproj(x, g, wt, *, tm=512, tn=512):
    n, d = x.shape
    dout = wt.shape[0]
    tm = min(tm, n)
    return pl.pallas_call(
        _inproj_body,
        grid=(n // tm, dout // tn),
        in_specs=[
            pl.BlockSpec((tm, d), lambda i, j: (i, 0)),
            pl.BlockSpec((1, d), lambda i, j: (0, 0)),
            pl.BlockSpec((tn, d), lambda i, j: (j, 0)),
        ],
        out_specs=pl.BlockSpec((tm, tn), lambda i, j: (i, j)),
        out_shape=jax.ShapeDtypeStruct((n, dout), F32),
        scratch_shapes=[pltpu.VMEM((tm, d), BF16)],
        compiler_params=_cparams(("parallel", "arbitrary")),
        name="inproj",
    )(x, g, wt)


def _inproj_t_body(x_ref, g_ref, wt_ref, o_ref, ob_ref):
    y = _dot_nt(wt_ref[...], _rms(x_ref[...], g_ref[...]).astype(BF16))
    o_ref[...] = y
    ob_ref[...] = y.astype(BF16)


def _inproj_t(x, g, wt, *, tm=512):
    n, d = x.shape
    dout = wt.shape[0]
    tm = min(tm, n)
    return pl.pallas_call(
        _inproj_t_body,
        grid=(n // tm,),
        in_specs=[
            pl.BlockSpec((tm, d), lambda i: (i, 0)),
            pl.BlockSpec((1, d), lambda i: (0, 0)),
            pl.BlockSpec((dout, d), lambda i: (0, 0)),
        ],
        out_specs=[pl.BlockSpec((dout, tm), lambda i: (0, i)), pl.BlockSpec((dout, tm), lambda i: (0, i))],
        out_shape=[jax.ShapeDtypeStruct((dout, n), F32), jax.ShapeDtypeStruct((dout, n), BF16)],
        compiler_params=_cparams(("parallel",)),
        name="inproj_t",
    )(x, g, wt)


def _merge_body(ohg_ref, on0_ref, on1_ref, ga_ref, gb_ref, wh_ref, wn_ref, y_ref):
    onsa = jnp.concatenate([on0_ref[...], on1_ref[...]], axis=-1).astype(BF16)
    yh = _dot(ohg_ref[...].astype(BF16), wh_ref[...])
    yn = _dot(onsa, wn_ref[...])
    y = jax.nn.sigmoid(ga_ref[...]) * yh + jax.nn.sigmoid(gb_ref[...]) * yn
    y_ref[...] = y.astype(BF16)


def _merge(ohg, on0, on1, main, wh, wn, *, tm=512):
    n = ohg.shape[0]
    d = wh.shape[1]
    tm = min(tm, n)
    return pl.pallas_call(
        _merge_body,
        grid=(n // tm,),
        in_specs=[
            pl.BlockSpec((tm, ohg.shape[1]), lambda i: (i, 0)),
            pl.BlockSpec((tm, on0.shape[1]), lambda i: (i, 0)),
            pl.BlockSpec((tm, on1.shape[1]), lambda i: (i, 0)),
            pl.BlockSpec((tm, d), lambda i: (i, C_GA // d)),
            pl.BlockSpec((tm, d), lambda i: (i, C_GB // d)),
            pl.BlockSpec(wh.shape, lambda i: (0, 0)),
            pl.BlockSpec(wn.shape, lambda i: (0, 0)),
        ],
        out_specs=pl.BlockSpec((tm, d), lambda i: (i, 0)),
        out_shape=jax.ShapeDtypeStruct((n, d), BF16),
        compiler_params=_cparams(("parallel",)),
        name="merge",
    )(ohg, on0, on1, main, main, wh, wn)


def _outproj_body(y_ref, x_ref, w_ref, g_ref, o_ref):
    o_ref[...] = x_ref[...] + _rms(_dot(y_ref[...], w_ref[...]), g_ref[...])


def _outproj(y, x, w, g, *, tm=512):
    n, d = x.shape
    tm = min(tm, n)
    return pl.pallas_call(
        _outproj_body,
        grid=(n // tm,),
        in_specs=[
            pl.BlockSpec((tm, d), lambda i: (i, 0)),
            pl.BlockSpec((tm, d), lambda i: (i, 0)),
            pl.BlockSpec(w.shape, lambda i: (0, 0)),
            pl.BlockSpec((1, d), lambda i: (0, 0)),
        ],
        out_specs=pl.BlockSpec((tm, d), lambda i: (i, 0)),
        out_shape=jax.ShapeDtypeStruct((n, d), F32),
        compiler_params=_cparams(("parallel",)),
        name="outproj",
    )(y, x, w, g)


def _hgrn_range_matrix(c):
    t = np.arange(c)[:, None]
    u = np.arange(c)[None, :]
    blocks = [u <= t, u > t]
    m = c // 2
    while m >= 1:
        p = (t // (2 * m)) * 2 * m + m - 1
        upper = (t // m) % 2 == 1
        blocks.append(np.where(upper, (u > p) & (u <= t), (u > t) & (u <= p)))
        m //= 2
    return np.concatenate(blocks, axis=0).astype(np.float32)


def _hgrn_gates(q_raw, f_raw, lb):
    f = lb + (1.0 - lb) * jax.nn.sigmoid(f_raw)
    return _silu(q_raw), jnp.log(f), 1.0 - f


def _hgrn_finish(o, g_raw, gnorm):
    return _rms(o, gnorm) * _silu(g_raw)


def _hgrn_prompt_body(q_ref, f_ref, i_ref, g_ref, lb_ref, gn_ref, rm_ref, o_ref, sfin_ref, s_ref, *, c, hp):
    @pl.when(pl.program_id(1) == 0)
    def _():
        s_ref[...] = jnp.zeros(s_ref.shape, F32)

    cols = [slice(j * HG_D, (j + 1) * HG_D) for j in range(hp)]
    res = [_hgrn_chunk(q_ref[:, sl], f_ref[:, sl], i_ref[:, sl], lb_ref[:, sl], rm_ref[...], s_ref[j], c)
           for j, sl in enumerate(cols)]
    outs = [_hgrn_finish(o, g_ref[:, sl], gn_ref[...]) for (o, _), sl in zip(res, cols)]
    for j, sl in enumerate(cols):
        s_ref[j] = res[j][1]
        sfin_ref[j] = res[j][1]
        o_ref[:, sl] = outs[j]


def _hgrn_chunk(q_raw, f_raw, v, lb, rm, s, c):
    q, logf, k = _hgrn_gates(q_raw, f_raw, lb)
    vb = v.astype(BF16)
    hi, mid, lo = _split3(logf)
    r3 = _dot(rm, jnp.concatenate([hi, mid, lo], axis=1))
    e = r3[:, 0:HG_D] + r3[:, HG_D:2 * HG_D] + r3[:, 2 * HG_D:3 * HG_D]
    b = e[0:c]
    qe = (q * jnp.exp(b)).astype(BF16)
    row = lax.broadcasted_iota(jnp.int32, (c, c), 0)
    col = lax.broadcasted_iota(jnp.int32, (c, c), 1)
    x = row ^ col
    a = jnp.where(x == 0, _dot_nt(q.astype(BF16), k.astype(BF16)), 0.0)
    rowk = lax.broadcasted_iota(jnp.int32, (c, HG_D), 0)
    m = c // 2
    lvl = 0
    while m >= 1:
        w = jnp.exp(e[(2 + lvl) * c:(3 + lvl) * c])
        upper = (rowk & m) != 0
        ql = jnp.where(upper, q * w, 0.0).astype(BF16)
        kl = jnp.where(upper, 0.0, k * w).astype(BF16)
        a = a + jnp.where(x < 2 * m, _dot_nt(ql, kl), 0.0)
        m //= 2
        lvl += 1
    o = _dot(jnp.concatenate([qe, a.astype(BF16)], axis=1), jnp.concatenate([s.astype(BF16), vb], axis=0))
    kd = (k * jnp.exp(e[c:2 * c])).astype(BF16)
    ecol = jnp.transpose(jnp.broadcast_to(jnp.exp(b[c - 1:c, :]), (HG_D, HG_D)))
    return o, ecol * s + _dot_tn(kd, vb)


def _hgrn_prompt(main, lb, gnorm, *, c=128, hp=8):
    t = main.shape[0]
    assert c == HG_D and HG_HEADS % hp == 0
    rm = jnp.asarray(_hgrn_range_matrix(c), BF16)
    w = hp * HG_D
    col = lambda base: (lambda h, ci: (ci, base // w + h))
    return pl.pallas_call(
        functools.partial(_hgrn_prompt_body, c=c, hp=hp),
        grid=(HG_HEADS // hp, t // c),
        in_specs=[
            pl.BlockSpec((c, w), col(C_HGQ)),
            pl.BlockSpec((c, w), col(C_HGF)),
            pl.BlockSpec((c, w), col(C_HGI)),
            pl.BlockSpec((c, w), col(C_HGG)),
            pl.BlockSpec((1, w), lambda h, ci: (0, h)),
            pl.BlockSpec((1, HG_D), lambda h, ci: (0, 0)),
            pl.BlockSpec(rm.shape, lambda h, ci: (0, 0)),
        ],
        out_specs=[
            pl.BlockSpec((c, w), lambda h, ci: (ci, h)),
            pl.BlockSpec((hp, HG_D, HG_D), lambda h, ci: (h, 0, 0)),
        ],
        out_shape=[
            jax.ShapeDtypeStruct((t, HG_HEADS * HG_D), F32),
            jax.ShapeDtypeStruct((HG_HEADS, HG_D, HG_D), F32),
        ],
        scratch_shapes=[pltpu.VMEM((hp, HG_D, HG_D), F32)],
        compiler_params=_cparams(("parallel", "arbitrary")),
        name="hgrn_prompt",
    )(main, main, main, main, lb, gnorm, rm)


def _pad16(x):
    return jnp.concatenate([x, jnp.zeros_like(x)], axis=0)


def _hgrn_sample_body(q_ref, f_ref, i_ref, g_ref, lb_ref, gn_ref, s_ref, o_ref, so_ref, *, tn):
    rows = 2 * tn
    row = lax.broadcasted_iota(jnp.int32, (rows, HG_D), 0)
    pos = row & (tn - 1)
    first = row < tn
    for h in range(HG_HEADS):
        sl = slice(h * HG_D, (h + 1) * HG_D)
        q, logf, k = _hgrn_gates(q_ref[:, sl], f_ref[:, sl], lb_ref[:, sl])
        v = i_ref[:, sl]
        b = logf
        d = 1
        while d < tn:
            b = b + jnp.where(pos >= d, pltpu.roll(b, d, 0), 0.0)
            d *= 2
        oi = jnp.sum(q * k, axis=-1, keepdims=True) * v
        for d in range(1, tn):
            ok = pos >= d
            w = jnp.exp(jnp.where(ok, b - pltpu.roll(b, d, 0), 0.0))
            a = jnp.sum(jnp.where(ok, q * pltpu.roll(k, d, 0) * w, 0.0), axis=-1, keepdims=True)
            oi = oi + a * pltpu.roll(v, d, 0)
        qe = _pad16(q * jnp.exp(b)).astype(BF16)
        blast = jnp.where(first, b[tn - 1:tn, :], b[rows - 1:rows, :])
        kd = k * jnp.exp(blast - b)
        v16 = _pad16(v).astype(BF16)
        o_inter = []
        for bi in range(2):
            s = s_ref[bi, h]
            o_inter.append(_dot(qe, s.astype(BF16))[0:rows])
            mine = first if bi == 0 else jnp.logical_not(first)
            kd16 = _pad16(jnp.where(mine, kd, 0.0)).astype(BF16)
            e_row = jnp.exp(b[(bi + 1) * tn - 1:(bi + 1) * tn, :])
            ecol = jnp.transpose(jnp.broadcast_to(e_row, (HG_D, HG_D)))
            so_ref[bi, h] = ecol * s + _dot_tn(kd16, v16)
        o = jnp.where(first, o_inter[0], o_inter[1]) + oi
        o_ref[:, sl] = _hgrn_finish(o, g_ref[:, sl], gn_ref[...])


def _hgrn_sample(main, lb, gnorm, state, *, tn):
    n = main.shape[0]
    nb = state.shape[0]
    assert n == nb * tn and tn == 4 and nb % 2 == 0
    w = HG_HEADS * HG_D
    rows = 2 * tn
    col = lambda base: (lambda i: (i, base // w))
    return pl.pallas_call(
        functools.partial(_hgrn_sample_body, tn=tn),
        grid=(nb // 2,),
        in_specs=[
            pl.BlockSpec((rows, w), col(C_HGQ)),
            pl.BlockSpec((rows, w), col(C_HGF)),
            pl.BlockSpec((rows, w), col(C_HGI)),
            pl.BlockSpec((rows, w), col(C_HGG)),
            pl.BlockSpec((1, w), lambda i: (0, 0)),
            pl.BlockSpec((1, HG_D), lambda i: (0, 0)),
            pl.BlockSpec((2, HG_HEADS, HG_D, HG_D), lambda i: (i, 0, 0, 0)),
        ],
        out_specs=[
            pl.BlockSpec((rows, w), lambda i: (i, 0)),
            pl.BlockSpec((2, HG_HEADS, HG_D, HG_D), lambda i: (i, 0, 0, 0)),
        ],
        out_shape=[
            jax.ShapeDtypeStruct((n, w), F32),
            jax.ShapeDtypeStruct(state.shape, F32),
        ],
        compiler_params=_cparams(("parallel",)),
        name="hgrn_sample",
    )(main, main, main, main, lb, gnorm, state)


def _alibi_slopes():
    return np.power(2.0, -8.0 * np.arange(1, NSA_HEADS + 1) / NSA_HEADS).astype(np.float32)


def _gelu_tanh(x):
    return 0.5 * x * (1.0 + jnp.tanh(np.float32(np.sqrt(2.0 / np.pi)) * (x + 0.044715 * (x * x * x))))


def _overlap_matrix(ncp, nlanes):
    cs = np.arange(ncp)[:, None] * CMP_STRIDE
    ss = np.arange(nlanes)[None, :] * SLC_BLOCK
    return ((cs <= ss + SLC_BLOCK - 1) & (cs + CMP_BLOCK - 1 >= ss)).astype(np.float32)


def _expand_matrix(nlanes, nkeys):
    return (np.arange(nkeys)[None, :] // SLC_BLOCK == np.arange(nlanes)[:, None]).astype(np.float32)


def _np_split3(x):
    parts = []
    r = np.asarray(x, np.float64)
    for _ in range(3):
        p = r.astype(np.float32).astype(BF16).astype(np.float64)
        parts.append(p.astype(np.float32))
        r = r - p
    return parts


def _softmax2_rows(s, mask):
    m = jnp.max(s, axis=-1, keepdims=True)
    e = jnp.where(mask, jnp.exp2(s - m), 0.0)
    l = jnp.sum(e, axis=-1, keepdims=True)
    return e * (1.0 / jnp.maximum(l, 1e-30))


def _select_blocks(score, qblk, n_top):
    blk = lax.broadcasted_iota(jnp.int32, score.shape, 1)
    forced = (blk == 0) | (blk == qblk) | (blk == qblk - 1)
    valid = blk <= qblk
    work = jnp.where(valid, jnp.where(forced, FORCE_SCORE, score), NEG)
    nrow, nblk = score.shape
    if nrow < nblk:
        work = jnp.concatenate([work, jnp.full((nblk - nrow, nblk), NEG, F32)], axis=0)
    wt = jnp.transpose(work)
    blk_t = lax.broadcasted_iota(jnp.int32, wt.shape, 0)
    sel_t = jnp.zeros(wt.shape, F32)
    for _ in range(n_top):
        m = jnp.max(wt, axis=0, keepdims=True)
        first = jnp.min(jnp.where(wt == m, blk_t, nblk), axis=0, keepdims=True)
        pick = blk_t == first
        sel_t = jnp.where(pick, 1.0, sel_t)
        wt = jnp.where(pick, NEG_PICKED, wt)
    return jnp.where(valid, jnp.transpose(sel_t)[0:nrow], 0.0)


def _compress_body(g_ref, pos_ref, w1_ref, w2_ref, o_ref):
    g = g_ref[0]
    ylo = _dot((g + pos_ref[0, 0]).astype(BF16), w1_ref[0, 0])
    yhi = _dot((g + pos_ref[0, 1]).astype(BF16), w1_ref[0, 1])
    pre = ylo + pltpu.roll(yhi, g.shape[0] - 1, 0)
    o_ref[0] = _dot(_gelu_tanh(pre).astype(BF16), w2_ref[0]).astype(BF16)


def _compress_prompt(groups, pos, w1, w2):
    na, ng, gw = groups.shape
    hid = w1.shape[-1]
    kv = lambda a: a // NSA_KV_HEADS
    return pl.pallas_call(
        _compress_body,
        grid=(na,),
        in_specs=[
            pl.BlockSpec((1, ng, gw), lambda a: (a, 0, 0)),
            pl.BlockSpec((1, 2, 1, gw), lambda a: (kv(a), 0, 0, 0)),
            pl.BlockSpec((1, 2, gw, hid), lambda a: (kv(a), 0, 0, 0)),
            pl.BlockSpec((1, hid, HEAD_DIM), lambda a: (kv(a), 0, 0)),
        ],
        out_specs=pl.BlockSpec((1, ng, HEAD_DIM), lambda a: (a, 0, 0)),
        out_shape=jax.ShapeDtypeStruct((na, ng, HEAD_DIM), BF16),
        compiler_params=_cparams(("parallel",)),
        name="compress_prompt",
    )(groups, pos, w1, w2)


SLC_CHUNK = 512
CHUNK_BLOCK_SHIFT = 3
N_SLOPE_COLS = 6
STRIP = Q_BLOCK


def _nsa_prompt_body(q_ref, gate_ref, sl_ref, kca_ref, vc_ref, ka_ref, vsa_ref, kwa_ref, vwa_ref, ov_ref,
                     o_ref, s_ref, p_ref, bias_ref, psum_ref, sw_ref, pw_ref, bw_ref, s2_ref, mx_ref, bias2_ref,
                     m_ref, alpha_ref, acc_ref, todo_ref, *, n, n_top):
    g_n = NSA_GROUP
    qb = Q_BLOCK
    nrow = g_n * qb
    nstrip = nrow // STRIP
    i = pl.program_id(0)
    start = i * qb
    q = (q_ref[...] * Q_SCALE).astype(BF16)
    qs = jnp.concatenate([q[:, HEAD_DIM * g:HEAD_DIM * (g + 1)] for g in range(g_n)], axis=0)
    qc = jnp.concatenate([qs, sl_ref[...]], axis=1)

    def strip_rows(k):
        r0 = k * STRIP
        t0 = r0 % qb
        return slice(r0, r0 + STRIP), slice(t0, t0 + STRIP)

    def for_strips(fn):
        for k in range(nstrip):
            fn(k, None)

    def softmax_strip(s):
        m = jnp.maximum(jnp.max(s, axis=-1, keepdims=True), M_INIT)
        e = jnp.exp2(s - m)
        return e * (1.0 / jnp.maximum(jnp.sum(e, axis=-1, keepdims=True), 1e-30))

    ncp = kca_ref.shape[1]
    t_c = lax.broadcasted_iota(jnp.int32, (qb, ncp), 0)
    c_c = lax.broadcasted_iota(jnp.int32, (qb, ncp), 1)
    bias_ref[:, 0:ncp] = jnp.where(t_c - CMP_STRIDE * c_c + (start - (CMP_BLOCK - 1)) >= 0, 0.0, NEG)
    s_ref[:, 0:ncp] = _dot(qc, kca_ref[...])
    psum_ref[...] = jnp.zeros(psum_ref.shape, F32)

    def cmp_strip(k, carry):
        rows, trows = strip_rows(k)
        p = softmax_strip(s_ref[rows, 0:ncp] + bias_ref[trows, 0:ncp])
        psum_ref[trows, :] += p
        p_ref[rows, 0:ncp] = p.astype(BF16)
        return carry

    for_strips(cmp_strip)
    o_cmp = _dot(p_ref[:, 0:ncp], vc_ref[...])

    nw = WINDOW + qb
    t_w = lax.broadcasted_iota(jnp.int32, (qb, nw), 0)
    w_w = lax.broadcasted_iota(jnp.int32, (qb, nw), 1)
    dist_w = t_w + WINDOW - w_w
    mask_w = (dist_w >= 0) & (dist_w < WINDOW) & (w_w >= WINDOW - start)
    bw_ref[...] = jnp.where(mask_w, 0.0, NEG)
    w0 = pl.multiple_of(start, qb)
    sw_ref[...] = _dot(qc, kwa_ref[:, pl.ds(w0, nw)])

    def win_strip(k, carry):
        rows, trows = strip_rows(k)
        s = sw_ref[rows, :] + bw_ref[trows, :]
        m = jnp.max(s, axis=-1, keepdims=True)
        pw_ref[rows, :] = jnp.exp2(s - m).astype(BF16)
        return carry

    for_strips(win_strip)
    win = _dot_nt(pw_ref[...], vwa_ref[:, pl.ds(w0, nw)])

    ph, plo = _split2(psum_ref[...])
    ov = ov_ref[...]
    score = _dot(ph, ov) + _dot(plo, ov)
    t_b = lax.broadcasted_iota(jnp.int32, score.shape, 0)
    qblk = jnp.right_shift(start + t_b, SLC_SHIFT)
    sel = _select_blocks(score, qblk, n_top)

    ck = SLC_CHUNK
    notsel = (1.0 - sel).astype(BF16)
    qa = jnp.concatenate([qc, jnp.concatenate([notsel] * g_n, axis=0)], axis=1)
    m_ref[...] = jnp.full(m_ref.shape, M_INIT, F32)
    acc_ref[...] = jnp.zeros(acc_ref.shape, F32)
    c_last = lax.div(start, ck)
    t_s = lax.broadcasted_iota(jnp.int32, (qb, ck), 0)
    s_s = lax.broadcasted_iota(jnp.int32, (qb, ck), 1)
    bias2_ref[0] = jnp.zeros((qb, ck), F32)
    bias2_ref[1] = jnp.where(t_s - s_s + (start - c_last * ck) >= 0, 0.0, NEG)

    blk_any = jnp.max(sel, axis=0, keepdims=True)
    lane = lax.broadcasted_iota(jnp.int32, blk_any.shape, 1)
    cnt = jnp.int32(0)
    for c in range(todo_ref.shape[0] - 2):
        used = jnp.max(jnp.where(jnp.right_shift(lane, CHUNK_BLOCK_SHIFT) == c, blk_any, 0.0)) > 0.5
        todo_ref[cnt] = c
        cnt = cnt + jnp.logical_and(used, c < c_last).astype(jnp.int32)
    todo_ref[cnt] = c_last
    todo_ref[cnt + 1] = c_last

    def scores(c, slot):
        k0 = pl.multiple_of(c * ck, ck)
        last = (c == c_last).astype(jnp.int32)
        s = _dot(qa, ka_ref[:, pl.ds(k0, ck)])
        for k in range(nstrip):
            rows, _ = strip_rows(k)
            sk = s[rows] + bias2_ref[last]
            s2_ref[slot, rows, :] = sk
            mx_ref[slot, rows, :] = jnp.broadcast_to(jnp.max(sk, axis=-1, keepdims=True), (qb, LANE))

    def accumulate(c, slot):
        k0 = pl.multiple_of(c * ck, ck)
        for k in range(nstrip):
            rows, _ = strip_rows(k)
            m_old = m_ref[rows, :]
            m_new = jnp.maximum(m_old, mx_ref[slot, rows, :])
            alpha_ref[rows, :] = jnp.exp2(m_old - m_new)
            m_ref[rows, :] = m_new
            p_ref[rows, 0:ck] = jnp.exp2(s2_ref[slot, rows, :] - jnp.tile(m_new, (1, ck // LANE))).astype(BF16)
        acc_ref[...] = alpha_ref[...] * acc_ref[...] + _dot_nt(p_ref[:, 0:ck], vsa_ref[:, pl.ds(k0, ck)])

    scores(todo_ref[0], 0)
    ntrip = cnt + 1

    def visit_pair(i, carry):
        j = 2 * i
        scores(todo_ref[j + 1], 1)
        accumulate(todo_ref[j], 0)
        scores(todo_ref[j + 2], 0)
        accumulate(todo_ref[j + 1], 1)
        return carry

    lax.fori_loop(0, ntrip // 2, visit_pair, 0)

    @pl.when(ntrip % 2 == 1)
    def _():
        accumulate(todo_ref[ntrip - 1], 0)

    gate = jax.nn.sigmoid(gate_ref[...])
    hd = HEAD_DIM

    def normalised(x):
        return x[:, 0:hd] * (1.0 / jnp.maximum(x[:, hd:hd + 1], 1e-30))

    outs = []
    for g in range(g_n):
        c0 = 3 * (n * g_n + g)
        rows = slice(g * qb, (g + 1) * qb)
        outs.append(gate[:, c0:c0 + 1] * o_cmp[rows] + gate[:, c0 + 1:c0 + 2] * normalised(acc_ref[rows, :])
                    + gate[:, c0 + 2:c0 + 3] * normalised(win[rows]))
    o_ref[...] = jnp.concatenate(outs, axis=-1)


def _nsa_prompt(main, kvtb, kwtp, kct, vc, n):
    t = main.shape[0]
    ncp = kct.shape[1]
    ns = t // SLC_BLOCK
    assert ns <= LANE and t % SLC_CHUNK == 0
    g_n, hd = NSA_GROUP, HEAD_DIM
    nrow = g_n * Q_BLOCK
    nw = WINDOW + Q_BLOCK
    ov = jnp.asarray(_overlap_matrix(ncp, LANE), BF16)
    sl = np.zeros((g_n, Q_BLOCK, hd), np.float32)
    parts = _np_split3(_alibi_slopes()[n * g_n:(n + 1) * g_n].astype(np.float64) * LOG2E)
    for j in range(N_SLOPE_COLS):
        sl[:, :, j] = parts[j % 3][:, None]
    sl = jnp.asarray(sl.reshape(nrow, hd), BF16)

    def with_pos_rows(kt, pos):
        rows = np.zeros((hd, pos.shape[0]), np.float32)
        rows[0:3] = (pos // SLC_BLOCK) * SLC_BLOCK
        rows[3:6] = pos % SLC_BLOCK
        return jnp.concatenate([kt, jnp.asarray(rows, BF16)], axis=0)

    ka = jnp.concatenate([with_pos_rows(kvtb[R_KSLC + n * hd:R_KSLC + (n + 1) * hd], np.arange(t)),
                          jnp.asarray(NEG * _expand_matrix(LANE, t), BF16)], axis=0)
    kca = with_pos_rows(kct, np.arange(ncp) * CMP_STRIDE + (CMP_BLOCK - 1))
    kwa = with_pos_rows(kwtp[n * hd:(n + 1) * hd], np.arange(kwtp.shape[1]))

    def with_ones_row(vt):
        rows = np.zeros((hd, vt.shape[1]), np.float32)
        rows[0] = 1.0
        return jnp.concatenate([vt, jnp.asarray(rows, BF16)], axis=0)

    vsa = with_ones_row(kvtb[R_VSLC + n * hd:R_VSLC + (n + 1) * hd])
    vwa = with_ones_row(kwtp[KVW + n * hd:KVW + (n + 1) * hd])
    wq = g_n * hd
    whole = lambda a: pl.BlockSpec(a.shape, lambda i: (0,) * a.ndim)
    return pl.pallas_call(
        functools.partial(_nsa_prompt_body, n=n, n_top=min(SLC_TOP, ns)),
        grid=(t // Q_BLOCK,),
        in_specs=[
            pl.BlockSpec((Q_BLOCK, wq), lambda i: (i, C_Q // wq + n)),
            pl.BlockSpec((Q_BLOCK, LANE), lambda i: (i, C_NGATE // LANE)),
            whole(sl), whole(kca), whole(vc), whole(ka), whole(vsa), whole(kwa), whole(vwa), whole(ov),
        ],
        out_specs=pl.BlockSpec((Q_BLOCK, wq), lambda i: (i, 0)),
        out_shape=jax.ShapeDtypeStruct((t, wq), F32),
        scratch_shapes=[
            pltpu.VMEM((nrow, ncp), F32),
            pltpu.VMEM((nrow, max(ncp, SLC_CHUNK)), BF16),
            pltpu.VMEM((Q_BLOCK, ncp), F32),
            pltpu.VMEM((Q_BLOCK, ncp), F32),
            pltpu.VMEM((nrow, nw), F32),
            pltpu.VMEM((nrow, nw), BF16),
            pltpu.VMEM((Q_BLOCK, nw), F32),
            pltpu.VMEM((2, nrow, SLC_CHUNK), F32),
            pltpu.VMEM((2, nrow, LANE), F32),
            pltpu.VMEM((2, Q_BLOCK, SLC_CHUNK), F32),
            pltpu.VMEM((nrow, LANE), F32),
            pltpu.VMEM((nrow, LANE), F32),
            pltpu.VMEM((nrow, 2 * hd), F32),
            pltpu.SMEM((t // SLC_CHUNK + 2,), jnp.int32),
        ],
        compiler_params=_cparams(("parallel",)),
        name=f"nsa_prompt_{n}",
    )(main, main, sl, kca, vc, ka, vsa, kwa, vwa, ov)


N_KINDS = 4
SAMPLE_PER_STEP = 2


def _nsa_sample_body(pt_ref, cache_ref, q_ref, new_ref, win_ref, gate_ref, slope_ref,
                     posk_ref, w1k_ref, w2k_ref, posv_ref, w1v_ref, w2v_ref, ov_ref, ex_ref, same_ref,
                     o_ref, wout_ref, buf_ref, rows_ref, newp_ref, sem, *, tn, n_top):
    b = pl.program_id(0)
    nsteps = pl.num_programs(0)
    nbs, npages = buf_ref.shape[1], buf_ref.shape[2]
    slot = b % 2

    def page_copy(step, sl, e, j):
        return pltpu.make_async_copy(cache_ref.at[pt_ref[step * nbs + e, j]], buf_ref.at[sl, e, j], sem.at[sl])

    def all_pages(step, sl, fn):
        for e in range(nbs):
            for j in range(npages):
                fn(page_copy(step, sl, e, j))

    @pl.when(b == 0)
    def _():
        all_pages(0, 0, lambda c: c.start())
        newp_ref[...] = jnp.zeros(newp_ref.shape, F32)

    @pl.when(b + 1 < nsteps)
    def _():
        all_pages(b + 1, 1 - slot, lambda c: c.start())

    all_pages(b, slot, lambda c: c.wait())
    for e in range(nbs):
        one = pl.ds(e, 1)
        _nsa_sample_one(buf_ref.at[slot, e], q_ref.at[one], new_ref.at[one], win_ref.at[one], gate_ref.at[one],
                        slope_ref, posk_ref, w1k_ref, w2k_ref, posv_ref, w1v_ref, w2v_ref, ov_ref, ex_ref,
                        same_ref, o_ref.at[one], wout_ref.at[one], rows_ref.at[e], newp_ref.at[e],
                        tn=tn, n_top=n_top)


def _nsa_sample_one(pg, q_ref, new_ref, win_ref, gate_ref, slope_ref,
                    posk_ref, w1k_ref, w2k_ref, posv_ref, w1v_ref, w2v_ref, ov_ref, ex_ref, same_ref,
                    o_ref, wout_ref, rows_ref, newp_ref, *, tn, n_top):
    npages, page = pg.shape[0], pg.shape[3]
    past_len = npages * page

    for a in range(4):
        newp_ref[a, :, 0:tn] = new_ref[0, (2 + a) * KVW:(3 + a) * KVW, :]

    ncp = past_len // CMP_STRIDE
    for a in range(2):
        for j in range(npages):
            rows_ref[a, j * page:(j + 1) * page, :] = jnp.transpose(pg[j, a])
        rows_ref[a, past_len:past_len + LANE, :] = jnp.zeros((LANE, KVW), F32)

    def compress(a, pos_ref, w1_ref, w2_ref):
        lo, hi = [], []
        for r in range(CMP_STRIDE):
            x = rows_ref[a, pl.ds(r, ncp + 8, stride=CMP_STRIDE), :]
            lo.append((x[0:ncp] + pos_ref[r]).astype(BF16))
            hi.append((x[1:ncp + 1] + pos_ref[CMP_STRIDE + r]).astype(BF16))
        x = jnp.concatenate(lo + hi, axis=1)
        return _dot(_gelu_tanh(_dot(x, w1_ref[...])).astype(BF16), w2_ref[...]).astype(BF16)

    kc = compress(0, posk_ref, w1k_ref, w2k_ref)
    vc = compress(1, posv_ref, w1v_ref, w2v_ref)

    rows = q_ref.shape[1]
    q = q_ref[0]
    slope = slope_ref[:, 0:1]
    r_i = lax.broadcasted_iota(jnp.int32, (rows, 1), 0)
    t_i = r_i & (tn - 1)
    qpos = past_len + t_i
    low = r_i < rows // 2

    def pick_head(o):
        return jnp.where(low, o[:, 0:HEAD_DIM], o[:, HEAD_DIM:KVW])

    cpos = CMP_STRIDE * lax.broadcasted_iota(jnp.int32, (rows, ncp), 1) + (CMP_BLOCK - 1)
    dist_c = (qpos - cpos).astype(F32)
    mask_c = dist_c >= 0.0
    s = jnp.where(mask_c, _dot_nt(q, kc) - slope * dist_c, NEG)
    p_c = _softmax2_rows(s, mask_c)
    o_cmp = pick_head(_dot(p_c.astype(BF16), vc))

    same = same_ref[...]
    ph, plo = _split2(p_c)
    psum = _dot(same, ph) + _dot(same, plo)
    ph, plo = _split2(psum)
    ov = ov_ref[...]
    score = _dot(ph, ov) + _dot(plo, ov)
    sel = _select_blocks(score, jnp.right_shift(qpos, SLC_SHIFT), n_top)

    u_i = lax.broadcasted_iota(jnp.int32, (rows, LANE), 1)
    new_ok = u_i <= t_i
    new_blk = past_len // SLC_BLOCK
    keep = _dot(sel.astype(BF16), ex_ref[...]) > 0.5
    spos = lax.broadcasted_iota(jnp.int32, (rows, past_len), 1)
    ks = jnp.concatenate([pg[j, 2].astype(BF16) for j in range(npages)], axis=1)
    vs = jnp.concatenate([pg[j, 3].astype(BF16) for j in range(npages)], axis=1)
    s_past = _dot(q, ks)
    s_past = jnp.where(keep, s_past - slope * (qpos - spos).astype(F32), NEG)
    keep_new = new_ok & (sel[:, new_blk:new_blk + 1] > 0.5)
    s_new = jnp.where(keep_new, _dot(q, newp_ref[0].astype(BF16)) - slope * (t_i - u_i).astype(F32), NEG)
    m = jnp.maximum(jnp.max(s_past, axis=-1, keepdims=True), jnp.max(s_new, axis=-1, keepdims=True))
    e_past = jnp.where(keep, jnp.exp2(s_past - m), 0.0)
    e_new = jnp.where(keep_new, jnp.exp2(s_new - m), 0.0)
    l = jnp.sum(e_past, axis=-1, keepdims=True) + jnp.sum(e_new, axis=-1, keepdims=True)
    o_sel = _dot_nt(e_new.astype(BF16), newp_ref[1].astype(BF16)) + _dot_nt(e_past.astype(BF16), vs)
    o_sel = pick_head(o_sel * (1.0 / jnp.maximum(l, 1e-30)))

    wl = win_ref.shape[3]
    w_i = lax.broadcasted_iota(jnp.int32, (rows, wl), 1)
    dist_wi = (wl + t_i) - w_i
    mask_w = (dist_wi < WINDOW) & (w_i >= wl - past_len)
    s_win = jnp.where(mask_w, _dot(q, win_ref[0, 0].astype(BF16)) - slope * dist_wi.astype(F32), NEG)
    s_new = jnp.where(new_ok, _dot(q, newp_ref[2].astype(BF16)) - slope * (t_i - u_i).astype(F32), NEG)
    m = jnp.maximum(jnp.max(s_win, axis=-1, keepdims=True), jnp.max(s_new, axis=-1, keepdims=True))
    e_win = jnp.where(mask_w, jnp.exp2(s_win - m), 0.0)
    e_new = jnp.where(new_ok, jnp.exp2(s_new - m), 0.0)
    l = jnp.sum(e_win, axis=-1, keepdims=True) + jnp.sum(e_new, axis=-1, keepdims=True)
    o_win = (_dot_nt(e_win.astype(BF16), win_ref[0, 1].astype(BF16))
             + _dot_nt(e_new.astype(BF16), newp_ref[3].astype(BF16)))
    o_win = pick_head(o_win * (1.0 / jnp.maximum(l, 1e-30)))

    gate = jax.nn.sigmoid(gate_ref[0])
    o_ref[0] = gate[:, 0:1] * o_cmp + gate[:, 1:2] * o_sel + gate[:, 2:3] * o_win

    lane_w = lax.broadcasted_iota(jnp.int32, (KVW, LANE), 1)
    for a in range(2):
        shifted = pltpu.roll(win_ref[0, a], wl - tn, 1)
        tail = pltpu.roll(newp_ref[2 + a], LANE - tn, 1)
        wout_ref[0, a, :, 0:wl - LANE] = shifted[:, 0:wl - LANE]
        wout_ref[0, a, :, wl - LANE:wl] = jnp.where(lane_w >= LANE - tn, tail, shifted[:, wl - LANE:wl])


def _nsa_sample(page_table, cache, qrows, newt, win, gates, cmpk, cmpv, *, tn):
    nb, npages = page_table.shape
    page = cache.shape[3]
    past_len = npages * page
    wl = win.shape[3]
    rows = qrows.shape[1]
    assert tn == 4 and page == LANE and wl % LANE == 0 and wl > LANE and past_len >= wl
    ncp = past_len // CMP_STRIDE
    ns = -(-(past_len + tn) // SLC_BLOCK)
    assert ncp <= LANE and ns <= LANE
    r = np.arange(rows)
    h = r // tn
    slope = jnp.asarray(np.broadcast_to((_alibi_slopes()[h].astype(np.float64) * LOG2E)[:, None], (rows, LANE)), F32)
    nt = h // NSA_GROUP * tn + r % tn
    same = jnp.asarray((nt[:, None] == nt[None, :]).astype(np.float32), BF16)
    ov = jnp.asarray(_overlap_matrix(ncp, LANE), BF16)
    ex = jnp.asarray(_expand_matrix(LANE, past_len), BF16)
    posk, w1k, w2k = cmpk
    posv, w1v, w2v = cmpv
    nbs = SAMPLE_PER_STEP
    assert nb % nbs == 0
    const = lambda a: pl.BlockSpec(a.shape, lambda b, pt: (0,) * a.ndim)
    per_b = lambda a: pl.BlockSpec((nbs,) + a.shape[1:], lambda b, pt: (b,) + (0,) * (a.ndim - 1))
    grid_spec = pltpu.PrefetchScalarGridSpec(
        num_scalar_prefetch=1,
        grid=(nb // nbs,),
        in_specs=[
            pl.BlockSpec(memory_space=pl.ANY),
            per_b(qrows), per_b(newt), per_b(win), per_b(gates), const(slope),
            const(posk), const(w1k), const(w2k), const(posv), const(w1v), const(w2v),
            const(ov), const(ex), const(same),
        ],
        out_specs=[pl.BlockSpec((nbs, rows, HEAD_DIM), lambda b, pt: (b, 0, 0)), per_b(win)],
        scratch_shapes=[
            pltpu.VMEM((2, nbs, npages, N_KINDS, KVW, page), F32),
            pltpu.VMEM((nbs, 2, past_len + LANE, KVW), F32),
            pltpu.VMEM((nbs, 4, KVW, LANE), F32),
            pltpu.SemaphoreType.DMA((2,)),
        ],
    )
    return pl.pallas_call(
        functools.partial(_nsa_sample_body, tn=tn, n_top=min(SLC_TOP, ns)),
        grid_spec=grid_spec,
        out_shape=[jax.ShapeDtypeStruct((nb, rows, HEAD_DIM), F32), jax.ShapeDtypeStruct(win.shape, F32)],
        compiler_params=_cparams(("arbitrary",)),
        name="nsa_sample",
    )(page_table, cache, qrows, newt, win, gates, slope, posk, w1k, w2k, posv, w1v, w2v, ov, ex, same)


def _prep_ffn(wg, wu, wd, tf=512):
    wg, wu, wd = wg.astype(BF16), wu.astype(BF16), wd.astype(BF16)
    main = wg.shape[1] // tf * tf
    return wg, wu, wd, wg[:, main:], wu[:, main:], wd[main:]


def _prep_w_in(w_in):
    wt = w_in.T
    d = wt.shape[1]
    rows = [wt[:W_NSAQ], wt[W_GA:], wt[W_KV:W_KV + 2 * KVW], wt[W_NGATE:W_GA]]
    used = sum(r.shape[0] for r in rows)
    w_main = jnp.concatenate(rows + [jnp.zeros((C_Q - used, d), wt.dtype), wt[W_NSAQ:W_KV]], axis=0).astype(BF16)
    return w_main, wt[W_KV:W_NGATE].astype(BF16)


def _prep_cmp_prompt(pos_k, w1_k, w2_k, pos_v, w1_v, w2_v):
    half = CMP_STRIDE * HEAD_DIM
    pos = jnp.stack([pos_k.reshape(2, 1, half), pos_v.reshape(2, 1, half)])
    w1 = jnp.stack([w1_k.reshape(2, half, -1), w1_v.reshape(2, half, -1)]).astype(BF16)
    w2 = jnp.stack([w2_k, w2_v]).astype(BF16)
    return pos, w1, w2


def _prep_cmp_sample(pos, w1, w2):
    hid = w1.shape[1]
    w1p = w1.reshape(CMP_BLOCK, HEAD_DIM, hid)
    z1 = jnp.zeros_like(w1p)
    w1b = jnp.concatenate([jnp.concatenate([w1p, z1], axis=2), jnp.concatenate([z1, w1p], axis=2)], axis=1)
    z2 = jnp.zeros_like(w2)
    w2b = jnp.concatenate([jnp.concatenate([w2, z2], axis=1), jnp.concatenate([z2, w2], axis=1)], axis=0)
    posb = jnp.concatenate([pos, pos], axis=1)[:, None, :]
    return posb, w1b.reshape(CMP_BLOCK * KVW, 2 * hid).astype(BF16), w2b.astype(BF16)


def _project(x, g, w_main, w_kvt):
    main = _inproj(x, g, w_main, tm=1024)
    kvt, kvtb = _inproj_t(x, g, w_kvt)
    return main, kvt, kvtb


def _mix_prompt(main, kvtb, lb, gnorm, cmp_w):
    t = main.shape[0]
    o_hg, s_fin = _hgrn_prompt(main, lb, gnorm)
    nkv, hd = NSA_KV_HEADS, HEAD_DIM
    groups = main[:, C_CMP:C_CMP + 2 * KVW].reshape(t // CMP_STRIDE, CMP_STRIDE, 2, nkv, hd)
    groups = groups.transpose(2, 3, 0, 1, 4).reshape(2 * nkv, t // CMP_STRIDE, CMP_STRIDE * hd)
    cmp = _compress_prompt(groups, *cmp_w)
    kct = cmp[:nkv].transpose(0, 2, 1)
    vc = cmp[nkv:]
    kwtp = jnp.pad(kvtb[R_KWIN:N_KV_ROWS], ((0, 0), (WINDOW, 0)))
    o_nsa = [_nsa_prompt(main, kvtb, kwtp, kct[n], vc[n], n) for n in range(nkv)]
    return o_hg, o_nsa, s_fin


def _mix_sample(main, kvt, lb, gnorm, state, cache, win, page_table, cmp_w, *, tn):
    nb = state.shape[0]
    o_hg, s_new = _hgrn_sample(main, lb, gnorm, state, tn=tn)
    g_n, nkv, hd = NSA_GROUP, NSA_KV_HEADS, HEAD_DIM
    q = (main[:, C_Q:C_Q + NSA_HEADS * hd] * Q_SCALE).astype(BF16)
    q = q.reshape(nb, tn, nkv, g_n, hd).transpose(0, 2, 3, 1, 4)
    eye = jnp.eye(nkv, dtype=BF16)
    qrows = (q[:, :, :, :, None, :] * eye[None, :, None, None, :, None]).reshape(nb, nkv * g_n * tn, nkv * hd)
    gates = main[:, C_NGATE:C_NGATE + N_NGATE].reshape(nb, tn, NSA_HEADS, 3).transpose(0, 2, 1, 3)
    gates = jnp.pad(gates.reshape(nb, NSA_HEADS * tn, 3), ((0, 0), (0, 0), (0, LANE - 3)))
    newt = kvt.reshape(N_KV_ROWS, nb, tn).transpose(1, 0, 2)
    o, wout = _nsa_sample(page_table, cache, qrows, newt, win, gates, *cmp_w, tn=tn)
    o = o.reshape(nb, nkv, g_n, tn, hd).transpose(0, 3, 1, 2, 4).reshape(nb * tn, nkv, g_n * hd)
    return o_hg, [o[:, n] for n in range(nkv)], s_new, wout


def kernel(x_prompt, x_sample, cache_kv, cache_win, state_hgrn, page_table, norm_pre1, norm_post1, ff1_gate, ff1_up, ff1_down, norm_pre2, norm_post2, w_in, hg_lb, hg_gnorm, cmp_pos_k, cmp_w1_k, cmp_w2_k, cmp_pos_v, cmp_w1_v, cmp_w2_v, w_proj_hg, w_proj_nsa, w_out, norm_pre3, norm_post3, ff2_gate, ff2_up, ff2_down):
    depth = norm_pre1.shape[0]
    bp, tp, d = x_prompt.shape
    nb, tn, _ = x_sample.shape
    assert bp == 1
    nkv, hd = NSA_KV_HEADS, HEAD_DIM
    lb_all = jnp.cumsum(jax.nn.softmax(hg_lb.astype(F32), axis=0), axis=0)[:depth]
    xp = x_prompt.reshape(tp, d)
    xs = x_sample.reshape(nb * tn, d)
    kv_p, kv_s, win_p, win_s, st_p, st_s = [], [], [], [], [], []
    row = lambda a: a.reshape(1, -1)
    for l in range(depth):
        ff1 = _prep_ffn(ff1_gate[l], ff1_up[l], ff1_down[l])
        ff2 = _prep_ffn(ff2_gate[l], ff2_up[l], ff2_down[l])
        w_in_l = _prep_w_in(w_in[l])
        wh, wn, wo = w_proj_hg[l].astype(BF16), w_proj_nsa[l].astype(BF16), w_out[l].astype(BF16)
        cmp_p = _prep_cmp_prompt(cmp_pos_k[l], cmp_w1_k[l], cmp_w2_k[l], cmp_pos_v[l], cmp_w1_v[l], cmp_w2_v[l])
        cmp_s = (_prep_cmp_sample(cmp_pos_k[l], cmp_w1_k[l], cmp_w2_k[l]),
                 _prep_cmp_sample(cmp_pos_v[l], cmp_w1_v[l], cmp_w2_v[l]))
        lb, gnorm = row(lb_all[l]), row(hg_gnorm[l])
        n_pool, page = cache_kv.shape[1], cache_kv.shape[2]
        cache = cache_kv[l].transpose(0, 2, 3, 4, 1).reshape(n_pool, N_KINDS, KVW, page)
        wl = cache_win.shape[2]
        win = cache_win[l].transpose(0, 2, 3, 4, 1).reshape(nb, 2, KVW, wl)

        xp = _ffn(xp, row(norm_pre1[l]), row(norm_post1[l]), *ff1)
        xs = _ffn(xs, row(norm_pre1[l]), row(norm_post1[l]), *ff1)

        main_p, kvt_p, kvtb_p = _project(xp, row(norm_pre2[l]), *w_in_l)
        main_s, kvt_s, _ = _project(xs, row(norm_pre2[l]), *w_in_l)
        ohg_p, onsa_p, s_p = _mix_prompt(main_p, kvtb_p, lb, gnorm, cmp_p)
        ohg_s, onsa_s, s_s, wout = _mix_sample(main_s, kvt_s, lb, gnorm, state_hgrn[l], cache, win,
                                               page_table, cmp_s, tn=tn)
        xp = _outproj(_merge(ohg_p, *onsa_p, main_p, wh, wn), xp, wo, row(norm_post2[l]))
        xs = _outproj(_merge(ohg_s, *onsa_s, main_s, wh, wn), xs, wo, row(norm_post2[l]))

        kv_p.append(kvt_p[:N_KINDS * KVW].reshape(N_KINDS, nkv, hd, bp, tp).transpose(3, 4, 0, 1, 2))
        kv_s.append(kvt_s[:N_KINDS * KVW].reshape(N_KINDS, nkv, hd, nb, tn).transpose(3, 4, 0, 1, 2))
        wp = min(WINDOW, tp)
        win_p.append(kvt_p[R_KWIN:, tp - wp:].reshape(2, nkv, hd, bp, wp).transpose(3, 4, 0, 1, 2))
        win_s.append(wout.reshape(nb, 2, nkv, hd, wl).transpose(0, 4, 1, 2, 3))
        st_p.append(s_p[None])
        st_s.append(s_s)

        xp = _ffn(xp, row(norm_pre3[l]), row(norm_post3[l]), *ff2)
        xs = _ffn(xs, row(norm_pre3[l]), row(norm_post3[l]), *ff2)
    return (xp.reshape(bp, tp, d), xs.reshape(nb, tn, d), jnp.stack(kv_p), jnp.stack(kv_s),
            jnp.stack(win_p), jnp.stack(win_s), jnp.stack(st_p), jnp.stack(st_s))
```

```python
import functools
import itertools

import numpy as np
import jax
import jax.numpy as jnp
from jax import lax
from jax.experimental import pallas as pl
from jax.experimental.pallas import tpu as pltpu

F32 = jnp.float32
BF16 = jnp.bfloat16

EPS = 1e-6
HG_HEADS = 8
HG_D = 128
NSA_HEADS = 16
NSA_KV_HEADS = 2
NSA_GROUP = NSA_HEADS // NSA_KV_HEADS
HEAD_DIM = 64
CMP_BLOCK = 32
CMP_STRIDE = 16
SLC_BLOCK = 64
SLC_SHIFT = 6
SLC_TOP = 16
WINDOW = 512
Q_BLOCK = 128
FORCE_SCORE = 1.0e4
NEG = -1.0e30
NEG_PICKED = -3.0e38
M_INIT = -1.0e20
LOG2E = float(np.log2(np.e))

LANE = 128
VMEM_LIMIT = 56 * 1024 * 1024

W_NSAQ, W_KV, W_NGATE, W_GA = 4096, 5120, 5888, 5936
N_NGATE = 3 * NSA_HEADS
KVW = NSA_KV_HEADS * HEAD_DIM
C_HGQ, C_HGF, C_HGI, C_HGG = 0, 1024, 2048, 3072
C_GA, C_GB = 4096, 6144
C_CMP = 8192
C_NGATE = 8448
C_Q = 8704
D_MAIN = 9728
Q_SCALE = HEAD_DIM ** -0.5 * LOG2E
R_KSLC, R_VSLC, R_KWIN, R_VWIN = 2 * KVW, 3 * KVW, 4 * KVW, 5 * KVW
N_KV_ROWS = 6 * KVW


def _cparams(sem):
    return pltpu.CompilerParams(dimension_semantics=sem, vmem_limit_bytes=VMEM_LIMIT)


def _rms(x, g):
    return x * lax.rsqrt(jnp.mean(x * x, axis=-1, keepdims=True) + EPS) * g


def _silu(x):
    return x * jax.nn.sigmoid(x)


def _dot(a, b):
    return jnp.dot(a, b, preferred_element_type=F32)


def _dot_nt(a, b):
    return lax.dot_general(a, b, (((1,), (1,)), ((), ())), preferred_element_type=F32)


def _dot_tn(a, b):
    return lax.dot_general(a, b, (((0,), (0,)), ((), ())), preferred_element_type=F32)


def _split3(x):
    hi = x.astype(BF16)
    r = x - hi.astype(F32)
    mid = r.astype(BF16)
    lo = (r - mid.astype(F32)).astype(BF16)
    return hi, mid, lo


def _split2(x):
    hi = x.astype(BF16)
    return hi, (x - hi.astype(F32)).astype(BF16)


def _ffn_body(x_ref, npre_ref, npost_ref, wg_ref, wu_ref, wd_ref, wgr_ref, wur_ref, wdr_ref, o_ref, h_ref, acc_ref):
    f = pl.program_id(1)

    @pl.when(f == 0)
    def _():
        h_ref[...] = _rms(x_ref[...], npre_ref[...]).astype(BF16)
        acc_ref[...] = jnp.zeros_like(acc_ref)

    def partial_down(wg, wu, wd):
        h = h_ref[...]
        a = _silu(_dot(h, wg[...])) * _dot(h, wu[...])
        return _dot(a.astype(BF16), wd[...])

    acc_ref[...] += partial_down(wg_ref, wu_ref, wd_ref)

    @pl.when(f == pl.num_programs(1) - 1)
    def _():
        y = acc_ref[...] + partial_down(wgr_ref, wur_ref, wdr_ref)
        o_ref[...] = x_ref[...] + 0.5 * _rms(y, npost_ref[...])


def _ffn(x, npre, npost, wg, wu, wd, wgr, wur, wdr, *, tm=512, tf=512):
    n, d = x.shape
    dff = wg.shape[1]
    rem = wgr.shape[1]
    assert rem > 0 and (dff - rem) % tf == 0 and rem % LANE == 0
    tm = min(tm, n)
    return pl.pallas_call(
        _ffn_body,
        grid=(n // tm, (dff - rem) // tf),
        in_specs=[
            pl.BlockSpec((tm, d), lambda i, f: (i, 0)),
            pl.BlockSpec((1, d), lambda i, f: (0, 0)),
            pl.BlockSpec((1, d), lambda i, f: (0, 0)),
            pl.BlockSpec((d, tf), lambda i, f: (0, f)),
            pl.BlockSpec((d, tf), lambda i, f: (0, f)),
            pl.BlockSpec((tf, d), lambda i, f: (f, 0)),
            pl.BlockSpec((d, rem), lambda i, f: (0, 0)),
            pl.BlockSpec((d, rem), lambda i, f: (0, 0)),
            pl.BlockSpec((rem, d), lambda i, f: (0, 0)),
        ],
        out_specs=pl.BlockSpec((tm, d), lambda i, f: (i, 0)),
        out_shape=jax.ShapeDtypeStruct((n, d), F32),
        scratch_shapes=[pltpu.VMEM((tm, d), BF16), pltpu.VMEM((tm, d), F32)],
        compiler_params=_cparams(("parallel", "arbitrary")),
        name="ffn",
    )(x, npre, npost, wg, wu, wd, wgr, wur, wdr)


def _inproj_body(x_ref, g_ref, wt_ref, o_ref, h_ref):
    @pl.when(pl.program_id(1) == 0)
    def _():
        h_ref[...] = _rms(x_ref[...], g_ref[...]).astype(BF16)

    o_ref[...] = _dot_nt(h_ref[...], wt_ref[...])


def _inproj(x, g, wt, *, tm=512, tn=512):
    n, d = x.shape
    dout = wt.shape[0]
    tm = min(tm, n)
    return pl.pallas_call(
        _inproj_body,
        grid=(n // tm, dout // tn),
        in_specs=[
            pl.BlockSpec((tm, d), lambda i, j: (i, 0)),
            pl.BlockSpec((1, d), lambda i, j: (0, 0)),
            pl.BlockSpec((tn, d), lambda i, j: (j, 0)),
        ],
        out_specs=pl.BlockSpec((tm, tn), lambda i, j: (i, j)),
        out_shape=jax.ShapeDtypeStruct((n, dout), F32),
        scratch_shapes=[pltpu.VMEM((tm, d), BF16)],
        compiler_params=_cparams(("parallel", "arbitrary")),
        name="inproj",
    )(x, g, wt)


def _inproj_t_body(x_ref, g_ref, wt_ref, o_ref, ob_ref):
    y = _dot_nt(wt_ref[...], _rms(x_ref[...], g_ref[...]).astype(BF16))
    o_ref[...] = y
    ob_ref[...] = y.astype(BF16)


def _inproj_t(x, g, wt, *, tm=512):
    n, d = x.shape
    dout = wt.shape[0]
    tm = min(tm, n)
    return pl.pallas_call(
        _inproj_t_body,
        grid=(n // tm,),
        in_specs=[
            pl.BlockSpec((tm, d), lambda i: (i, 0)),
            pl.BlockSpec((1, d), lambda i: (0, 0)),
            pl.BlockSpec((dout, d), lambda i: (0, 0)),
        ],
        out_specs=[pl.BlockSpec((dout, tm), lambda i: (0, i)), pl.BlockSpec((dout, tm), lambda i: (0, i))],
        out_shape=[jax.ShapeDtypeStruct((dout, n), F32), jax.ShapeDtypeStruct((dout, n), BF16)],
        compiler_params=_cparams(("parallel",)),
        name="inproj_t",
    )(x, g, wt)


def _merge_body(ohg_ref, on0_ref, on1_ref, ga_ref, gb_ref, wh_ref, wn_ref, y_ref):
    onsa = jnp.concatenate([on0_ref[...], on1_ref[...]], axis=-1).astype(BF16)
    yh = _dot(ohg_ref[...].astype(BF16), wh_ref[...])
    yn = _dot(onsa, wn_ref[...])
    y = jax.nn.sigmoid(ga_ref[...]) * yh + jax.nn.sigmoid(gb_ref[...]) * yn
    y_ref[...] = y.astype(BF16)


def _merge(ohg, on0, on1, main, wh, wn, *, tm=512):
    n = ohg.shape[0]
    d = wh.shape[1]
    tm = min(tm, n)
    return pl.pallas_call(
        _merge_body,
        grid=(n // tm,),
        in_specs=[
            pl.BlockSpec((tm, ohg.shape[1]), lambda i: (i, 0)),
            pl.BlockSpec((tm, on0.shape[1]), lambda i: (i, 0)),
            pl.BlockSpec((tm, on1.shape[1]), lambda i: (i, 0)),
            pl.BlockSpec((tm, d), lambda i: (i, C_GA // d)),
            pl.BlockSpec((tm, d), lambda i: (i, C_GB // d)),
            pl.BlockSpec(wh.shape, lambda i: (0, 0)),
            pl.BlockSpec(wn.shape, lambda i: (0, 0)),
        ],
        out_specs=pl.BlockSpec((tm, d), lambda i: (i, 0)),
        out_shape=jax.ShapeDtypeStruct((n, d), BF16),
        compiler_params=_cparams(("parallel",)),
        name="merge",
    )(ohg, on0, on1, main, main, wh, wn)


def _outproj_body(y_ref, x_ref, w_ref, g_ref, o_ref):
    o_ref[...] = x_ref[...] + _rms(_dot(y_ref[...], w_ref[...]), g_ref[...])


def _outproj(y, x, w, g, *, tm=512):
    n, d = x.shape
    tm = min(tm, n)
    return pl.pallas_call(
        _outproj_body,
        grid=(n // tm,),
        in_specs=[
            pl.BlockSpec((tm, d), lambda i: (i, 0)),
            pl.BlockSpec((tm, d), lambda i: (i, 0)),
            pl.BlockSpec(w.shape, lambda i: (0, 0)),
            pl.BlockSpec((1, d), lambda i: (0, 0)),
        ],
        out_specs=pl.BlockSpec((tm, d), lambda i: (i, 0)),
        out_shape=jax.ShapeDtypeStruct((n, d), F32),
        compiler_params=_cparams(("parallel",)),
        name="outproj",
    )(y, x, w, g)


def _hgrn_range_matrix(c):
    t = np.arange(c)[:, None]
    u = np.arange(c)[None, :]
    blocks = [u <= t, u > t]
    m = c // 2
    while m >= 1:
        p = (t // (2 * m)) * 2 * m + m - 1
        upper = (t // m) % 2 == 1
        blocks.append(np.where(upper, (u > p) & (u <= t), (u > t) & (u <= p)))
        m //= 2
    return np.concatenate(blocks, axis=0).astype(np.float32)


def _hgrn_gates(q_raw, f_raw, lb):
    f = lb + (1.0 - lb) * jax.nn.sigmoid(f_raw)
    return _silu(q_raw), jnp.log(f), 1.0 - f


def _hgrn_finish(o, g_raw, gnorm):
    return _rms(o, gnorm) * _silu(g_raw)


def _hgrn_prompt_body(q_ref, f_ref, i_ref, g_ref, lb_ref, gn_ref, rm_ref, o_ref, sfin_ref, s_ref, *, c, hp):
    @pl.when(pl.program_id(1) == 0)
    def _():
        s_ref[...] = jnp.zeros(s_ref.shape, F32)

    cols = [slice(j * HG_D, (j + 1) * HG_D) for j in range(hp)]
    res = [_hgrn_chunk(q_ref[:, sl], f_ref[:, sl], i_ref[:, sl], lb_ref[:, sl], rm_ref[...], s_ref[j], c)
           for j, sl in enumerate(cols)]
    outs = [_hgrn_finish(o, g_ref[:, sl], gn_ref[...]) for (o, _), sl in zip(res, cols)]
    for j, sl in enumerate(cols):
        s_ref[j] = res[j][1]
        sfin_ref[j] = res[j][1]
        o_ref[:, sl] = outs[j]


def _hgrn_chunk(q_raw, f_raw, v, lb, rm, s, c):
    q, logf, k = _hgrn_gates(q_raw, f_raw, lb)
    vb = v.astype(BF16)
    hi, mid, lo = _split3(logf)
    r3 = _dot(rm, jnp.concatenate([hi, mid, lo], axis=1))
    e = r3[:, 0:HG_D] + r3[:, HG_D:2 * HG_D] + r3[:, 2 * HG_D:3 * HG_D]
    b = e[0:c]
    qe = (q * jnp.exp(b)).astype(BF16)
    row = lax.broadcasted_iota(jnp.int32, (c, c), 0)
    col = lax.broadcasted_iota(jnp.int32, (c, c), 1)
    x = row ^ col
    a = jnp.where(x == 0, _dot_nt(q.astype(BF16), k.astype(BF16)), 0.0)
    rowk = lax.broadcasted_iota(jnp.int32, (c, HG_D), 0)
    m = c // 2
    lvl = 0
    while m >= 1:
        w = jnp.exp(e[(2 + lvl) * c:(3 + lvl) * c])
        upper = (rowk & m) != 0
        ql = jnp.where(upper, q * w, 0.0).astype(BF16)
        kl = jnp.where(upper, 0.0, k * w).astype(BF16)
        a = a + jnp.where(x < 2 * m, _dot_nt(ql, kl), 0.0)
        m //= 2
        lvl += 1
    o = _dot(jnp.concatenate([qe, a.astype(BF16)], axis=1), jnp.concatenate([s.astype(BF16), vb], axis=0))
    kd = (k * jnp.exp(e[c:2 * c])).astype(BF16)
    ecol = jnp.transpose(jnp.broadcast_to(jnp.exp(b[c - 1:c, :]), (HG_D, HG_D)))
    return o, ecol * s + _dot_tn(kd, vb)


def _hgrn_prompt(main, lb, gnorm, *, c=128, hp=8):
    t = main.shape[0]
    assert c == HG_D and HG_HEADS % hp == 0
    rm = jnp.asarray(_hgrn_range_matrix(c), BF16)
    w = hp * HG_D
    col = lambda base: (lambda h, ci: (ci, base // w + h))
    return pl.pallas_call(
        functools.partial(_hgrn_prompt_body, c=c, hp=hp),
        grid=(HG_HEADS // hp, t // c),
        in_specs=[
            pl.BlockSpec((c, w), col(C_HGQ)),
            pl.BlockSpec((c, w), col(C_HGF)),
            pl.BlockSpec((c, w), col(C_HGI)),
            pl.BlockSpec((c, w), col(C_HGG)),
            pl.BlockSpec((1, w), lambda h, ci: (0, h)),
            pl.BlockSpec((1, HG_D), lambda h, ci: (0, 0)),
            pl.BlockSpec(rm.shape, lambda h, ci: (0, 0)),
        ],
        out_specs=[
            pl.BlockSpec((c, w), lambda h, ci: (ci, h)),
            pl.BlockSpec((hp, HG_D, HG_D), lambda h, ci: (h, 0, 0)),
        ],
        out_shape=[
            jax.ShapeDtypeStruct((t, HG_HEADS * HG_D), F32),
            jax.ShapeDtypeStruct((HG_HEADS, HG_D, HG_D), F32),
        ],
        scratch_shapes=[pltpu.VMEM((hp, HG_D, HG_D), F32)],
        compiler_params=_cparams(("parallel", "arbitrary")),
        name="hgrn_prompt",
    )(main, main, main, main, lb, gnorm, rm)


def _pad16(x):
    return jnp.concatenate([x, jnp.zeros_like(x)], axis=0)


def _hgrn_sample_body(q_ref, f_ref, i_ref, g_ref, lb_ref, gn_ref, s_ref, o_ref, so_ref, *, tn):
    rows = 2 * tn
    row = lax.broadcasted_iota(jnp.int32, (rows, HG_D), 0)
    pos = row & (tn - 1)
    first = row < tn
    for h in range(HG_HEADS):
        sl = slice(h * HG_D, (h + 1) * HG_D)
        q, logf, k = _hgrn_gates(q_ref[:, sl], f_ref[:, sl], lb_ref[:, sl])
        v = i_ref[:, sl]
        b = logf
        d = 1
        while d < tn:
            b = b + jnp.where(pos >= d, pltpu.roll(b, d, 0), 0.0)
            d *= 2
        oi = jnp.sum(q * k, axis=-1, keepdims=True) * v
        for d in range(1, tn):
            ok = pos >= d
            w = jnp.exp(jnp.where(ok, b - pltpu.roll(b, d, 0), 0.0))
            a = jnp.sum(jnp.where(ok, q * pltpu.roll(k, d, 0) * w, 0.0), axis=-1, keepdims=True)
            oi = oi + a * pltpu.roll(v, d, 0)
        qe = _pad16(q * jnp.exp(b)).astype(BF16)
        blast = jnp.where(first, b[tn - 1:tn, :], b[rows - 1:rows, :])
        kd = k * jnp.exp(blast - b)
        v16 = _pad16(v).astype(BF16)
        o_inter = []
        for bi in range(2):
            s = s_ref[bi, h]
            o_inter.append(_dot(qe, s.astype(BF16))[0:rows])
            mine = first if bi == 0 else jnp.logical_not(first)
            kd16 = _pad16(jnp.where(mine, kd, 0.0)).astype(BF16)
            e_row = jnp.exp(b[(bi + 1) * tn - 1:(bi + 1) * tn, :])
            ecol = jnp.transpose(jnp.broadcast_to(e_row, (HG_D, HG_D)))
            so_ref[bi, h] = ecol * s + _dot_tn(kd16, v16)
        o = jnp.where(first, o_inter[0], o_inter[1]) + oi
        o_ref[:, sl] = _hgrn_finish(o, g_ref[:, sl], gn_ref[...])


def _hgrn_sample(main, lb, gnorm, state, *, tn):
    n = main.shape[0]
    nb = state.shape[0]
    assert n == nb * tn and tn == 4 and nb % 2 == 0
    w = HG_HEADS * HG_D
    rows = 2 * tn
    col = lambda base: (lambda i: (i, base // w))
    return pl.pallas_call(
        functools.partial(_hgrn_sample_body, tn=tn),
        grid=(nb // 2,),
        in_specs=[
            pl.BlockSpec((rows, w), col(C_HGQ)),
            pl.BlockSpec((rows, w), col(C_HGF)),
            pl.BlockSpec((rows, w), col(C_HGI)),
            pl.BlockSpec((rows, w), col(C_HGG)),
            pl.BlockSpec((1, w), lambda i: (0, 0)),
            pl.BlockSpec((1, HG_D), lambda i: (0, 0)),
            pl.BlockSpec((2, HG_HEADS, HG_D, HG_D), lambda i: (i, 0, 0, 0)),
        ],
        out_specs=[
            pl.BlockSpec((rows, w), lambda i: (i, 0)),
            pl.BlockSpec((2, HG_HEADS, HG_D, HG_D), lambda i: (i, 0, 0, 0)),
        ],
        out_shape=[
            jax.ShapeDtypeStruct((n, w), F32),
            jax.ShapeDtypeStruct(state.shape, F32),
        ],
        compiler_params=_cparams(("parallel",)),
        name="hgrn_sample",
    )(main, main, main, main, lb, gnorm, state)


def _alibi_slopes():
    return np.power(2.0, -8.0 * np.arange(1, NSA_HEADS + 1) / NSA_HEADS).astype(np.float32)


def _gelu_tanh(x):
    return 0.5 * x * (1.0 + jnp.tanh(np.float32(np.sqrt(2.0 / np.pi)) * (x + 0.044715 * (x * x * x))))


def _overlap_matrix(ncp, nlanes):
    cs = np.arange(ncp)[:, None] * CMP_STRIDE
    ss = np.arange(nlanes)[None, :] * SLC_BLOCK
    return ((cs <= ss + SLC_BLOCK - 1) & (cs + CMP_BLOCK - 1 >= ss)).astype(np.float32)


def _expand_matrix(nlanes, nkeys):
    return (np.arange(nkeys)[None, :] // SLC_BLOCK == np.arange(nlanes)[:, None]).astype(np.float32)


def _np_split3(x):
    parts = []
    r = np.asarray(x, np.float64)
    for _ in range(3):
        p = r.astype(np.float32).astype(BF16).astype(np.float64)
        parts.append(p.astype(np.float32))
        r = r - p
    return parts


def _softmax2_rows(s, mask):
    m = jnp.max(s, axis=-1, keepdims=True)
    e = jnp.where(mask, jnp.exp2(s - m), 0.0)
    l = jnp.sum(e, axis=-1, keepdims=True)
    return e * (1.0 / jnp.maximum(l, 1e-30))


def _select_blocks(score, qblk, n_top):
    blk = lax.broadcasted_iota(jnp.int32, score.shape, 1)
    forced = (blk == 0) | (blk == qblk) | (blk == qblk - 1)
    valid = blk <= qblk
    work = jnp.where(valid, jnp.where(forced, FORCE_SCORE, score), NEG)
    nrow, nblk = score.shape
    if nrow < nblk:
        work = jnp.concatenate([work, jnp.full((nblk - nrow, nblk), NEG, F32)], axis=0)
    wt = jnp.transpose(work)
    blk_t = lax.broadcasted_iota(jnp.int32, wt.shape, 0)
    sel_t = jnp.zeros(wt.shape, F32)
    for _ in range(n_top):
        m = jnp.max(wt, axis=0, keepdims=True)
        first = jnp.min(jnp.where(wt == m, blk_t, nblk), axis=0, keepdims=True)
        pick = blk_t == first
        sel_t = jnp.where(pick, 1.0, sel_t)
        wt = jnp.where(pick, NEG_PICKED, wt)
    return jnp.where(valid, jnp.transpose(sel_t)[0:nrow], 0.0)


def _compress_body(g_ref, pos_ref, w1_ref, w2_ref, o_ref):
    g = g_ref[0]
    ylo = _dot((g + pos_ref[0, 0]).astype(BF16), w1_ref[0, 0])
    yhi = _dot((g + pos_ref[0, 1]).astype(BF16), w1_ref[0, 1])
    pre = ylo + pltpu.roll(yhi, g.shape[0] - 1, 0)
    o_ref[0] = _dot(_gelu_tanh(pre).astype(BF16), w2_ref[0]).astype(BF16)


def _compress_prompt(groups, pos, w1, w2):
    na, ng, gw = groups.shape
    hid = w1.shape[-1]
    kv = lambda a: a // NSA_KV_HEADS
    return pl.pallas_call(
        _compress_body,
        grid=(na,),
        in_specs=[
            pl.BlockSpec((1, ng, gw), lambda a: (a, 0, 0)),
            pl.BlockSpec((1, 2, 1, gw), lambda a: (kv(a), 0, 0, 0)),
            pl.BlockSpec((1, 2, gw, hid), lambda a: (kv(a), 0, 0, 0)),
            pl.BlockSpec((1, hid, HEAD_DIM), lambda a: (kv(a), 0, 0)),
        ],
        out_specs=pl.BlockSpec((1, ng, HEAD_DIM), lambda a: (a, 0, 0)),
        out_shape=jax.ShapeDtypeStruct((na, ng, HEAD_DIM), BF16),
        compiler_params=_cparams(("parallel",)),
        name="compress_prompt",
    )(groups, pos, w1, w2)


SLC_CHUNK = 512
CHUNK_BLOCK_SHIFT = 3
N_SLOPE_COLS = 6
STRIP = Q_BLOCK


def _nsa_prompt_body(q_ref, gate_ref, sl_ref, kca_ref, vc_ref, ka_ref, vsa_ref, kwa_ref, vwa_ref, ov_ref,
                     o_ref, s_ref, p_ref, bias_ref, psum_ref, sw_ref, pw_ref, bw_ref, s2_ref, mx_ref, bias2_ref,
                     m_ref, alpha_ref, acc_ref, todo_ref, *, n, n_top):
    g_n = NSA_GROUP
    qb = Q_BLOCK
    nrow = g_n * qb
    nstrip = nrow // STRIP
    i = pl.program_id(0)
    start = i * qb
    q = (q_ref[...] * Q_SCALE).astype(BF16)
    qs = jnp.concatenate([q[:, HEAD_DIM * g:HEAD_DIM * (g + 1)] for g in range(g_n)], axis=0)
    qc = jnp.concatenate([qs, sl_ref[...]], axis=1)

    def strip_rows(k):
        r0 = k * STRIP
        t0 = r0 % qb
        return slice(r0, r0 + STRIP), slice(t0, t0 + STRIP)

    def for_strips(fn):
        for k in range(nstrip):
            fn(k, None)

    def softmax_strip(s):
        m = jnp.maximum(jnp.max(s, axis=-1, keepdims=True), M_INIT)
        e = jnp.exp2(s - m)
        return e * (1.0 / jnp.maximum(jnp.sum(e, axis=-1, keepdims=True), 1e-30))

    ncp = kca_ref.shape[1]
    t_c = lax.broadcasted_iota(jnp.int32, (qb, ncp), 0)
    c_c = lax.broadcasted_iota(jnp.int32, (qb, ncp), 1)
    bias_ref[:, 0:ncp] = jnp.where(t_c - CMP_STRIDE * c_c + (start - (CMP_BLOCK - 1)) >= 0, 0.0, NEG)
    s_ref[:, 0:ncp] = _dot(qc, kca_ref[...])
    psum_ref[...] = jnp.zeros(psum_ref.shape, F32)

    def cmp_strip(k, carry):
        rows, trows = strip_rows(k)
        p = softmax_strip(s_ref[rows, 0:ncp] + bias_ref[trows, 0:ncp])
        psum_ref[trows, :] += p
        p_ref[rows, 0:ncp] = p.astype(BF16)
        return carry

    for_strips(cmp_strip)
    o_cmp = _dot(p_ref[:, 0:ncp], vc_ref[...])

    nw = WINDOW + qb
    t_w = lax.broadcasted_iota(jnp.int32, (qb, nw), 0)
    w_w = lax.broadcasted_iota(jnp.int32, (qb, nw), 1)
    dist_w = t_w + WINDOW - w_w
    mask_w = (dist_w >= 0) & (dist_w < WINDOW) & (w_w >= WINDOW - start)
    bw_ref[...] = jnp.where(mask_w, 0.0, NEG)
    w0 = pl.multiple_of(start, qb)
    sw_ref[...] = _dot(qc, kwa_ref[:, pl.ds(w0, nw)])

    def win_strip(k, carry):
        rows, trows = strip_rows(k)
        s = sw_ref[rows, :] + bw_ref[trows, :]
        m = jnp.max(s, axis=-1, keepdims=True)
        pw_ref[rows, :] = jnp.exp2(s - m).astype(BF16)
        return carry

    for_strips(win_strip)
    win = _dot_nt(pw_ref[...], vwa_ref[:, pl.ds(w0, nw)])

    ph, plo = _split2(psum_ref[...])
    ov = ov_ref[...]
    score = _dot(ph, ov) + _dot(plo, ov)
    t_b = lax.broadcasted_iota(jnp.int32, score.shape, 0)
    qblk = jnp.right_shift(start + t_b, SLC_SHIFT)
    sel = _select_blocks(score, qblk, n_top)

    ck = SLC_CHUNK
    notsel = (1.0 - sel).astype(BF16)
    qa = jnp.concatenate([qc, jnp.concatenate([notsel] * g_n, axis=0)], axis=1)
    m_ref[...] = jnp.full(m_ref.shape, M_INIT, F32)
    acc_ref[...] = jnp.zeros(acc_ref.shape, F32)
    c_last = lax.div(start, ck)
    t_s = lax.broadcasted_iota(jnp.int32, (qb, ck), 0)
    s_s = lax.broadcasted_iota(jnp.int32, (qb, ck), 1)
    bias2_ref[0] = jnp.zeros((qb, ck), F32)
    bias2_ref[1] = jnp.where(t_s - s_s + (start - c_last * ck) >= 0, 0.0, NEG)

    blk_any = jnp.max(sel, axis=0, keepdims=True)
    lane = lax.broadcasted_iota(jnp.int32, blk_any.shape, 1)
    cnt = jnp.int32(0)
    for c in range(todo_ref.shape[0] - 2):
        used = jnp.max(jnp.where(jnp.right_shift(lane, CHUNK_BLOCK_SHIFT) == c, blk_any, 0.0)) > 0.5
        todo_ref[cnt] = c
        cnt = cnt + jnp.logical_and(used, c < c_last).astype(jnp.int32)
    todo_ref[cnt] = c_last
    todo_ref[cnt + 1] = c_last

    def scores(c, slot):
        k0 = pl.multiple_of(c * ck, ck)
        last = (c == c_last).astype(jnp.int32)
        s = _dot(qa, ka_ref[:, pl.ds(k0, ck)])
        for k in range(nstrip):
            rows, _ = strip_rows(k)
            sk = s[rows] + bias2_ref[last]
            s2_ref[slot, rows, :] = sk
            mx_ref[slot, rows, :] = jnp.broadcast_to(jnp.max(sk, axis=-1, keepdims=True), (qb, LANE))

    def accumulate(c, slot):
        k0 = pl.multiple_of(c * ck, ck)
        for k in range(nstrip):
            rows, _ = strip_rows(k)
            m_old = m_ref[rows, :]
            m_new = jnp.maximum(m_old, mx_ref[slot, rows, :])
            alpha_ref[rows, :] = jnp.exp2(m_old - m_new)
            m_ref[rows, :] = m_new
            p_ref[rows, 0:ck] = jnp.exp2(s2_ref[slot, rows, :] - jnp.tile(m_new, (1, ck // LANE))).astype(BF16)
        acc_ref[...] = alpha_ref[...] * acc_ref[...] + _dot_nt(p_ref[:, 0:ck], vsa_ref[:, pl.ds(k0, ck)])

    scores(todo_ref[0], 0)
    ntrip = cnt + 1

    def visit_pair(i, carry):
        j = 2 * i
        scores(todo_ref[j + 1], 1)
        accumulate(todo_ref[j], 0)
        scores(todo_ref[j + 2], 0)
        accumulate(todo_ref[j + 1], 1)
        return carry

    lax.fori_loop(0, ntrip // 2, visit_pair, 0)

    @pl.when(ntrip % 2 == 1)
    def _():
        accumulate(todo_ref[ntrip - 1], 0)

    gate = jax.nn.sigmoid(gate_ref[...])
    hd = HEAD_DIM

    def normalised(x):
        return x[:, 0:hd] * (1.0 / jnp.maximum(x[:, hd:hd + 1], 1e-30))

    outs = []
    for g in range(g_n):
        c0 = 3 * (n * g_n + g)
        rows = slice(g * qb, (g + 1) * qb)
        outs.append(gate[:, c0:c0 + 1] * o_cmp[rows] + gate[:, c0 + 1:c0 + 2] * normalised(acc_ref[rows, :])
                    + gate[:, c0 + 2:c0 + 3] * normalised(win[rows]))
    o_ref[...] = jnp.concatenate(outs, axis=-1)


def _nsa_prompt(main, kvtb, kwtp, kct, vc, n):
    t = main.shape[0]
    ncp = kct.shape[1]
    ns = t // SLC_BLOCK
    assert ns <= LANE and t % SLC_CHUNK == 0
    g_n, hd = NSA_GROUP, HEAD_DIM
    nrow = g_n * Q_BLOCK
    nw = WINDOW + Q_BLOCK
    ov = jnp.asarray(_overlap_matrix(ncp, LANE), BF16)
    sl = np.zeros((g_n, Q_BLOCK, hd), np.float32)
    parts = _np_split3(_alibi_slopes()[n * g_n:(n + 1) * g_n].astype(np.float64) * LOG2E)
    for j in range(N_SLOPE_COLS):
        sl[:, :, j] = parts[j % 3][:, None]
    sl = jnp.asarray(sl.reshape(nrow, hd), BF16)

    def with_pos_rows(kt, pos):
        rows = np.zeros((hd, pos.shape[0]), np.float32)
        rows[0:3] = (pos // SLC_BLOCK) * SLC_BLOCK
        rows[3:6] = pos % SLC_BLOCK
        return jnp.concatenate([kt, jnp.asarray(rows, BF16)], axis=0)

    ka = jnp.concatenate([with_pos_rows(kvtb[R_KSLC + n * hd:R_KSLC + (n + 1) * hd], np.arange(t)),
                          jnp.asarray(NEG * _expand_matrix(LANE, t), BF16)], axis=0)
    kca = with_pos_rows(kct, np.arange(ncp) * CMP_STRIDE + (CMP_BLOCK - 1))
    kwa = with_pos_rows(kwtp[n * hd:(n + 1) * hd], np.arange(kwtp.shape[1]))

    def with_ones_row(vt):
        rows = np.zeros((hd, vt.shape[1]), np.float32)
        rows[0] = 1.0
        return jnp.concatenate([vt, jnp.asarray(rows, BF16)], axis=0)

    vsa = with_ones_row(kvtb[R_VSLC + n * hd:R_VSLC + (n + 1) * hd])
    vwa = with_ones_row(kwtp[KVW + n * hd:KVW + (n + 1) * hd])
    wq = g_n * hd
    whole = lambda a: pl.BlockSpec(a.shape, lambda i: (0,) * a.ndim)
    return pl.pallas_call(
        functools.partial(_nsa_prompt_body, n=n, n_top=min(SLC_TOP, ns)),
        grid=(t // Q_BLOCK,),
        in_specs=[
            pl.BlockSpec((Q_BLOCK, wq), lambda i: (i, C_Q // wq + n)),
            pl.BlockSpec((Q_BLOCK, LANE), lambda i: (i, C_NGATE // LANE)),
            whole(sl), whole(kca), whole(vc), whole(ka), whole(vsa), whole(kwa), whole(vwa), whole(ov),
        ],
        out_specs=pl.BlockSpec((Q_BLOCK, wq), lambda i: (i, 0)),
        out_shape=jax.ShapeDtypeStruct((t, wq), F32),
        scratch_shapes=[
            pltpu.VMEM((nrow, ncp), F32),
            pltpu.VMEM((nrow, max(ncp, SLC_CHUNK)), BF16),
            pltpu.VMEM((Q_BLOCK, ncp), F32),
            pltpu.VMEM((Q_BLOCK, ncp), F32),
            pltpu.VMEM((nrow, nw), F32),
            pltpu.VMEM((nrow, nw), BF16),
            pltpu.VMEM((Q_BLOCK, nw), F32),
            pltpu.VMEM((2, nrow, SLC_CHUNK), F32),
            pltpu.VMEM((2, nrow, LANE), F32),
            pltpu.VMEM((2, Q_BLOCK, SLC_CHUNK), F32),
            pltpu.VMEM((nrow, LANE), F32),
            pltpu.VMEM((nrow, LANE), F32),
            pltpu.VMEM((nrow, 2 * hd), F32),
            pltpu.SMEM((t // SLC_CHUNK + 2,), jnp.int32),
        ],
        compiler_params=_cparams(("parallel",)),
        name=f"nsa_prompt_{n}",
    )(main, main, sl, kca, vc, ka, vsa, kwa, vwa, ov)


N_KINDS = 4
SAMPLE_PER_STEP = 2


def _nsa_sample_body(pt_ref, cache_ref, q_ref, new_ref, win_ref, gate_ref, slope_ref,
                     posk_ref, w1k_ref, w2k_ref, posv_ref, w1v_ref, w2v_ref, ov_ref, ex_ref, same_ref,
                     o_ref, wout_ref, buf_ref, rows_ref, newp_ref, sem, *, tn, n_top):
    b = pl.program_id(0)
    nsteps = pl.num_programs(0)
    nbs, npages = buf_ref.shape[1], buf_ref.shape[2]
    slot = b % 2

    def page_copy(step, sl, e, j):
        return pltpu.make_async_copy(cache_ref.at[pt_ref[step * nbs + e, j]], buf_ref.at[sl, e, j], sem.at[sl])

    def all_pages(step, sl, fn):
        for e in range(nbs):
            for j in range(npages):
                fn(page_copy(step, sl, e, j))

    @pl.when(b == 0)
    def _():
        all_pages(0, 0, lambda c: c.start())
        newp_ref[...] = jnp.zeros(newp_ref.shape, F32)

    @pl.when(b + 1 < nsteps)
    def _():
        all_pages(b + 1, 1 - slot, lambda c: c.start())

    all_pages(b, slot, lambda c: c.wait())
    ones = [pl.ds(e, 1) for e in range(nbs)]
    elements = [
        _nsa_sample_one(buf_ref.at[slot, e], q_ref.at[one], new_ref.at[one], win_ref.at[one], gate_ref.at[one],
                        slope_ref, posk_ref, w1k_ref, w2k_ref, posv_ref, w1v_ref, w2v_ref, ov_ref, ex_ref,
                        same_ref, o_ref.at[one], wout_ref.at[one], rows_ref.at[e], newp_ref.at[e],
                        tn=tn, n_top=n_top)
        for e, one in enumerate(ones)]
    for _ in itertools.zip_longest(*elements):
        pass


def _nsa_sample_one(pg, q_ref, new_ref, win_ref, gate_ref, slope_ref,
                    posk_ref, w1k_ref, w2k_ref, posv_ref, w1v_ref, w2v_ref, ov_ref, ex_ref, same_ref,
                    o_ref, wout_ref, rows_ref, newp_ref, *, tn, n_top):
    npages, page = pg.shape[0], pg.shape[3]
    past_len = npages * page

    for a in range(4):
        newp_ref[a, :, 0:tn] = new_ref[0, (2 + a) * KVW:(3 + a) * KVW, :]

    ncp = past_len // CMP_STRIDE
    for a in range(2):
        for j in range(npages):
            rows_ref[a, j * page:(j + 1) * page, :] = jnp.transpose(pg[j, a])
        rows_ref[a, past_len:past_len + LANE, :] = jnp.zeros((LANE, KVW), F32)

    def compress(a, pos_ref, w1_ref, w2_ref):
        lo, hi = [], []
        for r in range(CMP_STRIDE):
            x = rows_ref[a, pl.ds(r, ncp + 8, stride=CMP_STRIDE), :]
            lo.append((x[0:ncp] + pos_ref[r]).astype(BF16))
            hi.append((x[1:ncp + 1] + pos_ref[CMP_STRIDE + r]).astype(BF16))
        x = jnp.concatenate(lo + hi, axis=1)
        return _dot(_gelu_tanh(_dot(x, w1_ref[...])).astype(BF16), w2_ref[...]).astype(BF16)

    yield
    kc = compress(0, posk_ref, w1k_ref, w2k_ref)
    yield
    vc = compress(1, posv_ref, w1v_ref, w2v_ref)
    yield

    rows = q_ref.shape[1]
    q = q_ref[0]
    slope = slope_ref[:, 0:1]
    r_i = lax.broadcasted_iota(jnp.int32, (rows, 1), 0)
    t_i = r_i & (tn - 1)
    qpos = past_len + t_i
    low = r_i < rows // 2

    def pick_head(o):
        return jnp.where(low, o[:, 0:HEAD_DIM], o[:, HEAD_DIM:KVW])

    cpos = CMP_STRIDE * lax.broadcasted_iota(jnp.int32, (rows, ncp), 1) + (CMP_BLOCK - 1)
    dist_c = (qpos - cpos).astype(F32)
    mask_c = dist_c >= 0.0
    s = jnp.where(mask_c, _dot_nt(q, kc) - slope * dist_c, NEG)
    p_c = _softmax2_rows(s, mask_c)
    o_cmp = pick_head(_dot(p_c.astype(BF16), vc))

    yield
    same = same_ref[...]
    ph, plo = _split2(p_c)
    psum = _dot(same, ph) + _dot(same, plo)
    ph, plo = _split2(psum)
    ov = ov_ref[...]
    score = _dot(ph, ov) + _dot(plo, ov)
    yield
    sel = _select_blocks(score, jnp.right_shift(qpos, SLC_SHIFT), n_top)
    yield

    u_i = lax.broadcasted_iota(jnp.int32, (rows, LANE), 1)
    new_ok = u_i <= t_i
    new_blk = past_len // SLC_BLOCK
    keep = _dot(sel.astype(BF16), ex_ref[...]) > 0.5
    spos = lax.broadcasted_iota(jnp.int32, (rows, past_len), 1)
    ks = jnp.concatenate([pg[j, 2].astype(BF16) for j in range(npages)], axis=1)
    vs = jnp.concatenate([pg[j, 3].astype(BF16) for j in range(npages)], axis=1)
    s_past = _dot(q, ks)
    s_past = jnp.where(keep, s_past - slope * (qpos - spos).astype(F32), NEG)
    keep_new = new_ok & (sel[:, new_blk:new_blk + 1] > 0.5)
    s_new = jnp.where(keep_new, _dot(q, newp_ref[0].astype(BF16)) - slope * (t_i - u_i).astype(F32), NEG)
    m = jnp.maximum(jnp.max(s_past, axis=-1, keepdims=True), jnp.max(s_new, axis=-1, keepdims=True))
    e_past = jnp.where(keep, jnp.exp2(s_past - m), 0.0)
    e_new = jnp.where(keep_new, jnp.exp2(s_new - m), 0.0)
    l = jnp.sum(e_past, axis=-1, keepdims=True) + jnp.sum(e_new, axis=-1, keepdims=True)
    o_sel = _dot_nt(e_new.astype(BF16), newp_ref[1].astype(BF16)) + _dot_nt(e_past.astype(BF16), vs)
    o_sel = pick_head(o_sel * (1.0 / jnp.maximum(l, 1e-30)))

    yield
    wl = win_ref.shape[3]
    w_i = lax.broadcasted_iota(jnp.int32, (rows, wl), 1)
    dist_wi = (wl + t_i) - w_i
    mask_w = (dist_wi < WINDOW) & (w_i >= wl - past_len)
    s_win = jnp.where(mask_w, _dot(q, win_ref[0, 0].astype(BF16)) - slope * dist_wi.astype(F32), NEG)
    s_new = jnp.where(new_ok, _dot(q, newp_ref[2].astype(BF16)) - slope * (t_i - u_i).astype(F32), NEG)
    m = jnp.maximum(jnp.max(s_win, axis=-1, keepdims=True), jnp.max(s_new, axis=-1, keepdims=True))
    e_win = jnp.where(mask_w, jnp.exp2(s_win - m), 0.0)
    e_new = jnp.where(new_ok, jnp.exp2(s_new - m), 0.0)
    l = jnp.sum(e_win, axis=-1, keepdims=True) + jnp.sum(e_new, axis=-1, keepdims=True)
    o_win = (_dot_nt(e_win.astype(BF16), win_ref[0, 1].astype(BF16))
             + _dot_nt(e_new.astype(BF16), newp_ref[3].astype(BF16)))
    o_win = pick_head(o_win * (1.0 / jnp.maximum(l, 1e-30)))

    gate = jax.nn.sigmoid(gate_ref[0])
    o_ref[0] = gate[:, 0:1] * o_cmp + gate[:, 1:2] * o_sel + gate[:, 2:3] * o_win

    lane_w = lax.broadcasted_iota(jnp.int32, (KVW, LANE), 1)
    for a in range(2):
        shifted = pltpu.roll(win_ref[0, a], wl - tn, 1)
        tail = pltpu.roll(newp_ref[2 + a], LANE - tn, 1)
        wout_ref[0, a, :, 0:wl - LANE] = shifted[:, 0:wl - LANE]
        wout_ref[0, a, :, wl - LANE:wl] = jnp.where(lane_w >= LANE - tn, tail, shifted[:, wl - LANE:wl])


def _nsa_sample(page_table, cache, qrows, newt, win, gates, cmpk, cmpv, *, tn):
    nb, npages = page_table.shape
    page = cache.shape[3]
    past_len = npages * page
    wl = win.shape[3]
    rows = qrows.shape[1]
    assert tn == 4 and page == LANE and wl % LANE == 0 and wl > LANE and past_len >= wl
    ncp = past_len // CMP_STRIDE
    ns = -(-(past_len + tn) // SLC_BLOCK)
    assert ncp <= LANE and ns <= LANE
    r = np.arange(rows)
    h = r // tn
    slope = jnp.asarray(np.broadcast_to((_alibi_slopes()[h].astype(np.float64) * LOG2E)[:, None], (rows, LANE)), F32)
    nt = h // NSA_GROUP * tn + r % tn
    same = jnp.asarray((nt[:, None] == nt[None, :]).astype(np.float32), BF16)
    ov = jnp.asarray(_overlap_matrix(ncp, LANE), BF16)
    ex = jnp.asarray(_expand_matrix(LANE, past_len), BF16)
    posk, w1k, w2k = cmpk
    posv, w1v, w2v = cmpv
    nbs = SAMPLE_PER_STEP
    assert nb % nbs == 0
    const = lambda a: pl.BlockSpec(a.shape, lambda b, pt: (0,) * a.ndim)
    per_b = lambda a: pl.BlockSpec((nbs,) + a.shape[1:], lambda b, pt: (b,) + (0,) * (a.ndim - 1))
    grid_spec = pltpu.PrefetchScalarGridSpec(
        num_scalar_prefetch=1,
        grid=(nb // nbs,),
        in_specs=[
            pl.BlockSpec(memory_space=pl.ANY),
            per_b(qrows), per_b(newt), per_b(win), per_b(gates), const(slope),
            const(posk), const(w1k), const(w2k), const(posv), const(w1v), const(w2v),
            const(ov), const(ex), const(same),
        ],
        out_specs=[pl.BlockSpec((nbs, rows, HEAD_DIM), lambda b, pt: (b, 0, 0)), per_b(win)],
        scratch_shapes=[
            pltpu.VMEM((2, nbs, npages, N_KINDS, KVW, page), F32),
            pltpu.VMEM((nbs, 2, past_len + LANE, KVW), F32),
            pltpu.VMEM((nbs, 4, KVW, LANE), F32),
            pltpu.SemaphoreType.DMA((2,)),
        ],
    )
    return pl.pallas_call(
        functools.partial(_nsa_sample_body, tn=tn, n_top=min(SLC_TOP, ns)),
        grid_spec=grid_spec,
        out_shape=[jax.ShapeDtypeStruct((nb, rows, HEAD_DIM), F32), jax.ShapeDtypeStruct(win.shape, F32)],
        compiler_params=_cparams(("arbitrary",)),
        name="nsa_sample",
    )(page_table, cache, qrows, newt, win, gates, slope, posk, w1k, w2k, posv, w1v, w2v, ov, ex, same)


def _prep_ffn(wg, wu, wd, tf=512):
    wg, wu, wd = wg.astype(BF16), wu.astype(BF16), wd.astype(BF16)
    main = wg.shape[1] // tf * tf
    return wg, wu, wd, wg[:, main:], wu[:, main:], wd[main:]


def _prep_w_in(w_in):
    wt = w_in.T
    d = wt.shape[1]
    rows = [wt[:W_NSAQ], wt[W_GA:], wt[W_KV:W_KV + 2 * KVW], wt[W_NGATE:W_GA]]
    used = sum(r.shape[0] for r in rows)
    w_main = jnp.concatenate(rows + [jnp.zeros((C_Q - used, d), wt.dtype), wt[W_NSAQ:W_KV]], axis=0).astype(BF16)
    return w_main, wt[W_KV:W_NGATE].astype(BF16)


def _prep_cmp_prompt(pos_k, w1_k, w2_k, pos_v, w1_v, w2_v):
    half = CMP_STRIDE * HEAD_DIM
    pos = jnp.stack([pos_k.reshape(2, 1, half), pos_v.reshape(2, 1, half)])
    w1 = jnp.stack([w1_k.reshape(2, half, -1), w1_v.reshape(2, half, -1)]).astype(BF16)
    w2 = jnp.stack([w2_k, w2_v]).astype(BF16)
    return pos, w1, w2


def _prep_cmp_sample(pos, w1, w2):
    hid = w1.shape[1]
    w1p = w1.reshape(CMP_BLOCK, HEAD_DIM, hid)
    z1 = jnp.zeros_like(w1p)
    w1b = jnp.concatenate([jnp.concatenate([w1p, z1], axis=2), jnp.concatenate([z1, w1p], axis=2)], axis=1)
    z2 = jnp.zeros_like(w2)
    w2b = jnp.concatenate([jnp.concatenate([w2, z2], axis=1), jnp.concatenate([z2, w2], axis=1)], axis=0)
    posb = jnp.concatenate([pos, pos], axis=1)[:, None, :]
    return posb, w1b.reshape(CMP_BLOCK * KVW, 2 * hid).astype(BF16), w2b.astype(BF16)


def _project(x, g, w_main, w_kvt):
    main = _inproj(x, g, w_main, tm=1024)
    kvt, kvtb = _inproj_t(x, g, w_kvt)
    return main, kvt, kvtb


def _mix_prompt(main, kvtb, lb, gnorm, cmp_w):
    t = main.shape[0]
    o_hg, s_fin = _hgrn_prompt(main, lb, gnorm)
    nkv, hd = NSA_KV_HEADS, HEAD_DIM
    groups = main[:, C_CMP:C_CMP + 2 * KVW].reshape(t // CMP_STRIDE, CMP_STRIDE, 2, nkv, hd)
    groups = groups.transpose(2, 3, 0, 1, 4).reshape(2 * nkv, t // CMP_STRIDE, CMP_STRIDE * hd)
    cmp = _compress_prompt(groups, *cmp_w)
    kct = cmp[:nkv].transpose(0, 2, 1)
    vc = cmp[nkv:]
    kwtp = jnp.pad(kvtb[R_KWIN:N_KV_ROWS], ((0, 0), (WINDOW, 0)))
    o_nsa = [_nsa_prompt(main, kvtb, kwtp, kct[n], vc[n], n) for n in range(nkv)]
    return o_hg, o_nsa, s_fin


def _mix_sample(main, kvt, lb, gnorm, state, cache, win, page_table, cmp_w, *, tn):
    nb = state.shape[0]
    o_hg, s_new = _hgrn_sample(main, lb, gnorm, state, tn=tn)
    g_n, nkv, hd = NSA_GROUP, NSA_KV_HEADS, HEAD_DIM
    q = (main[:, C_Q:C_Q + NSA_HEADS * hd] * Q_SCALE).astype(BF16)
    q = q.reshape(nb, tn, nkv, g_n, hd).transpose(0, 2, 3, 1, 4)
    eye = jnp.eye(nkv, dtype=BF16)
    qrows = (q[:, :, :, :, None, :] * eye[None, :, None, None, :, None]).reshape(nb, nkv * g_n * tn, nkv * hd)
    gates = main[:, C_NGATE:C_NGATE + N_NGATE].reshape(nb, tn, NSA_HEADS, 3).transpose(0, 2, 1, 3)
    gates = jnp.pad(gates.reshape(nb, NSA_HEADS * tn, 3), ((0, 0), (0, 0), (0, LANE - 3)))
    newt = kvt.reshape(N_KV_ROWS, nb, tn).transpose(1, 0, 2)
    o, wout = _nsa_sample(page_table, cache, qrows, newt, win, gates, *cmp_w, tn=tn)
    o = o.reshape(nb, nkv, g_n, tn, hd).transpose(0, 3, 1, 2, 4).reshape(nb * tn, nkv, g_n * hd)
    return o_hg, [o[:, n] for n in range(nkv)], s_new, wout


def kernel(x_prompt, x_sample, cache_kv, cache_win, state_hgrn, page_table, norm_pre1, norm_post1, ff1_gate, ff1_up, ff1_down, norm_pre2, norm_post2, w_in, hg_lb, hg_gnorm, cmp_pos_k, cmp_w1_k, cmp_w2_k, cmp_pos_v, cmp_w1_v, cmp_w2_v, w_proj_hg, w_proj_nsa, w_out, norm_pre3, norm_post3, ff2_gate, ff2_up, ff2_down):
    depth = norm_pre1.shape[0]
    bp, tp, d = x_prompt.shape
    nb, tn, _ = x_sample.shape
    assert bp == 1
    nkv, hd = NSA_KV_HEADS, HEAD_DIM
    lb_all = jnp.cumsum(jax.nn.softmax(hg_lb.astype(F32), axis=0), axis=0)[:depth]
    xp = x_prompt.reshape(tp, d)
    xs = x_sample.reshape(nb * tn, d)
    kv_p, kv_s, win_p, win_s, st_p, st_s = [], [], [], [], [], []
    row = lambda a: a.reshape(1, -1)
    for l in range(depth):
        ff1 = _prep_ffn(ff1_gate[l], ff1_up[l], ff1_down[l])
        ff2 = _prep_ffn(ff2_gate[l], ff2_up[l], ff2_down[l])
        w_in_l = _prep_w_in(w_in[l])
        wh, wn, wo = w_proj_hg[l].astype(BF16), w_proj_nsa[l].astype(BF16), w_out[l].astype(BF16)
        cmp_p = _prep_cmp_prompt(cmp_pos_k[l], cmp_w1_k[l], cmp_w2_k[l], cmp_pos_v[l], cmp_w1_v[l], cmp_w2_v[l])
        cmp_s = (_prep_cmp_sample(cmp_pos_k[l], cmp_w1_k[l], cmp_w2_k[l]),
                 _prep_cmp_sample(cmp_pos_v[l], cmp_w1_v[l], cmp_w2_v[l]))
        lb, gnorm = row(lb_all[l]), row(hg_gnorm[l])
        n_pool, page = cache_kv.shape[1], cache_kv.shape[2]
        cache = cache_kv[l].transpose(0, 2, 3, 4, 1).reshape(n_pool, N_KINDS, KVW, page)
        wl = cache_win.shape[2]
        win = cache_win[l].transpose(0, 2, 3, 4, 1).reshape(nb, 2, KVW, wl)

        xp = _ffn(xp, row(norm_pre1[l]), row(norm_post1[l]), *ff1)
        xs = _ffn(xs, row(norm_pre1[l]), row(norm_post1[l]), *ff1)

        main_p, kvt_p, kvtb_p = _project(xp, row(norm_pre2[l]), *w_in_l)
        main_s, kvt_s, _ = _project(xs, row(norm_pre2[l]), *w_in_l)
        ohg_p, onsa_p, s_p = _mix_prompt(main_p, kvtb_p, lb, gnorm, cmp_p)
        ohg_s, onsa_s, s_s, wout = _mix_sample(main_s, kvt_s, lb, gnorm, state_hgrn[l], cache, win,
                                               page_table, cmp_s, tn=tn)
        xp = _outproj(_merge(ohg_p, *onsa_p, main_p, wh, wn), xp, wo, row(norm_post2[l]))
        xs = _outproj(_merge(ohg_s, *onsa_s, main_s, wh, wn), xs, wo, row(norm_post2[l]))

        kv_p.append(kvt_p[:N_KINDS * KVW].reshape(N_KINDS, nkv, hd, bp, tp).transpose(3, 4, 0, 1, 2))
        kv_s.append(kvt_s[:N_KINDS * KVW].reshape(N_KINDS, nkv, hd, nb, tn).transpose(3, 4, 0, 1, 2))
        wp = min(WINDOW, tp)
        win_p.append(kvt_p[R_KWIN:, tp - wp:].reshape(2, nkv, hd, bp, wp).transpose(3, 4, 0, 1, 2))
        win_s.append(wout.reshape(nb, 2, nkv, hd, wl).transpose(0, 4, 1, 2, 3))
        st_p.append(s_p[None])
        st_s.append(s_s)

        xp = _ffn(xp, row(norm_pre3[l]), row(norm_post3[l]), *ff2)
        xs = _ffn(xs, row(norm_pre3[l]), row(norm_post3[l]), *ff2)
    return (xp.reshape(bp, tp, d), xs.reshape(nb, tn, d), jnp.stack(kv_p), jnp.stack(kv_s),
            jnp.stack(win_p), jnp.stack(win_s), jnp.stack(st_p), jnp.stack(st_s))
```

```python
import functools
import itertools

import numpy as np
import jax
import jax.numpy as jnp
from jax import lax
from jax.experimental import pallas as pl
from jax.experimental.pallas import tpu as pltpu

F32 = jnp.float32
BF16 = jnp.bfloat16

EPS = 1e-6
HG_HEADS = 8
HG_D = 128
NSA_HEADS = 16
NSA_KV_HEADS = 2
NSA_GROUP = NSA_HEADS // NSA_KV_HEADS
HEAD_DIM = 64
CMP_BLOCK = 32
CMP_STRIDE = 16
SLC_BLOCK = 64
SLC_SHIFT = 6
SLC_TOP = 16
WINDOW = 512
Q_BLOCK = 128
FORCE_SCORE = 1.0e4
NEG = -1.0e30
NEG_PICKED = -3.0e38
M_INIT = -1.0e20
LOG2E = float(np.log2(np.e))

LANE = 128
VMEM_LIMIT = 56 * 1024 * 1024

W_NSAQ, W_KV, W_NGATE, W_GA = 4096, 5120, 5888, 5936
N_NGATE = 3 * NSA_HEADS
KVW = NSA_KV_HEADS * HEAD_DIM
C_HGQ, C_HGF, C_HGI, C_HGG = 0, 1024, 2048, 3072
C_GA, C_GB = 4096, 6144
C_CMP = 8192
C_NGATE = 8448
C_Q = 8704
D_MAIN = 9728
Q_SCALE = HEAD_DIM ** -0.5 * LOG2E
R_KSLC, R_VSLC, R_KWIN, R_VWIN = 2 * KVW, 3 * KVW, 4 * KVW, 5 * KVW
N_KV_ROWS = 6 * KVW


def _cparams(sem):
    return pltpu.CompilerParams(dimension_semantics=sem, vmem_limit_bytes=VMEM_LIMIT)


def _rms(x, g):
    return x * lax.rsqrt(jnp.mean(x * x, axis=-1, keepdims=True) + EPS) * g


def _silu(x):
    return x * jax.nn.sigmoid(x)


def _dot(a, b):
    return jnp.dot(a, b, preferred_element_type=F32)


def _dot_nt(a, b):
    return lax.dot_general(a, b, (((1,), (1,)), ((), ())), preferred_element_type=F32)


def _dot_tn(a, b):
    return lax.dot_general(a, b, (((0,), (0,)), ((), ())), preferred_element_type=F32)


def _split3(x):
    hi = x.astype(BF16)
    r = x - hi.astype(F32)
    mid = r.astype(BF16)
    lo = (r - mid.astype(F32)).astype(BF16)
    return hi, mid, lo


def _split2(x):
    hi = x.astype(BF16)
    return hi, (x - hi.astype(F32)).astype(BF16)


def _ffn_body(x_ref, npre_ref, npost_ref, wg_ref, wu_ref, wd_ref, wgr_ref, wur_ref, wdr_ref, o_ref, h_ref, acc_ref):
    f = pl.program_id(1)

    @pl.when(f == 0)
    def _():
        h_ref[...] = _rms(x_ref[...], npre_ref[...]).astype(BF16)
        acc_ref[...] = jnp.zeros_like(acc_ref)

    def partial_down(wg, wu, wd):
        h = h_ref[...]
        a = _silu(_dot(h, wg[...])) * _dot(h, wu[...])
        return _dot(a.astype(BF16), wd[...])

    acc_ref[...] += partial_down(wg_ref, wu_ref, wd_ref)

    @pl.when(f == pl.num_programs(1) - 1)
    def _():
        y = acc_ref[...] + partial_down(wgr_ref, wur_ref, wdr_ref)
        o_ref[...] = x_ref[...] + 0.5 * _rms(y, npost_ref[...])


def _ffn(x, npre, npost, wg, wu, wd, wgr, wur, wdr, *, tm=512, tf=512):
    n, d = x.shape
    dff = wg.shape[1]
    rem = wgr.shape[1]
    assert rem > 0 and (dff - rem) % tf == 0 and rem % LANE == 0
    tm = min(tm, n)
    return pl.pallas_call(
        _ffn_body,
        grid=(n // tm, (dff - rem) // tf),
        in_specs=[
            pl.BlockSpec((tm, d), lambda i, f: (i, 0)),
            pl.BlockSpec((1, d), lambda i, f: (0, 0)),
            pl.BlockSpec((1, d), lambda i, f: (0, 0)),
            pl.BlockSpec((d, tf), lambda i, f: (0, f)),
            pl.BlockSpec((d, tf), lambda i, f: (0, f)),
            pl.BlockSpec((tf, d), lambda i, f: (f, 0)),
            pl.BlockSpec((d, rem), lambda i, f: (0, 0)),
            pl.BlockSpec((d, rem), lambda i, f: (0, 0)),
            pl.BlockSpec((rem, d), lambda i, f: (0, 0)),
        ],
        out_specs=pl.BlockSpec((tm, d), lambda i, f: (i, 0)),
        out_shape=jax.ShapeDtypeStruct((n, d), F32),
        scratch_shapes=[pltpu.VMEM((tm, d), BF16), pltpu.VMEM((tm, d), F32)],
        compiler_params=_cparams(("parallel", "arbitrary")),
        name="ffn",
    )(x, npre, npost, wg, wu, wd, wgr, wur, wdr)


def _inproj_body(x_ref, g_ref, wt_ref, o_ref, h_ref):
    @pl.when(pl.program_id(1) == 0)
    def _():
        h_ref[...] = _rms(x_ref[...], g_ref[...]).astype(BF16)

    o_ref[...] = _dot_nt(h_ref[...], wt_ref[...])


def _inproj(x, g, wt, *, tm=512, tn=512):
    n, d = x.shape
    dout = wt.shape[0]
    tm = min(tm, n)
    return pl.pallas_call(
        _inproj_body,
        grid=(n // tm, dout // tn),
        in_specs=[
            pl.BlockSpec((tm, d), lambda i, j: (i, 0)),
            pl.BlockSpec((1, d), lambda i, j: (0, 0)),
            pl.BlockSpec((tn, d), lambda i, j: (j, 0)),
        ],
        out_specs=pl.BlockSpec((tm, tn), lambda i, j: (i, j)),
        out_shape=jax.ShapeDtypeStruct((n, dout), F32),
        scratch_shapes=[pltpu.VMEM((tm, d), BF16)],
        compiler_params=_cparams(("parallel", "arbitrary")),
        name="inproj",
    )(x, g, wt)


def _inproj_t_body(x_ref, g_ref, wt_ref, o_ref, ob_ref):
    y = _dot_nt(wt_ref[...], _rms(x_ref[...], g_ref[...]).astype(BF16))
    o_ref[...] = y
    ob_ref[...] = y.astype(BF16)


def _inproj_t(x, g, wt, *, tm=512):
    n, d = x.shape
    dout = wt.shape[0]
    tm = min(tm, n)
    return pl.pallas_call(
        _inproj_t_body,
        grid=(n // tm,),
        in_specs=[
            pl.BlockSpec((tm, d), lambda i: (i, 0)),
            pl.BlockSpec((1, d), lambda i: (0, 0)),
            pl.BlockSpec((dout, d), lambda i: (0, 0)),
        ],
        out_specs=[pl.BlockSpec((dout, tm), lambda i: (0, i)), pl.BlockSpec((dout, tm), lambda i: (0, i))],
        out_shape=[jax.ShapeDtypeStruct((dout, n), F32), jax.ShapeDtypeStruct((dout, n), BF16)],
        compiler_params=_cparams(("parallel",)),
        name="inproj_t",
    )(x, g, wt)


def _merge_body(ohg_ref, on0_ref, on1_ref, ga_ref, gb_ref, wh_ref, wn_ref, y_ref):
    onsa = jnp.concatenate([on0_ref[...], on1_ref[...]], axis=-1).astype(BF16)
    yh = _dot(ohg_ref[...].astype(BF16), wh_ref[...])
    yn = _dot(onsa, wn_ref[...])
    y = jax.nn.sigmoid(ga_ref[...]) * yh + jax.nn.sigmoid(gb_ref[...]) * yn
    y_ref[...] = y.astype(BF16)


def _merge(ohg, on0, on1, main, wh, wn, *, tm=512):
    n = ohg.shape[0]
    d = wh.shape[1]
    tm = min(tm, n)
    return pl.pallas_call(
        _merge_body,
        grid=(n // tm,),
        in_specs=[
            pl.BlockSpec((tm, ohg.shape[1]), lambda i: (i, 0)),
            pl.BlockSpec((tm, on0.shape[1]), lambda i: (i, 0)),
            pl.BlockSpec((tm, on1.shape[1]), lambda i: (i, 0)),
            pl.BlockSpec((tm, d), lambda i: (i, C_GA // d)),
            pl.BlockSpec((tm, d), lambda i: (i, C_GB // d)),
            pl.BlockSpec(wh.shape, lambda i: (0, 0)),
            pl.BlockSpec(wn.shape, lambda i: (0, 0)),
        ],
        out_specs=pl.BlockSpec((tm, d), lambda i: (i, 0)),
        out_shape=jax.ShapeDtypeStruct((n, d), BF16),
        compiler_params=_cparams(("parallel",)),
        name="merge",
    )(ohg, on0, on1, main, main, wh, wn)


def _outproj_body(y_ref, x_ref, w_ref, g_ref, o_ref):
    o_ref[...] = x_ref[...] + _rms(_dot(y_ref[...], w_ref[...]), g_ref[...])


def _outproj(y, x, w, g, *, tm=512):
    n, d = x.shape
    tm = min(tm, n)
    return pl.pallas_call(
        _outproj_body,
        grid=(n // tm,),
        in_specs=[
            pl.BlockSpec((tm, d), lambda i: (i, 0)),
            pl.BlockSpec((tm, d), lambda i: (i, 0)),
            pl.BlockSpec(w.shape, lambda i: (0, 0)),
            pl.BlockSpec((1, d), lambda i: (0, 0)),
        ],
        out_specs=pl.BlockSpec((tm, d), lambda i: (i, 0)),
        out_shape=jax.ShapeDtypeStruct((n, d), F32),
        compiler_params=_cparams(("parallel",)),
        name="outproj",
    )(y, x, w, g)


def _hgrn_range_matrix(c):
    t = np.arange(c)[:, None]
    u = np.arange(c)[None, :]
    blocks = [u <= t, u > t]
    m = c // 2
    while m >= 1:
        p = (t // (2 * m)) * 2 * m + m - 1
        upper = (t // m) % 2 == 1
        blocks.append(np.where(upper, (u > p) & (u <= t), (u > t) & (u <= p)))
        m //= 2
    return np.concatenate(blocks, axis=0).astype(np.float32)


def _hgrn_gates(q_raw, f_raw, lb):
    f = lb + (1.0 - lb) * jax.nn.sigmoid(f_raw)
    return _silu(q_raw), jnp.log(f), 1.0 - f


def _hgrn_finish(o, g_raw, gnorm):
    return _rms(o, gnorm) * _silu(g_raw)


def _hgrn_prompt_body(q_ref, f_ref, i_ref, g_ref, lb_ref, gn_ref, rm_ref, o_ref, sfin_ref, s_ref, *, c, hp):
    @pl.when(pl.program_id(1) == 0)
    def _():
        s_ref[...] = jnp.zeros(s_ref.shape, F32)

    cols = [slice(j * HG_D, (j + 1) * HG_D) for j in range(hp)]
    res = [None] * hp
    heads = [_hgrn_chunk(q_ref[:, sl], f_ref[:, sl], i_ref[:, sl], lb_ref[:, sl], rm_ref[...], s_ref[j], c, res, j)
             for j, sl in enumerate(cols)]
    for _ in itertools.zip_longest(*heads):
        pass
    outs = [_hgrn_finish(o, g_ref[:, sl], gn_ref[...]) for (o, _), sl in zip(res, cols)]
    for j, sl in enumerate(cols):
        s_ref[j] = res[j][1]
        sfin_ref[j] = res[j][1]
        o_ref[:, sl] = outs[j]


def _hgrn_chunk(q_raw, f_raw, v, lb, rm, s, c, res, j):
    q, logf, k = _hgrn_gates(q_raw, f_raw, lb)
    vb = v.astype(BF16)
    hi, mid, lo = _split3(logf)
    r3 = _dot(rm, jnp.concatenate([hi, mid, lo], axis=1))
    e = r3[:, 0:HG_D] + r3[:, HG_D:2 * HG_D] + r3[:, 2 * HG_D:3 * HG_D]
    b = e[0:c]
    yield
    qe = (q * jnp.exp(b)).astype(BF16)
    row = lax.broadcasted_iota(jnp.int32, (c, c), 0)
    col = lax.broadcasted_iota(jnp.int32, (c, c), 1)
    x = row ^ col
    a = jnp.where(x == 0, _dot_nt(q.astype(BF16), k.astype(BF16)), 0.0)
    rowk = lax.broadcasted_iota(jnp.int32, (c, HG_D), 0)
    m = c // 2
    lvl = 0
    while m >= 1:
        w = jnp.exp(e[(2 + lvl) * c:(3 + lvl) * c])
        upper = (rowk & m) != 0
        ql = jnp.where(upper, q * w, 0.0).astype(BF16)
        kl = jnp.where(upper, 0.0, k * w).astype(BF16)
        a = a + jnp.where(x < 2 * m, _dot_nt(ql, kl), 0.0)
        m //= 2
        lvl += 1
        yield
    o = _dot(jnp.concatenate([qe, a.astype(BF16)], axis=1), jnp.concatenate([s.astype(BF16), vb], axis=0))
    kd = (k * jnp.exp(e[c:2 * c])).astype(BF16)
    ecol = jnp.transpose(jnp.broadcast_to(jnp.exp(b[c - 1:c, :]), (HG_D, HG_D)))
    res[j] = (o, ecol * s + _dot_tn(kd, vb))


def _hgrn_prompt(main, lb, gnorm, *, c=128, hp=8):
    t = main.shape[0]
    assert c == HG_D and HG_HEADS % hp == 0
    rm = jnp.asarray(_hgrn_range_matrix(c), BF16)
    w = hp * HG_D
    col = lambda base: (lambda h, ci: (ci, base // w + h))
    return pl.pallas_call(
        functools.partial(_hgrn_prompt_body, c=c, hp=hp),
        grid=(HG_HEADS // hp, t // c),
        in_specs=[
            pl.BlockSpec((c, w), col(C_HGQ)),
            pl.BlockSpec((c, w), col(C_HGF)),
            pl.BlockSpec((c, w), col(C_HGI)),
            pl.BlockSpec((c, w), col(C_HGG)),
            pl.BlockSpec((1, w), lambda h, ci: (0, h)),
            pl.BlockSpec((1, HG_D), lambda h, ci: (0, 0)),
            pl.BlockSpec(rm.shape, lambda h, ci: (0, 0)),
        ],
        out_specs=[
            pl.BlockSpec((c, w), lambda h, ci: (ci, h)),
            pl.BlockSpec((hp, HG_D, HG_D), lambda h, ci: (h, 0, 0)),
        ],
        out_shape=[
            jax.ShapeDtypeStruct((t, HG_HEADS * HG_D), F32),
            jax.ShapeDtypeStruct((HG_HEADS, HG_D, HG_D), F32),
        ],
        scratch_shapes=[pltpu.VMEM((hp, HG_D, HG_D), F32)],
        compiler_params=_cparams(("parallel", "arbitrary")),
        name="hgrn_prompt",
    )(main, main, main, main, lb, gnorm, rm)


def _pad16(x):
    return jnp.concatenate([x, jnp.zeros_like(x)], axis=0)


def _hgrn_sample_body(q_ref, f_ref, i_ref, g_ref, lb_ref, gn_ref, s_ref, o_ref, so_ref, *, tn):
    rows = 2 * tn
    row = lax.broadcasted_iota(jnp.int32, (rows, HG_D), 0)
    pos = row & (tn - 1)
    first = row < tn
    def head(h):
        sl = slice(h * HG_D, (h + 1) * HG_D)
        q, logf, k = _hgrn_gates(q_ref[:, sl], f_ref[:, sl], lb_ref[:, sl])
        v = i_ref[:, sl]
        b = logf
        d = 1
        while d < tn:
            b = b + jnp.where(pos >= d, pltpu.roll(b, d, 0), 0.0)
            d *= 2
        oi = jnp.sum(q * k, axis=-1, keepdims=True) * v
        for d in range(1, tn):
            ok = pos >= d
            w = jnp.exp(jnp.where(ok, b - pltpu.roll(b, d, 0), 0.0))
            a = jnp.sum(jnp.where(ok, q * pltpu.roll(k, d, 0) * w, 0.0), axis=-1, keepdims=True)
            oi = oi + a * pltpu.roll(v, d, 0)
            yield
        qe = _pad16(q * jnp.exp(b)).astype(BF16)
        blast = jnp.where(first, b[tn - 1:tn, :], b[rows - 1:rows, :])
        kd = k * jnp.exp(blast - b)
        v16 = _pad16(v).astype(BF16)
        o_inter = []
        for bi in range(2):
            s = s_ref[bi, h]
            o_inter.append(_dot(qe, s.astype(BF16))[0:rows])
            mine = first if bi == 0 else jnp.logical_not(first)
            kd16 = _pad16(jnp.where(mine, kd, 0.0)).astype(BF16)
            e_row = jnp.exp(b[(bi + 1) * tn - 1:(bi + 1) * tn, :])
            ecol = jnp.transpose(jnp.broadcast_to(e_row, (HG_D, HG_D)))
            so_ref[bi, h] = ecol * s + _dot_tn(kd16, v16)
            yield
        o = jnp.where(first, o_inter[0], o_inter[1]) + oi
        o_ref[:, sl] = _hgrn_finish(o, g_ref[:, sl], gn_ref[...])

    for _ in itertools.zip_longest(*[head(h) for h in range(HG_HEADS)]):
        pass


def _hgrn_sample(main, lb, gnorm, state, *, tn):
    n = main.shape[0]
    nb = state.shape[0]
    assert n == nb * tn and tn == 4 and nb % 2 == 0
    w = HG_HEADS * HG_D
    rows = 2 * tn
    col = lambda base: (lambda i: (i, base // w))
    return pl.pallas_call(
        functools.partial(_hgrn_sample_body, tn=tn),
        grid=(nb // 2,),
        in_specs=[
            pl.BlockSpec((rows, w), col(C_HGQ)),
            pl.BlockSpec((rows, w), col(C_HGF)),
            pl.BlockSpec((rows, w), col(C_HGI)),
            pl.BlockSpec((rows, w), col(C_HGG)),
            pl.BlockSpec((1, w), lambda i: (0, 0)),
            pl.BlockSpec((1, HG_D), lambda i: (0, 0)),
            pl.BlockSpec((2, HG_HEADS, HG_D, HG_D), lambda i: (i, 0, 0, 0)),
        ],
        out_specs=[
            pl.BlockSpec((rows, w), lambda i: (i, 0)),
            pl.BlockSpec((2, HG_HEADS, HG_D, HG_D), lambda i: (i, 0, 0, 0)),
        ],
        out_shape=[
            jax.ShapeDtypeStruct((n, w), F32),
            jax.ShapeDtypeStruct(state.shape, F32),
        ],
        compiler_params=_cparams(("parallel",)),
        name="hgrn_sample",
    )(main, main, main, main, lb, gnorm, state)


def _alibi_slopes():
    return np.power(2.0, -8.0 * np.arange(1, NSA_HEADS + 1) / NSA_HEADS).astype(np.float32)


def _gelu_tanh(x):
    return 0.5 * x * (1.0 + jnp.tanh(np.float32(np.sqrt(2.0 / np.pi)) * (x + 0.044715 * (x * x * x))))


def _overlap_matrix(ncp, nlanes):
    cs = np.arange(ncp)[:, None] * CMP_STRIDE
    ss = np.arange(nlanes)[None, :] * SLC_BLOCK
    return ((cs <= ss + SLC_BLOCK - 1) & (cs + CMP_BLOCK - 1 >= ss)).astype(np.float32)


def _expand_matrix(nlanes, nkeys):
    return (np.arange(nkeys)[None, :] // SLC_BLOCK == np.arange(nlanes)[:, None]).astype(np.float32)


def _np_split3(x):
    parts = []
    r = np.asarray(x, np.float64)
    for _ in range(3):
        p = r.astype(np.float32).astype(BF16).astype(np.float64)
        parts.append(p.astype(np.float32))
        r = r - p
    return parts


def _softmax2_rows(s, mask):
    m = jnp.max(s, axis=-1, keepdims=True)
    e = jnp.where(mask, jnp.exp2(s - m), 0.0)
    l = jnp.sum(e, axis=-1, keepdims=True)
    return e * (1.0 / jnp.maximum(l, 1e-30))


def _select_blocks(score, qblk, n_top):
    blk = lax.broadcasted_iota(jnp.int32, score.shape, 1)
    forced = (blk == 0) | (blk == qblk) | (blk == qblk - 1)
    valid = blk <= qblk
    work = jnp.where(valid, jnp.where(forced, FORCE_SCORE, score), NEG)
    nrow, nblk = score.shape
    if nrow < nblk:
        work = jnp.concatenate([work, jnp.full((nblk - nrow, nblk), NEG, F32)], axis=0)
    wt = jnp.transpose(work)
    blk_t = lax.broadcasted_iota(jnp.int32, wt.shape, 0)
    sel_t = jnp.zeros(wt.shape, F32)
    for _ in range(n_top):
        m = jnp.max(wt, axis=0, keepdims=True)
        first = jnp.min(jnp.where(wt == m, blk_t, nblk), axis=0, keepdims=True)
        pick = blk_t == first
        sel_t = jnp.where(pick, 1.0, sel_t)
        wt = jnp.where(pick, NEG_PICKED, wt)
    return jnp.where(valid, jnp.transpose(sel_t)[0:nrow], 0.0)


def _compress_body(g_ref, pos_ref, w1_ref, w2_ref, o_ref):
    g = g_ref[0]
    ylo = _dot((g + pos_ref[0, 0]).astype(BF16), w1_ref[0, 0])
    yhi = _dot((g + pos_ref[0, 1]).astype(BF16), w1_ref[0, 1])
    pre = ylo + pltpu.roll(yhi, g.shape[0] - 1, 0)
    o_ref[0] = _dot(_gelu_tanh(pre).astype(BF16), w2_ref[0]).astype(BF16)


def _compress_prompt(groups, pos, w1, w2):
    na, ng, gw = groups.shape
    hid = w1.shape[-1]
    kv = lambda a: a // NSA_KV_HEADS
    return pl.pallas_call(
        _compress_body,
        grid=(na,),
        in_specs=[
            pl.BlockSpec((1, ng, gw), lambda a: (a, 0, 0)),
            pl.BlockSpec((1, 2, 1, gw), lambda a: (kv(a), 0, 0, 0)),
            pl.BlockSpec((1, 2, gw, hid), lambda a: (kv(a), 0, 0, 0)),
            pl.BlockSpec((1, hid, HEAD_DIM), lambda a: (kv(a), 0, 0)),
        ],
        out_specs=pl.BlockSpec((1, ng, HEAD_DIM), lambda a: (a, 0, 0)),
        out_shape=jax.ShapeDtypeStruct((na, ng, HEAD_DIM), BF16),
        compiler_params=_cparams(("parallel",)),
        name="compress_prompt",
    )(groups, pos, w1, w2)


SLC_CHUNK = 512
CHUNK_BLOCK_SHIFT = 3
N_SLOPE_COLS = 6
STRIP = Q_BLOCK


def _nsa_prompt_body(q_ref, gate_ref, sl_ref, kca_ref, vc_ref, ka_ref, vsa_ref, kwa_ref, vwa_ref, ov_ref,
                     o_ref, s_ref, p_ref, bias_ref, psum_ref, sw_ref, pw_ref, bw_ref, s2_ref, mx_ref, bias2_ref,
                     m_ref, alpha_ref, acc_ref, todo_ref, *, n, n_top):
    g_n = NSA_GROUP
    qb = Q_BLOCK
    nrow = g_n * qb
    nstrip = nrow // STRIP
    i = pl.program_id(0)
    start = i * qb
    q = (q_ref[...] * Q_SCALE).astype(BF16)
    qs = jnp.concatenate([q[:, HEAD_DIM * g:HEAD_DIM * (g + 1)] for g in range(g_n)], axis=0)
    qc = jnp.concatenate([qs, sl_ref[...]], axis=1)

    def strip_rows(k):
        r0 = k * STRIP
        t0 = r0 % qb
        return slice(r0, r0 + STRIP), slice(t0, t0 + STRIP)

    def softmax_strip(s):
        m = jnp.maximum(jnp.max(s, axis=-1, keepdims=True), M_INIT)
        e = jnp.exp2(s - m)
        return e * (1.0 / jnp.maximum(jnp.sum(e, axis=-1, keepdims=True), 1e-30))

    ncp = kca_ref.shape[1]
    t_c = lax.broadcasted_iota(jnp.int32, (qb, ncp), 0)
    c_c = lax.broadcasted_iota(jnp.int32, (qb, ncp), 1)
    bias_ref[:, 0:ncp] = jnp.where(t_c - CMP_STRIDE * c_c + (start - (CMP_BLOCK - 1)) >= 0, 0.0, NEG)
    s_ref[:, 0:ncp] = _dot(qc, kca_ref[...])
    psum_ref[...] = jnp.zeros(psum_ref.shape, F32)

    def cmp_strip(k):
        rows, trows = strip_rows(k)
        p = softmax_strip(s_ref[rows, 0:ncp] + bias_ref[trows, 0:ncp])
        psum_ref[trows, :] += p
        p_ref[rows, 0:ncp] = p.astype(BF16)

    nw = WINDOW + qb
    t_w = lax.broadcasted_iota(jnp.int32, (qb, nw), 0)
    w_w = lax.broadcasted_iota(jnp.int32, (qb, nw), 1)
    dist_w = t_w + WINDOW - w_w
    mask_w = (dist_w >= 0) & (dist_w < WINDOW) & (w_w >= WINDOW - start)
    bw_ref[...] = jnp.where(mask_w, 0.0, NEG)
    w0 = pl.multiple_of(start, qb)
    sw_ref[...] = _dot(qc, kwa_ref[:, pl.ds(w0, nw)])

    def win_strip(k):
        rows, trows = strip_rows(k)
        s = sw_ref[rows, :] + bw_ref[trows, :]
        m = jnp.max(s, axis=-1, keepdims=True)
        pw_ref[rows, :] = jnp.exp2(s - m).astype(BF16)

    for k in range(nstrip):
        cmp_strip(k)
        win_strip(k)
    o_cmp = _dot(p_ref[:, 0:ncp], vc_ref[...])
    win = _dot_nt(pw_ref[...], vwa_ref[:, pl.ds(w0, nw)])

    ph, plo = _split2(psum_ref[...])
    ov = ov_ref[...]
    score = _dot(ph, ov) + _dot(plo, ov)
    t_b = lax.broadcasted_iota(jnp.int32, score.shape, 0)
    qblk = jnp.right_shift(start + t_b, SLC_SHIFT)
    sel = _select_blocks(score, qblk, n_top)

    ck = SLC_CHUNK
    notsel = (1.0 - sel).astype(BF16)
    qa = jnp.concatenate([qc, jnp.concatenate([notsel] * g_n, axis=0)], axis=1)
    m_ref[...] = jnp.full(m_ref.shape, M_INIT, F32)
    acc_ref[...] = jnp.zeros(acc_ref.shape, F32)
    c_last = lax.div(start, ck)
    t_s = lax.broadcasted_iota(jnp.int32, (qb, ck), 0)
    s_s = lax.broadcasted_iota(jnp.int32, (qb, ck), 1)
    bias2_ref[0] = jnp.zeros((qb, ck), F32)
    bias2_ref[1] = jnp.where(t_s - s_s + (start - c_last * ck) >= 0, 0.0, NEG)

    blk_any = jnp.max(sel, axis=0, keepdims=True)
    lane = lax.broadcasted_iota(jnp.int32, blk_any.shape, 1)
    cnt = jnp.int32(0)
    for c in range(todo_ref.shape[0] - 2):
        used = jnp.max(jnp.where(jnp.right_shift(lane, CHUNK_BLOCK_SHIFT) == c, blk_any, 0.0)) > 0.5
        todo_ref[cnt] = c
        cnt = cnt + jnp.logical_and(used, c < c_last).astype(jnp.int32)
    todo_ref[cnt] = c_last
    todo_ref[cnt + 1] = c_last

    def scores(c, slot):
        k0 = pl.multiple_of(c * ck, ck)
        last = (c == c_last).astype(jnp.int32)
        s = _dot(qa, ka_ref[:, pl.ds(k0, ck)])
        for k in range(nstrip):
            rows, _ = strip_rows(k)
            sk = s[rows] + bias2_ref[last]
            s2_ref[slot, rows, :] = sk
            mx_ref[slot, rows, :] = jnp.broadcast_to(jnp.max(sk, axis=-1, keepdims=True), (qb, LANE))

    def accumulate(c, slot):
        k0 = pl.multiple_of(c * ck, ck)
        for k in range(nstrip):
            rows, _ = strip_rows(k)
            m_old = m_ref[rows, :]
            m_new = jnp.maximum(m_old, mx_ref[slot, rows, :])
            alpha_ref[rows, :] = jnp.exp2(m_old - m_new)
            m_ref[rows, :] = m_new
            p_ref[rows, 0:ck] = jnp.exp2(s2_ref[slot, rows, :] - jnp.tile(m_new, (1, ck // LANE))).astype(BF16)
        acc_ref[...] = alpha_ref[...] * acc_ref[...] + _dot_nt(p_ref[:, 0:ck], vsa_ref[:, pl.ds(k0, ck)])

    scores(todo_ref[0], 0)
    ntrip = cnt + 1

    def visit_pair(i, carry):
        j = 2 * i
        scores(todo_ref[j + 1], 1)
        accumulate(todo_ref[j], 0)
        scores(todo_ref[j + 2], 0)
        accumulate(todo_ref[j + 1], 1)
        return carry

    lax.fori_loop(0, ntrip // 2, visit_pair, 0)

    @pl.when(ntrip % 2 == 1)
    def _():
        accumulate(todo_ref[ntrip - 1], 0)

    gate = jax.nn.sigmoid(gate_ref[...])
    hd = HEAD_DIM

    def normalised(x):
        return x[:, 0:hd] * (1.0 / jnp.maximum(x[:, hd:hd + 1], 1e-30))

    outs = []
    for g in range(g_n):
        c0 = 3 * (n * g_n + g)
        rows = slice(g * qb, (g + 1) * qb)
        outs.append(gate[:, c0:c0 + 1] * o_cmp[rows] + gate[:, c0 + 1:c0 + 2] * normalised(acc_ref[rows, :])
                    + gate[:, c0 + 2:c0 + 3] * normalised(win[rows]))
    o_ref[...] = jnp.concatenate(outs, axis=-1)


def _nsa_prompt(main, kvtb, kwtp, kct, vc, n):
    t = main.shape[0]
    ncp = kct.shape[1]
    ns = t // SLC_BLOCK
    assert ns <= LANE and t % SLC_CHUNK == 0
    g_n, hd = NSA_GROUP, HEAD_DIM
    nrow = g_n * Q_BLOCK
    nw = WINDOW + Q_BLOCK
    ov = jnp.asarray(_overlap_matrix(ncp, LANE), BF16)
    sl = np.zeros((g_n, Q_BLOCK, hd), np.float32)
    parts = _np_split3(_alibi_slopes()[n * g_n:(n + 1) * g_n].astype(np.float64) * LOG2E)
    for j in range(N_SLOPE_COLS):
        sl[:, :, j] = parts[j % 3][:, None]
    sl = jnp.asarray(sl.reshape(nrow, hd), BF16)

    def with_pos_rows(kt, pos):
        rows = np.zeros((hd, pos.shape[0]), np.float32)
        rows[0:3] = (pos // SLC_BLOCK) * SLC_BLOCK
        rows[3:6] = pos % SLC_BLOCK
        return jnp.concatenate([kt, jnp.asarray(rows, BF16)], axis=0)

    ka = jnp.concatenate([with_pos_rows(kvtb[R_KSLC + n * hd:R_KSLC + (n + 1) * hd], np.arange(t)),
                          jnp.asarray(NEG * _expand_matrix(LANE, t), BF16)], axis=0)
    kca = with_pos_rows(kct, np.arange(ncp) * CMP_STRIDE + (CMP_BLOCK - 1))
    kwa = with_pos_rows(kwtp[n * hd:(n + 1) * hd], np.arange(kwtp.shape[1]))

    def with_ones_row(vt):
        rows = np.zeros((hd, vt.shape[1]), np.float32)
        rows[0] = 1.0
        return jnp.concatenate([vt, jnp.asarray(rows, BF16)], axis=0)

    vsa = with_ones_row(kvtb[R_VSLC + n * hd:R_VSLC + (n + 1) * hd])
    vwa = with_ones_row(kwtp[KVW + n * hd:KVW + (n + 1) * hd])
    wq = g_n * hd
    whole = lambda a: pl.BlockSpec(a.shape, lambda i: (0,) * a.ndim)
    return pl.pallas_call(
        functools.partial(_nsa_prompt_body, n=n, n_top=min(SLC_TOP, ns)),
        grid=(t // Q_BLOCK,),
        in_specs=[
            pl.BlockSpec((Q_BLOCK, wq), lambda i: (i, C_Q // wq + n)),
            pl.BlockSpec((Q_BLOCK, LANE), lambda i: (i, C_NGATE // LANE)),
            whole(sl), whole(kca), whole(vc), whole(ka), whole(vsa), whole(kwa), whole(vwa), whole(ov),
        ],
        out_specs=pl.BlockSpec((Q_BLOCK, wq), lambda i: (i, 0)),
        out_shape=jax.ShapeDtypeStruct((t, wq), F32),
        scratch_shapes=[
            pltpu.VMEM((nrow, ncp), F32),
            pltpu.VMEM((nrow, max(ncp, SLC_CHUNK)), BF16),
            pltpu.VMEM((Q_BLOCK, ncp), F32),
            pltpu.VMEM((Q_BLOCK, ncp), F32),
            pltpu.VMEM((nrow, nw), F32),
            pltpu.VMEM((nrow, nw), BF16),
            pltpu.VMEM((Q_BLOCK, nw), F32),
            pltpu.VMEM((2, nrow, SLC_CHUNK), F32),
            pltpu.VMEM((2, nrow, LANE), F32),
            pltpu.VMEM((2, Q_BLOCK, SLC_CHUNK), F32),
            pltpu.VMEM((nrow, LANE), F32),
            pltpu.VMEM((nrow, LANE), F32),
            pltpu.VMEM((nrow, 2 * hd), F32),
            pltpu.SMEM((t // SLC_CHUNK + 2,), jnp.int32),
        ],
        compiler_params=_cparams(("parallel",)),
        name=f"nsa_prompt_{n}",
    )(main, main, sl, kca, vc, ka, vsa, kwa, vwa, ov)


N_KINDS = 4
SAMPLE_PER_STEP = 2


def _nsa_sample_body(pt_ref, cache_ref, q_ref, new_ref, win_ref, gate_ref, slope_ref,
                     posk_ref, w1k_ref, w2k_ref, posv_ref, w1v_ref, w2v_ref, ov_ref, ex_ref, same_ref,
                     o_ref, wout_ref, buf_ref, rows_ref, newp_ref, sem, *, tn, n_top):
    b = pl.program_id(0)
    nsteps = pl.num_programs(0)
    nbs, npages = buf_ref.shape[1], buf_ref.shape[2]
    slot = b % 2

    def page_copy(step, sl, e, j):
        return pltpu.make_async_copy(cache_ref.at[pt_ref[step * nbs + e, j]], buf_ref.at[sl, e, j], sem.at[sl])

    def all_pages(step, sl, fn):
        for e in range(nbs):
            for j in range(npages):
                fn(page_copy(step, sl, e, j))

    @pl.when(b == 0)
    def _():
        all_pages(0, 0, lambda c: c.start())
        newp_ref[...] = jnp.zeros(newp_ref.shape, F32)

    @pl.when(b + 1 < nsteps)
    def _():
        all_pages(b + 1, 1 - slot, lambda c: c.start())

    all_pages(b, slot, lambda c: c.wait())
    ones = [pl.ds(e, 1) for e in range(nbs)]
    elements = [
        _nsa_sample_one(buf_ref.at[slot, e], q_ref.at[one], new_ref.at[one], win_ref.at[one], gate_ref.at[one],
                        slope_ref, posk_ref, w1k_ref, w2k_ref, posv_ref, w1v_ref, w2v_ref, ov_ref, ex_ref,
                        same_ref, o_ref.at[one], wout_ref.at[one], rows_ref.at[e], newp_ref.at[e],
                        tn=tn, n_top=n_top)
        for e, one in enumerate(ones)]
    for _ in itertools.zip_longest(*elements):
        pass


def _nsa_sample_one(pg, q_ref, new_ref, win_ref, gate_ref, slope_ref,
                    posk_ref, w1k_ref, w2k_ref, posv_ref, w1v_ref, w2v_ref, ov_ref, ex_ref, same_ref,
                    o_ref, wout_ref, rows_ref, newp_ref, *, tn, n_top):
    npages, page = pg.shape[0], pg.shape[3]
    past_len = npages * page

    for a in range(4):
        newp_ref[a, :, 0:tn] = new_ref[0, (2 + a) * KVW:(3 + a) * KVW, :]

    ncp = past_len // CMP_STRIDE
    for a in range(2):
        for j in range(npages):
            rows_ref[a, j * page:(j + 1) * page, :] = jnp.transpose(pg[j, a])
        rows_ref[a, past_len:past_len + LANE, :] = jnp.zeros((LANE, KVW), F32)

    def compress(a, pos_ref, w1_ref, w2_ref):
        lo, hi = [], []
        for r in range(CMP_STRIDE):
            x = rows_ref[a, pl.ds(r, ncp + 8, stride=CMP_STRIDE), :]
            lo.append((x[0:ncp] + pos_ref[r]).astype(BF16))
            hi.append((x[1:ncp + 1] + pos_ref[CMP_STRIDE + r]).astype(BF16))
        x = jnp.concatenate(lo + hi, axis=1)
        return _dot(_gelu_tanh(_dot(x, w1_ref[...])).astype(BF16), w2_ref[...]).astype(BF16)

    yield
    kc = compress(0, posk_ref, w1k_ref, w2k_ref)
    yield
    vc = compress(1, posv_ref, w1v_ref, w2v_ref)
    yield

    rows = q_ref.shape[1]
    q = q_ref[0]
    slope = slope_ref[:, 0:1]
    r_i = lax.broadcasted_iota(jnp.int32, (rows, 1), 0)
    t_i = r_i & (tn - 1)
    qpos = past_len + t_i
    low = r_i < rows // 2

    def pick_head(o):
        return jnp.where(low, o[:, 0:HEAD_DIM], o[:, HEAD_DIM:KVW])

    cpos = CMP_STRIDE * lax.broadcasted_iota(jnp.int32, (rows, ncp), 1) + (CMP_BLOCK - 1)
    dist_c = (qpos - cpos).astype(F32)
    mask_c = dist_c >= 0.0
    s = jnp.where(mask_c, _dot_nt(q, kc) - slope * dist_c, NEG)
    p_c = _softmax2_rows(s, mask_c)
    o_cmp = pick_head(_dot(p_c.astype(BF16), vc))

    yield
    same = same_ref[...]
    ph, plo = _split2(p_c)
    psum = _dot(same, ph) + _dot(same, plo)
    ph, plo = _split2(psum)
    ov = ov_ref[...]
    score = _dot(ph, ov) + _dot(plo, ov)
    yield
    sel = _select_blocks(score, jnp.right_shift(qpos, SLC_SHIFT), n_top)
    yield

    u_i = lax.broadcasted_iota(jnp.int32, (rows, LANE), 1)
    new_ok = u_i <= t_i
    new_blk = past_len // SLC_BLOCK
    keep = _dot(sel.astype(BF16), ex_ref[...]) > 0.5
    spos = lax.broadcasted_iota(jnp.int32, (rows, past_len), 1)
    ks = jnp.concatenate([pg[j, 2].astype(BF16) for j in range(npages)], axis=1)
    vs = jnp.concatenate([pg[j, 3].astype(BF16) for j in range(npages)], axis=1)
    s_past = _dot(q, ks)
    s_past = jnp.where(keep, s_past - slope * (qpos - spos).astype(F32), NEG)
    keep_new = new_ok & (sel[:, new_blk:new_blk + 1] > 0.5)
    s_new = jnp.where(keep_new, _dot(q, newp_ref[0].astype(BF16)) - slope * (t_i - u_i).astype(F32), NEG)
    m = jnp.maximum(jnp.max(s_past, axis=-1, keepdims=True), jnp.max(s_new, axis=-1, keepdims=True))
    e_past = jnp.where(keep, jnp.exp2(s_past - m), 0.0)
    e_new = jnp.where(keep_new, jnp.exp2(s_new - m), 0.0)
    l = jnp.sum(e_past, axis=-1, keepdims=True) + jnp.sum(e_new, axis=-1, keepdims=True)
    o_sel = _dot_nt(e_new.astype(BF16), newp_ref[1].astype(BF16)) + _dot_nt(e_past.astype(BF16), vs)
    o_sel = pick_head(o_sel * (1.0 / jnp.maximum(l, 1e-30)))

    yield
    wl = win_ref.shape[3]
    w_i = lax.broadcasted_iota(jnp.int32, (rows, wl), 1)
    dist_wi = (wl + t_i) - w_i
    mask_w = (dist_wi < WINDOW) & (w_i >= wl - past_len)
    s_win = jnp.where(mask_w, _dot(q, win_ref[0, 0].astype(BF16)) - slope * dist_wi.astype(F32), NEG)
    s_new = jnp.where(new_ok, _dot(q, newp_ref[2].astype(BF16)) - slope * (t_i - u_i).astype(F32), NEG)
    m = jnp.maximum(jnp.max(s_win, axis=-1, keepdims=True), jnp.max(s_new, axis=-1, keepdims=True))
    e_win = jnp.where(mask_w, jnp.exp2(s_win - m), 0.0)
    e_new = jnp.where(new_ok, jnp.exp2(s_new - m), 0.0)
    l = jnp.sum(e_win, axis=-1, keepdims=True) + jnp.sum(e_new, axis=-1, keepdims=True)
    o_win = (_dot_nt(e_win.astype(BF16), win_ref[0, 1].astype(BF16))
             + _dot_nt(e_new.astype(BF16), newp_ref[3].astype(BF16)))
    o_win = pick_head(o_win * (1.0 / jnp.maximum(l, 1e-30)))

    gate = jax.nn.sigmoid(gate_ref[0])
    o_ref[0] = gate[:, 0:1] * o_cmp + gate[:, 1:2] * o_sel + gate[:, 2:3] * o_win

    lane_w = lax.broadcasted_iota(jnp.int32, (KVW, LANE), 1)
    for a in range(2):
        shifted = pltpu.roll(win_ref[0, a], wl - tn, 1)
        tail = pltpu.roll(newp_ref[2 + a], LANE - tn, 1)
        wout_ref[0, a, :, 0:wl - LANE] = shifted[:, 0:wl - LANE]
        wout_ref[0, a, :, wl - LANE:wl] = jnp.where(lane_w >= LANE - tn, tail, shifted[:, wl - LANE:wl])


def _nsa_sample(page_table, cache, qrows, newt, win, gates, cmpk, cmpv, *, tn):
    nb, npages = page_table.shape
    page = cache.shape[3]
    past_len = npages * page
    wl = win.shape[3]
    rows = qrows.shape[1]
    assert tn == 4 and page == LANE and wl % LANE == 0 and wl > LANE and past_len >= wl
    ncp = past_len // CMP_STRIDE
    ns = -(-(past_len + tn) // SLC_BLOCK)
    assert ncp <= LANE and ns <= LANE
    r = np.arange(rows)
    h = r // tn
    slope = jnp.asarray(np.broadcast_to((_alibi_slopes()[h].astype(np.float64) * LOG2E)[:, None], (rows, LANE)), F32)
    nt = h // NSA_GROUP * tn + r % tn
    same = jnp.asarray((nt[:, None] == nt[None, :]).astype(np.float32), BF16)
    ov = jnp.asarray(_overlap_matrix(ncp, LANE), BF16)
    ex = jnp.asarray(_expand_matrix(LANE, past_len), BF16)
    posk, w1k, w2k = cmpk
    posv, w1v, w2v = cmpv
    nbs = SAMPLE_PER_STEP
    assert nb % nbs == 0
    const = lambda a: pl.BlockSpec(a.shape, lambda b, pt: (0,) * a.ndim)
    per_b = lambda a: pl.BlockSpec((nbs,) + a.shape[1:], lambda b, pt: (b,) + (0,) * (a.ndim - 1))
    grid_spec = pltpu.PrefetchScalarGridSpec(
        num_scalar_prefetch=1,
        grid=(nb // nbs,),
        in_specs=[
            pl.BlockSpec(memory_space=pl.ANY),
            per_b(qrows), per_b(newt), per_b(win), per_b(gates), const(slope),
            const(posk), const(w1k), const(w2k), const(posv), const(w1v), const(w2v),
            const(ov), const(ex), const(same),
        ],
        out_specs=[pl.BlockSpec((nbs, rows, HEAD_DIM), lambda b, pt: (b, 0, 0)), per_b(win)],
        scratch_shapes=[
            pltpu.VMEM((2, nbs, npages, N_KINDS, KVW, page), F32),
            pltpu.VMEM((nbs, 2, past_len + LANE, KVW), F32),
            pltpu.VMEM((nbs, 4, KVW, LANE), F32),
            pltpu.SemaphoreType.DMA((2,)),
        ],
    )
    return pl.pallas_call(
        functools.partial(_nsa_sample_body, tn=tn, n_top=min(SLC_TOP, ns)),
        grid_spec=grid_spec,
        out_shape=[jax.ShapeDtypeStruct((nb, rows, HEAD_DIM), F32), jax.ShapeDtypeStruct(win.shape, F32)],
        compiler_params=_cparams(("arbitrary",)),
        name="nsa_sample",
    )(page_table, cache, qrows, newt, win, gates, slope, posk, w1k, w2k, posv, w1v, w2v, ov, ex, same)


def _prep_ffn(wg, wu, wd, tf=512):
    wg, wu, wd = wg.astype(BF16), wu.astype(BF16), wd.astype(BF16)
    main = wg.shape[1] // tf * tf
    return wg, wu, wd, wg[:, main:], wu[:, main:], wd[main:]


def _prep_w_in(w_in):
    wt = w_in.T
    d = wt.shape[1]
    rows = [wt[:W_NSAQ], wt[W_GA:], wt[W_KV:W_KV + 2 * KVW], wt[W_NGATE:W_GA]]
    used = sum(r.shape[0] for r in rows)
    w_main = jnp.concatenate(rows + [jnp.zeros((C_Q - used, d), wt.dtype), wt[W_NSAQ:W_KV]], axis=0).astype(BF16)
    return w_main, wt[W_KV:W_NGATE].astype(BF16)


def _prep_cmp_prompt(pos_k, w1_k, w2_k, pos_v, w1_v, w2_v):
    half = CMP_STRIDE * HEAD_DIM
    pos = jnp.stack([pos_k.reshape(2, 1, half), pos_v.reshape(2, 1, half)])
    w1 = jnp.stack([w1_k.reshape(2, half, -1), w1_v.reshape(2, half, -1)]).astype(BF16)
    w2 = jnp.stack([w2_k, w2_v]).astype(BF16)
    return pos, w1, w2


def _prep_cmp_sample(pos, w1, w2):
    hid = w1.shape[1]
    w1p = w1.reshape(CMP_BLOCK, HEAD_DIM, hid)
    z1 = jnp.zeros_like(w1p)
    w1b = jnp.concatenate([jnp.concatenate([w1p, z1], axis=2), jnp.concatenate([z1, w1p], axis=2)], axis=1)
    z2 = jnp.zeros_like(w2)
    w2b = jnp.concatenate([jnp.concatenate([w2, z2], axis=1), jnp.concatenate([z2, w2], axis=1)], axis=0)
    posb = jnp.concatenate([pos, pos], axis=1)[:, None, :]
    return posb, w1b.reshape(CMP_BLOCK * KVW, 2 * hid).astype(BF16), w2b.astype(BF16)


def _project(x, g, w_main, w_kvt):
    main = _inproj(x, g, w_main, tm=1024)
    kvt, kvtb = _inproj_t(x, g, w_kvt)
    return main, kvt, kvtb


def _mix_prompt(main, kvtb, lb, gnorm, cmp_w):
    t = main.shape[0]
    o_hg, s_fin = _hgrn_prompt(main, lb, gnorm)
    nkv, hd = NSA_KV_HEADS, HEAD_DIM
    groups = main[:, C_CMP:C_CMP + 2 * KVW].reshape(t // CMP_STRIDE, CMP_STRIDE, 2, nkv, hd)
    groups = groups.transpose(2, 3, 0, 1, 4).reshape(2 * nkv, t // CMP_STRIDE, CMP_STRIDE * hd)
    cmp = _compress_prompt(groups, *cmp_w)
    kct = cmp[:nkv].transpose(0, 2, 1)
    vc = cmp[nkv:]
    kwtp = jnp.pad(kvtb[R_KWIN:N_KV_ROWS], ((0, 0), (WINDOW, 0)))
    o_nsa = [_nsa_prompt(main, kvtb, kwtp, kct[n], vc[n], n) for n in range(nkv)]
    return o_hg, o_nsa, s_fin


def _mix_sample(main, kvt, lb, gnorm, state, cache, win, page_table, cmp_w, *, tn):
    nb = state.shape[0]
    o_hg, s_new = _hgrn_sample(main, lb, gnorm, state, tn=tn)
    g_n, nkv, hd = NSA_GROUP, NSA_KV_HEADS, HEAD_DIM
    q = (main[:, C_Q:C_Q + NSA_HEADS * hd] * Q_SCALE).astype(BF16)
    q = q.reshape(nb, tn, nkv, g_n, hd).transpose(0, 2, 3, 1, 4)
    eye = jnp.eye(nkv, dtype=BF16)
    qrows = (q[:, :, :, :, None, :] * eye[None, :, None, None, :, None]).reshape(nb, nkv * g_n * tn, nkv * hd)
    gates = main[:, C_NGATE:C_NGATE + N_NGATE].reshape(nb, tn, NSA_HEADS, 3).transpose(0, 2, 1, 3)
    gates = jnp.pad(gates.reshape(nb, NSA_HEADS * tn, 3), ((0, 0), (0, 0), (0, LANE - 3)))
    newt = kvt.reshape(N_KV_ROWS, nb, tn).transpose(1, 0, 2)
    o, wout = _nsa_sample(page_table, cache, qrows, newt, win, gates, *cmp_w, tn=tn)
    o = o.reshape(nb, nkv, g_n, tn, hd).transpose(0, 3, 1, 2, 4).reshape(nb * tn, nkv, g_n * hd)
    return o_hg, [o[:, n] for n in range(nkv)], s_new, wout


def kernel(x_prompt, x_sample, cache_kv, cache_win, state_hgrn, page_table, norm_pre1, norm_post1, ff1_gate, ff1_up, ff1_down, norm_pre2, norm_post2, w_in, hg_lb, hg_gnorm, cmp_pos_k, cmp_w1_k, cmp_w2_k, cmp_pos_v, cmp_w1_v, cmp_w2_v, w_proj_hg, w_proj_nsa, w_out, norm_pre3, norm_post3, ff2_gate, ff2_up, ff2_down):
    depth = norm_pre1.shape[0]
    bp, tp, d = x_prompt.shape
    nb, tn, _ = x_sample.shape
    assert bp == 1
    nkv, hd = NSA_KV_HEADS, HEAD_DIM
    lb_all = jnp.cumsum(jax.nn.softmax(hg_lb.astype(F32), axis=0), axis=0)[:depth]
    xp = x_prompt.reshape(tp, d)
    xs = x_sample.reshape(nb * tn, d)
    kv_p, kv_s, win_p, win_s, st_p, st_s = [], [], [], [], [], []
    row = lambda a: a.reshape(1, -1)
    for l in range(depth):
        ff1 = _prep_ffn(ff1_gate[l], ff1_up[l], ff1_down[l])
        ff2 = _prep_ffn(ff2_gate[l], ff2_up[l], ff2_down[l])
        w_in_l = _prep_w_in(w_in[l])
        wh, wn, wo = w_proj_hg[l].astype(BF16), w_proj_nsa[l].astype(BF16), w_out[l].astype(BF16)
        cmp_p = _prep_cmp_prompt(cmp_pos_k[l], cmp_w1_k[l], cmp_w2_k[l], cmp_pos_v[l], cmp_w1_v[l], cmp_w2_v[l])
        cmp_s = (_prep_cmp_sample(cmp_pos_k[l], cmp_w1_k[l], cmp_w2_k[l]),
                 _prep_cmp_sample(cmp_pos_v[l], cmp_w1_v[l], cmp_w2_v[l]))
        lb, gnorm = row(lb_all[l]), row(hg_gnorm[l])
        n_pool, page = cache_kv.shape[1], cache_kv.shape[2]
        cache = cache_kv[l].transpose(0, 2, 3, 4, 1).reshape(n_pool, N_KINDS, KVW, page)
        wl = cache_win.shape[2]
        win = cache_win[l].transpose(0, 2, 3, 4, 1).reshape(nb, 2, KVW, wl)

        xp = _ffn(xp, row(norm_pre1[l]), row(norm_post1[l]), *ff1)
        xs = _ffn(xs, row(norm_pre1[l]), row(norm_post1[l]), *ff1)

        main_p, kvt_p, kvtb_p = _project(xp, row(norm_pre2[l]), *w_in_l)
        main_s, kvt_s, _ = _project(xs, row(norm_pre2[l]), *w_in_l)
        ohg_p, onsa_p, s_p = _mix_prompt(main_p, kvtb_p, lb, gnorm, cmp_p)
        ohg_s, onsa_s, s_s, wout = _mix_sample(main_s, kvt_s, lb, gnorm, state_hgrn[l], cache, win,
                                               page_table, cmp_s, tn=tn)
        xp = _outproj(_merge(ohg_p, *onsa_p, main_p, wh, wn), xp, wo, row(norm_post2[l]))
        xs = _outproj(_merge(ohg_s, *onsa_s, main_s, wh, wn), xs, wo, row(norm_post2[l]))

        kv_p.append(kvt_p[:N_KINDS * KVW].reshape(N_KINDS, nkv, hd, bp, tp).transpose(3, 4, 0, 1, 2))
        kv_s.append(kvt_s[:N_KINDS * KVW].reshape(N_KINDS, nkv, hd, nb, tn).transpose(3, 4, 0, 1, 2))
        wp = min(WINDOW, tp)
        win_p.append(kvt_p[R_KWIN:, tp - wp:].reshape(2, nkv, hd, bp, wp).transpose(3, 4, 0, 1, 2))
        win_s.append(wout.reshape(nb, 2, nkv, hd, wl).transpose(0, 4, 1, 2, 3))
        st_p.append(s_p[None])
        st_s.append(s_s)

        xp = _ffn(xp, row(norm_pre3[l]), row(norm_post3[l]), *ff2)
        xs = _ffn(xs, row(norm_pre3[l]), row(norm_post3[l]), *ff2)
    return (xp.reshape(bp, tp, d), xs.reshape(nb, tn, d), jnp.stack(kv_p), jnp.stack(kv_s),
            jnp.stack(win_p), jnp.stack(win_s), jnp.stack(st_p), jnp.stack(st_s))
```

```python
import functools
import itertools

import numpy as np
import jax
import jax.numpy as jnp
from jax import lax
from jax.experimental import pallas as pl
from jax.experimental.pallas import tpu as pltpu

F32 = jnp.float32
BF16 = jnp.bfloat16

EPS = 1e-6
HG_HEADS = 8
HG_D = 128
NSA_HEADS = 16
NSA_KV_HEADS = 2
NSA_GROUP = NSA_HEADS // NSA_KV_HEADS
HEAD_DIM = 64
CMP_BLOCK = 32
CMP_STRIDE = 16
SLC_BLOCK = 64
SLC_SHIFT = 6
SLC_TOP = 16
WINDOW = 512
Q_BLOCK = 128
FORCE_SCORE = 1.0e4
NEG = -1.0e30
NEG_PICKED = -3.0e38
M_INIT = -1.0e20
LOG2E = float(np.log2(np.e))

LANE = 128
VMEM_LIMIT = 56 * 1024 * 1024

W_NSAQ, W_KV, W_NGATE, W_GA = 4096, 5120, 5888, 5936
N_NGATE = 3 * NSA_HEADS
KVW = NSA_KV_HEADS * HEAD_DIM
C_HGQ, C_HGF, C_HGI, C_HGG = 0, 1024, 2048, 3072
C_GA, C_GB = 4096, 6144
C_CMP = 8192
C_NGATE = 8448
C_Q = 8704
D_MAIN = 9728
Q_SCALE = HEAD_DIM ** -0.5 * LOG2E
R_KSLC, R_VSLC, R_KWIN, R_VWIN = 2 * KVW, 3 * KVW, 4 * KVW, 5 * KVW
N_KV_ROWS = 6 * KVW


def _cparams(sem):
    return pltpu.CompilerParams(dimension_semantics=sem, vmem_limit_bytes=VMEM_LIMIT)


def _rms(x, g):
    return x * lax.rsqrt(jnp.mean(x * x, axis=-1, keepdims=True) + EPS) * g


def _silu(x):
    return x * jax.nn.sigmoid(x)


def _dot(a, b):
    return jnp.dot(a, b, preferred_element_type=F32)


def _dot_nt(a, b):
    return lax.dot_general(a, b, (((1,), (1,)), ((), ())), preferred_element_type=F32)


def _dot_tn(a, b):
    return lax.dot_general(a, b, (((0,), (0,)), ((), ())), preferred_element_type=F32)


def _split3(x):
    hi = x.astype(BF16)
    r = x - hi.astype(F32)
    mid = r.astype(BF16)
    lo = (r - mid.astype(F32)).astype(BF16)
    return hi, mid, lo


def _split2(x):
    hi = x.astype(BF16)
    return hi, (x - hi.astype(F32)).astype(BF16)


def _ffn_body(x_ref, npre_ref, npost_ref, wg_ref, wu_ref, wd_ref, wgr_ref, wur_ref, wdr_ref, o_ref, h_ref, acc_ref):
    f = pl.program_id(1)

    @pl.when(f == 0)
    def _():
        h_ref[...] = _rms(x_ref[...], npre_ref[...]).astype(BF16)
        acc_ref[...] = jnp.zeros_like(acc_ref)

    def partial_down(wg, wu, wd):
        h = h_ref[...]
        a = _silu(_dot(h, wg[...])) * _dot(h, wu[...])
        return _dot(a.astype(BF16), wd[...])

    acc_ref[...] += partial_down(wg_ref, wu_ref, wd_ref)

    @pl.when(f == pl.num_programs(1) - 1)
    def _():
        y = acc_ref[...] + partial_down(wgr_ref, wur_ref, wdr_ref)
        o_ref[...] = x_ref[...] + 0.5 * _rms(y, npost_ref[...])


def _ffn(x, npre, npost, wg, wu, wd, wgr, wur, wdr, *, tm=512, tf=512):
    n, d = x.shape
    dff = wg.shape[1]
    rem = wgr.shape[1]
    assert rem > 0 and (dff - rem) % tf == 0 and rem % LANE == 0
    tm = min(tm, n)
    return pl.pallas_call(
        _ffn_body,
        grid=(n // tm, (dff - rem) // tf),
        in_specs=[
            pl.BlockSpec((tm, d), lambda i, f: (i, 0)),
            pl.BlockSpec((1, d), lambda i, f: (0, 0)),
            pl.BlockSpec((1, d), lambda i, f: (0, 0)),
            pl.BlockSpec((d, tf), lambda i, f: (0, f)),
            pl.BlockSpec((d, tf), lambda i, f: (0, f)),
            pl.BlockSpec((tf, d), lambda i, f: (f, 0)),
            pl.BlockSpec((d, rem), lambda i, f: (0, 0)),
            pl.BlockSpec((d, rem), lambda i, f: (0, 0)),
            pl.BlockSpec((rem, d), lambda i, f: (0, 0)),
        ],
        out_specs=pl.BlockSpec((tm, d), lambda i, f: (i, 0)),
        out_shape=jax.ShapeDtypeStruct((n, d), F32),
        scratch_shapes=[pltpu.VMEM((tm, d), BF16), pltpu.VMEM((tm, d), F32)],
        compiler_params=_cparams(("parallel", "arbitrary")),
        name="ffn",
    )(x, npre, npost, wg, wu, wd, wgr, wur, wdr)


def _inproj_body(x_ref, g_ref, wt_ref, o_ref, h_ref):
    @pl.when(pl.program_id(1) == 0)
    def _():
        h_ref[...] = _rms(x_ref[...], g_ref[...]).astype(BF16)

    o_ref[...] = _dot_nt(h_ref[...], wt_ref[...])


def _inproj(x, g, wt, *, tm=512, tn=512):
    n, d = x.shape
    dout = wt.shape[0]
    tm = min(tm, n)
    return pl.pallas_call(
        _inproj_body,
        grid=(n // tm, dout // tn),
        in_specs=[
            pl.BlockSpec((tm, d), lambda i, j: (i, 0)),
            pl.BlockSpec((1, d), lambda i, j: (0, 0)),
            pl.BlockSpec((tn, d), lambda i, j: (j, 0)),
        ],
        out_specs=pl.BlockSpec((tm, tn), lambda i, j: (i, j)),
        out_shape=jax.ShapeDtypeStruct((n, dout), F32),
        scratch_shapes=[pltpu.VMEM((tm, d), BF16)],
        compiler_params=_cparams(("parallel", "arbitrary")),
        name="inproj",
    )(x, g, wt)


def _inproj_t_body(x_ref, g_ref, wt_ref, o_ref, ob_ref):
    y = _dot_nt(wt_ref[...], _rms(x_ref[...], g_ref[...]).astype(BF16))
    o_ref[...] = y
    ob_ref[...] = y.astype(BF16)


def _inproj_t(x, g, wt, *, tm=512):
    n, d = x.shape
    dout = wt.shape[0]
    tm = min(tm, n)
    return pl.pallas_call(
        _inproj_t_body,
        grid=(n // tm,),
        in_specs=[
            pl.BlockSpec((tm, d), lambda i: (i, 0)),
            pl.BlockSpec((1, d), lambda i: (0, 0)),
            pl.BlockSpec((dout, d), lambda i: (0, 0)),
        ],
        out_specs=[pl.BlockSpec((dout, tm), lambda i: (0, i)), pl.BlockSpec((dout, tm), lambda i: (0, i))],
        out_shape=[jax.ShapeDtypeStruct((dout, n), F32), jax.ShapeDtypeStruct((dout, n), BF16)],
        compiler_params=_cparams(("parallel",)),
        name="inproj_t",
    )(x, g, wt)


def _merge_body(ohg_ref, on0_ref, on1_ref, ga_ref, gb_ref, wh_ref, wn_ref, y_ref):
    onsa = jnp.concatenate([on0_ref[...], on1_ref[...]], axis=-1).astype(BF16)
    yh = _dot(ohg_ref[...].astype(BF16), wh_ref[...])
    yn = _dot(onsa, wn_ref[...])
    y = jax.nn.sigmoid(ga_ref[...]) * yh + jax.nn.sigmoid(gb_ref[...]) * yn
    y_ref[...] = y.astype(BF16)


def _merge(ohg, on0, on1, main, wh, wn, *, tm=512):
    n = ohg.shape[0]
    d = wh.shape[1]
    tm = min(tm, n)
    return pl.pallas_call(
        _merge_body,
        grid=(n // tm,),
        in_specs=[
            pl.BlockSpec((tm, ohg.shape[1]), lambda i: (i, 0)),
            pl.BlockSpec((tm, on0.shape[1]), lambda i: (i, 0)),
            pl.BlockSpec((tm, on1.shape[1]), lambda i: (i, 0)),
            pl.BlockSpec((tm, d), lambda i: (i, C_GA // d)),
            pl.BlockSpec((tm, d), lambda i: (i, C_GB // d)),
            pl.BlockSpec(wh.shape, lambda i: (0, 0)),
            pl.BlockSpec(wn.shape, lambda i: (0, 0)),
        ],
        out_specs=pl.BlockSpec((tm, d), lambda i: (i, 0)),
        out_shape=jax.ShapeDtypeStruct((n, d), BF16),
        compiler_params=_cparams(("parallel",)),
        name="merge",
    )(ohg, on0, on1, main, main, wh, wn)


def _outproj_body(y_ref, x_ref, w_ref, g_ref, o_ref):
    o_ref[...] = x_ref[...] + _rms(_dot(y_ref[...], w_ref[...]), g_ref[...])


def _outproj(y, x, w, g, *, tm=512):
    n, d = x.shape
    tm = min(tm, n)
    return pl.pallas_call(
        _outproj_body,
        grid=(n // tm,),
        in_specs=[
            pl.BlockSpec((tm, d), lambda i: (i, 0)),
            pl.BlockSpec((tm, d), lambda i: (i, 0)),
            pl.BlockSpec(w.shape, lambda i: (0, 0)),
            pl.BlockSpec((1, d), lambda i: (0, 0)),
        ],
        out_specs=pl.BlockSpec((tm, d), lambda i: (i, 0)),
        out_shape=jax.ShapeDtypeStruct((n, d), F32),
        compiler_params=_cparams(("parallel",)),
        name="outproj",
    )(y, x, w, g)


def _hgrn_range_matrix(c):
    t = np.arange(c)[:, None]
    u = np.arange(c)[None, :]
    blocks = [u <= t, u > t]
    m = c // 2
    while m >= 1:
        p = (t // (2 * m)) * 2 * m + m - 1
        upper = (t // m) % 2 == 1
        blocks.append(np.where(upper, (u > p) & (u <= t), (u > t) & (u <= p)))
        m //= 2
    return np.concatenate(blocks, axis=0).astype(np.float32)


def _hgrn_gates(q_raw, f_raw, lb):
    f = lb + (1.0 - lb) * jax.nn.sigmoid(f_raw)
    return _silu(q_raw), jnp.log(f), 1.0 - f


def _hgrn_finish(o, g_raw, gnorm):
    return _rms(o, gnorm) * _silu(g_raw)


def _hgrn_prompt_body(q_ref, f_ref, i_ref, g_ref, lb_ref, gn_ref, rm_ref, o_ref, sfin_ref, s_ref, *, c, hp):
    @pl.when(pl.program_id(1) == 0)
    def _():
        s_ref[...] = jnp.zeros(s_ref.shape, F32)

    cols = [slice(j * HG_D, (j + 1) * HG_D) for j in range(hp)]
    res = [None] * hp
    heads = [_hgrn_chunk(q_ref[:, sl], f_ref[:, sl], i_ref[:, sl], lb_ref[:, sl], rm_ref[...], s_ref[j], c, res, j)
             for j, sl in enumerate(cols)]
    for _ in itertools.zip_longest(*heads):
        pass
    outs = [_hgrn_finish(o, g_ref[:, sl], gn_ref[...]) for (o, _), sl in zip(res, cols)]
    for j, sl in enumerate(cols):
        s_ref[j] = res[j][1]
        sfin_ref[j] = res[j][1]
        o_ref[:, sl] = outs[j]


def _hgrn_chunk(q_raw, f_raw, v, lb, rm, s, c, res, j):
    q, logf, k = _hgrn_gates(q_raw, f_raw, lb)
    vb = v.astype(BF16)
    hi, mid, lo = _split3(logf)
    r3 = _dot(rm, jnp.concatenate([hi, mid, lo], axis=1))
    e = r3[:, 0:HG_D] + r3[:, HG_D:2 * HG_D] + r3[:, 2 * HG_D:3 * HG_D]
    b = e[0:c]
    yield
    qe = (q * jnp.exp(b)).astype(BF16)
    row = lax.broadcasted_iota(jnp.int32, (c, c), 0)
    col = lax.broadcasted_iota(jnp.int32, (c, c), 1)
    x = row ^ col
    a = jnp.where(x == 0, _dot_nt(q.astype(BF16), k.astype(BF16)), 0.0)
    rowk = lax.broadcasted_iota(jnp.int32, (c, HG_D), 0)
    m = c // 2
    lvl = 0
    while m >= 1:
        w = jnp.exp(e[(2 + lvl) * c:(3 + lvl) * c])
        upper = (rowk & m) != 0
        ql = jnp.where(upper, q * w, 0.0).astype(BF16)
        kl = jnp.where(upper, 0.0, k * w).astype(BF16)
        a = a + jnp.where(x < 2 * m, _dot_nt(ql, kl), 0.0)
        m //= 2
        lvl += 1
        yield
    o = _dot(jnp.concatenate([qe, a.astype(BF16)], axis=1), jnp.concatenate([s.astype(BF16), vb], axis=0))
    kd = (k * jnp.exp(e[c:2 * c])).astype(BF16)
    ecol = jnp.transpose(jnp.broadcast_to(jnp.exp(b[c - 1:c, :]), (HG_D, HG_D)))
    res[j] = (o, ecol * s + _dot_tn(kd, vb))


def _hgrn_prompt(main, lb, gnorm, *, c=128, hp=8):
    t = main.shape[0]
    assert c == HG_D and HG_HEADS % hp == 0
    rm = jnp.asarray(_hgrn_range_matrix(c), BF16)
    w = hp * HG_D
    col = lambda base: (lambda h, ci: (ci, base // w + h))
    return pl.pallas_call(
        functools.partial(_hgrn_prompt_body, c=c, hp=hp),
        grid=(HG_HEADS // hp, t // c),
        in_specs=[
            pl.BlockSpec((c, w), col(C_HGQ)),
            pl.BlockSpec((c, w), col(C_HGF)),
            pl.BlockSpec((c, w), col(C_HGI)),
            pl.BlockSpec((c, w), col(C_HGG)),
            pl.BlockSpec((1, w), lambda h, ci: (0, h)),
            pl.BlockSpec((1, HG_D), lambda h, ci: (0, 0)),
            pl.BlockSpec(rm.shape, lambda h, ci: (0, 0)),
        ],
        out_specs=[
            pl.BlockSpec((c, w), lambda h, ci: (ci, h)),
            pl.BlockSpec((hp, HG_D, HG_D), lambda h, ci: (h, 0, 0)),
        ],
        out_shape=[
            jax.ShapeDtypeStruct((t, HG_HEADS * HG_D), F32),
            jax.ShapeDtypeStruct((HG_HEADS, HG_D, HG_D), F32),
        ],
        scratch_shapes=[pltpu.VMEM((hp, HG_D, HG_D), F32)],
        compiler_params=_cparams(("parallel", "arbitrary")),
        name="hgrn_prompt",
    )(main, main, main, main, lb, gnorm, rm)


def _pad16(x):
    return jnp.concatenate([x, jnp.zeros_like(x)], axis=0)


def _hgrn_sample_body(q_ref, f_ref, i_ref, g_ref, lb_ref, gn_ref, s_ref, o_ref, so_ref, *, tn):
    rows = 2 * tn
    row = lax.broadcasted_iota(jnp.int32, (rows, HG_D), 0)
    pos = row & (tn - 1)
    first = row < tn
    def head(h):
        sl = slice(h * HG_D, (h + 1) * HG_D)
        q, logf, k = _hgrn_gates(q_ref[:, sl], f_ref[:, sl], lb_ref[:, sl])
        v = i_ref[:, sl]
        b = logf
        d = 1
        while d < tn:
            b = b + jnp.where(pos >= d, pltpu.roll(b, d, 0), 0.0)
            d *= 2
        oi = jnp.sum(q * k, axis=-1, keepdims=True) * v
        for d in range(1, tn):
            ok = pos >= d
            w = jnp.exp(jnp.where(ok, b - pltpu.roll(b, d, 0), 0.0))
            a = jnp.sum(jnp.where(ok, q * pltpu.roll(k, d, 0) * w, 0.0), axis=-1, keepdims=True)
            oi = oi + a * pltpu.roll(v, d, 0)
            yield
        qe = _pad16(q * jnp.exp(b)).astype(BF16)
        blast = jnp.where(first, b[tn - 1:tn, :], b[rows - 1:rows, :])
        kd = k * jnp.exp(blast - b)
        v16 = _pad16(v).astype(BF16)
        o_inter = []
        for bi in range(2):
            s = s_ref[bi, h]
            o_inter.append(_dot(qe, s.astype(BF16))[0:rows])
            mine = first if bi == 0 else jnp.logical_not(first)
            kd16 = _pad16(jnp.where(mine, kd, 0.0)).astype(BF16)
            e_row = jnp.exp(b[(bi + 1) * tn - 1:(bi + 1) * tn, :])
            ecol = jnp.transpose(jnp.broadcast_to(e_row, (HG_D, HG_D)))
            so_ref[bi, h] = ecol * s + _dot_tn(kd16, v16)
            yield
        o = jnp.where(first, o_inter[0], o_inter[1]) + oi
        o_ref[:, sl] = _hgrn_finish(o, g_ref[:, sl], gn_ref[...])

    for _ in itertools.zip_longest(*[head(h) for h in range(HG_HEADS)]):
        pass


def _hgrn_sample(main, lb, gnorm, state, *, tn):
    n = main.shape[0]
    nb = state.shape[0]
    assert n == nb * tn and tn == 4 and nb % 2 == 0
    w = HG_HEADS * HG_D
    rows = 2 * tn
    col = lambda base: (lambda i: (i, base // w))
    return pl.pallas_call(
        functools.partial(_hgrn_sample_body, tn=tn),
        grid=(nb // 2,),
        in_specs=[
            pl.BlockSpec((rows, w), col(C_HGQ)),
            pl.BlockSpec((rows, w), col(C_HGF)),
            pl.BlockSpec((rows, w), col(C_HGI)),
            pl.BlockSpec((rows, w), col(C_HGG)),
            pl.BlockSpec((1, w), lambda i: (0, 0)),
            pl.BlockSpec((1, HG_D), lambda i: (0, 0)),
            pl.BlockSpec((2, HG_HEADS, HG_D, HG_D), lambda i: (i, 0, 0, 0)),
        ],
        out_specs=[
            pl.BlockSpec((rows, w), lambda i: (i, 0)),
            pl.BlockSpec((2, HG_HEADS, HG_D, HG_D), lambda i: (i, 0, 0, 0)),
        ],
        out_shape=[
            jax.ShapeDtypeStruct((n, w), F32),
            jax.ShapeDtypeStruct(state.shape, F32),
        ],
        compiler_params=_cparams(("parallel",)),
        name="hgrn_sample",
    )(main, main, main, main, lb, gnorm, state)


def _alibi_slopes():
    return np.power(2.0, -8.0 * np.arange(1, NSA_HEADS + 1) / NSA_HEADS).astype(np.float32)


def _gelu_tanh(x):
    return 0.5 * x * (1.0 + jnp.tanh(np.float32(np.sqrt(2.0 / np.pi)) * (x + 0.044715 * (x * x * x))))


def _overlap_matrix(ncp, nlanes):
    cs = np.arange(ncp)[:, None] * CMP_STRIDE
    ss = np.arange(nlanes)[None, :] * SLC_BLOCK
    return ((cs <= ss + SLC_BLOCK - 1) & (cs + CMP_BLOCK - 1 >= ss)).astype(np.float32)


def _expand_matrix(nlanes, nkeys):
    return (np.arange(nkeys)[None, :] // SLC_BLOCK == np.arange(nlanes)[:, None]).astype(np.float32)


def _np_split3(x):
    parts = []
    r = np.asarray(x, np.float64)
    for _ in range(3):
        p = r.astype(np.float32).astype(BF16).astype(np.float64)
        parts.append(p.astype(np.float32))
        r = r - p
    return parts


def _softmax2_rows(s, mask):
    m = jnp.max(s, axis=-1, keepdims=True)
    e = jnp.where(mask, jnp.exp2(s - m), 0.0)
    l = jnp.sum(e, axis=-1, keepdims=True)
    return e * (1.0 / jnp.maximum(l, 1e-30))


def _select_blocks(score, qblk, n_top):
    blk = lax.broadcasted_iota(jnp.int32, score.shape, 1)
    forced = (blk == 0) | (blk == qblk) | (blk == qblk - 1)
    valid = blk <= qblk
    work = jnp.where(valid, jnp.where(forced, FORCE_SCORE, score), NEG)
    nrow, nblk = score.shape
    if nrow < nblk:
        work = jnp.concatenate([work, jnp.full((nblk - nrow, nblk), NEG, F32)], axis=0)
    wt = jnp.transpose(work)
    blk_t = lax.broadcasted_iota(jnp.int32, wt.shape, 0)
    sel_t = jnp.zeros(wt.shape, F32)
    for _ in range(n_top):
        m = jnp.max(wt, axis=0, keepdims=True)
        first = jnp.min(jnp.where(wt == m, blk_t, nblk), axis=0, keepdims=True)
        pick = blk_t == first
        sel_t = jnp.where(pick, 1.0, sel_t)
        wt = jnp.where(pick, NEG_PICKED, wt)
    return jnp.where(valid, jnp.transpose(sel_t)[0:nrow], 0.0)


def _compress_body(g_ref, pos_ref, w1_ref, w2_ref, o_ref):
    g = g_ref[0]
    ylo = _dot((g + pos_ref[0, 0]).astype(BF16), w1_ref[0, 0])
    yhi = _dot((g + pos_ref[0, 1]).astype(BF16), w1_ref[0, 1])
    pre = ylo + pltpu.roll(yhi, g.shape[0] - 1, 0)
    o_ref[0] = _dot(_gelu_tanh(pre).astype(BF16), w2_ref[0]).astype(BF16)


def _compress_prompt(groups, pos, w1, w2):
    na, ng, gw = groups.shape
    hid = w1.shape[-1]
    kv = lambda a: a // NSA_KV_HEADS
    return pl.pallas_call(
        _compress_body,
        grid=(na,),
        in_specs=[
            pl.BlockSpec((1, ng, gw), lambda a: (a, 0, 0)),
            pl.BlockSpec((1, 2, 1, gw), lambda a: (kv(a), 0, 0, 0)),
            pl.BlockSpec((1, 2, gw, hid), lambda a: (kv(a), 0, 0, 0)),
            pl.BlockSpec((1, hid, HEAD_DIM), lambda a: (kv(a), 0, 0)),
        ],
        out_specs=pl.BlockSpec((1, ng, HEAD_DIM), lambda a: (a, 0, 0)),
        out_shape=jax.ShapeDtypeStruct((na, ng, HEAD_DIM), BF16),
        compiler_params=_cparams(("parallel",)),
        name="compress_prompt",
    )(groups, pos, w1, w2)


SLC_CHUNK = 512
CHUNK_BLOCK_SHIFT = 3
N_SLOPE_COLS = 6
STRIP = Q_BLOCK


def _nsa_prompt_body(q_ref, gate_ref, sl_ref, kca_ref, vc_ref, ka_ref, vsa_ref, kwa_ref, vwa_ref, ov_ref,
                     o_ref, s_ref, p_ref, bias_ref, psum_ref, sw_ref, pw_ref, bw_ref, s2_ref, mx_ref, bias2_ref,
                     m_ref, alpha_ref, acc_ref, todo_ref, *, n, n_top):
    g_n = NSA_GROUP
    qb = Q_BLOCK
    nrow = g_n * qb
    nstrip = nrow // STRIP
    i = pl.program_id(0)
    start = i * qb
    q = (q_ref[...] * Q_SCALE).astype(BF16)
    qs = jnp.concatenate([q[:, HEAD_DIM * g:HEAD_DIM * (g + 1)] for g in range(g_n)], axis=0)
    qc = jnp.concatenate([qs, sl_ref[...]], axis=1)

    def strip_rows(k):
        r0 = k * STRIP
        t0 = r0 % qb
        return slice(r0, r0 + STRIP), slice(t0, t0 + STRIP)

    def softmax_strip(s):
        m = jnp.maximum(jnp.max(s, axis=-1, keepdims=True), M_INIT)
        e = jnp.exp2(s - m)
        return e * (1.0 / jnp.maximum(jnp.sum(e, axis=-1, keepdims=True), 1e-30))

    ncp = kca_ref.shape[1]
    t_c = lax.broadcasted_iota(jnp.int32, (qb, ncp), 0)
    c_c = lax.broadcasted_iota(jnp.int32, (qb, ncp), 1)
    bias_ref[:, 0:ncp] = jnp.where(t_c - CMP_STRIDE * c_c + (start - (CMP_BLOCK - 1)) >= 0, 0.0, NEG)
    s_ref[:, 0:ncp] = _dot(qc, kca_ref[...])
    psum_ref[...] = jnp.zeros(psum_ref.shape, F32)

    def cmp_strip(k):
        rows, trows = strip_rows(k)
        p = softmax_strip(s_ref[rows, 0:ncp] + bias_ref[trows, 0:ncp])
        psum_ref[trows, :] += p
        p_ref[rows, 0:ncp] = p.astype(BF16)

    nw = WINDOW + qb
    t_w = lax.broadcasted_iota(jnp.int32, (qb, nw), 0)
    w_w = lax.broadcasted_iota(jnp.int32, (qb, nw), 1)
    dist_w = t_w + WINDOW - w_w
    mask_w = (dist_w >= 0) & (dist_w < WINDOW) & (w_w >= WINDOW - start)
    bw_ref[...] = jnp.where(mask_w, 0.0, NEG)
    w0 = pl.multiple_of(start, qb)
    sw_ref[...] = _dot(qc, kwa_ref[:, pl.ds(w0, nw)])

    def win_strip(k):
        rows, trows = strip_rows(k)
        s = sw_ref[rows, :] + bw_ref[trows, :]
        m = jnp.max(s, axis=-1, keepdims=True)
        pw_ref[rows, :] = jnp.exp2(s - m).astype(BF16)

    for k in range(nstrip):
        cmp_strip(k)
        win_strip(k)

    ph, plo = _split2(psum_ref[...])
    ov = ov_ref[...]
    score = _dot(ph, ov) + _dot(plo, ov)
    t_b = lax.broadcasted_iota(jnp.int32, score.shape, 0)
    qblk = jnp.right_shift(start + t_b, SLC_SHIFT)
    sel = _select_blocks(score, qblk, n_top)
    o_cmp = _dot(p_ref[:, 0:ncp], vc_ref[...])
    win = _dot_nt(pw_ref[...], vwa_ref[:, pl.ds(w0, nw)])

    ck = SLC_CHUNK
    notsel = (1.0 - sel).astype(BF16)
    qa = jnp.concatenate([qc, jnp.concatenate([notsel] * g_n, axis=0)], axis=1)
    m_ref[...] = jnp.full(m_ref.shape, M_INIT, F32)
    acc_ref[...] = jnp.zeros(acc_ref.shape, F32)
    c_last = lax.div(start, ck)
    t_s = lax.broadcasted_iota(jnp.int32, (qb, ck), 0)
    s_s = lax.broadcasted_iota(jnp.int32, (qb, ck), 1)
    bias2_ref[0] = jnp.zeros((qb, ck), F32)
    bias2_ref[1] = jnp.where(t_s - s_s + (start - c_last * ck) >= 0, 0.0, NEG)

    blk_any = jnp.max(sel, axis=0, keepdims=True)
    lane = lax.broadcasted_iota(jnp.int32, blk_any.shape, 1)
    cnt = jnp.int32(0)
    for c in range(todo_ref.shape[0] - 2):
        used = jnp.max(jnp.where(jnp.right_shift(lane, CHUNK_BLOCK_SHIFT) == c, blk_any, 0.0)) > 0.5
        todo_ref[cnt] = c
        cnt = cnt + jnp.logical_and(used, c < c_last).astype(jnp.int32)
    todo_ref[cnt] = c_last
    todo_ref[cnt + 1] = c_last

    def scores(c, slot):
        k0 = pl.multiple_of(c * ck, ck)
        last = (c == c_last).astype(jnp.int32)
        s = _dot(qa, ka_ref[:, pl.ds(k0, ck)])
        for k in range(nstrip):
            rows, _ = strip_rows(k)
            sk = s[rows] + bias2_ref[last]
            s2_ref[slot, rows, :] = sk
            mx_ref[slot, rows, :] = jnp.broadcast_to(jnp.max(sk, axis=-1, keepdims=True), (qb, LANE))

    def accumulate(c, slot):
        k0 = pl.multiple_of(c * ck, ck)
        for k in range(nstrip):
            rows, _ = strip_rows(k)
            m_old = m_ref[rows, :]
            m_new = jnp.maximum(m_old, mx_ref[slot, rows, :])
            alpha_ref[rows, :] = jnp.exp2(m_old - m_new)
            m_ref[rows, :] = m_new
            p_ref[rows, 0:ck] = jnp.exp2(s2_ref[slot, rows, :] - jnp.tile(m_new, (1, ck // LANE))).astype(BF16)
        acc_ref[...] = alpha_ref[...] * acc_ref[...] + _dot_nt(p_ref[:, 0:ck], vsa_ref[:, pl.ds(k0, ck)])

    scores(todo_ref[0], 0)
    ntrip = cnt + 1

    def visit_pair(i, carry):
        j = 2 * i
        scores(todo_ref[j + 1], 1)
        accumulate(todo_ref[j], 0)
        scores(todo_ref[j + 2], 0)
        accumulate(todo_ref[j + 1], 1)
        return carry

    lax.fori_loop(0, ntrip // 2, visit_pair, 0)

    @pl.when(ntrip % 2 == 1)
    def _():
        accumulate(todo_ref[ntrip - 1], 0)

    gate = jax.nn.sigmoid(gate_ref[...])
    hd = HEAD_DIM

    def normalised(x):
        return x[:, 0:hd] * (1.0 / jnp.maximum(x[:, hd:hd + 1], 1e-30))

    outs = []
    for g in range(g_n):
        c0 = 3 * (n * g_n + g)
        rows = slice(g * qb, (g + 1) * qb)
        outs.append(gate[:, c0:c0 + 1] * o_cmp[rows] + gate[:, c0 + 1:c0 + 2] * normalised(acc_ref[rows, :])
                    + gate[:, c0 + 2:c0 + 3] * normalised(win[rows]))
    o_ref[...] = jnp.concatenate(outs, axis=-1)


def _nsa_prompt(main, kvtb, kwtp, kct, vc, n):
    t = main.shape[0]
    ncp = kct.shape[1]
    ns = t // SLC_BLOCK
    assert ns <= LANE and t % SLC_CHUNK == 0
    g_n, hd = NSA_GROUP, HEAD_DIM
    nrow = g_n * Q_BLOCK
    nw = WINDOW + Q_BLOCK
    ov = jnp.asarray(_overlap_matrix(ncp, LANE), BF16)
    sl = np.zeros((g_n, Q_BLOCK, hd), np.float32)
    parts = _np_split3(_alibi_slopes()[n * g_n:(n + 1) * g_n].astype(np.float64) * LOG2E)
    for j in range(N_SLOPE_COLS):
        sl[:, :, j] = parts[j % 3][:, None]
    sl = jnp.asarray(sl.reshape(nrow, hd), BF16)

    def with_pos_rows(kt, pos):
        rows = np.zeros((hd, pos.shape[0]), np.float32)
        rows[0:3] = (pos // SLC_BLOCK) * SLC_BLOCK
        rows[3:6] = pos % SLC_BLOCK
        return jnp.concatenate([kt, jnp.asarray(rows, BF16)], axis=0)

    ka = jnp.concatenate([with_pos_rows(kvtb[R_KSLC + n * hd:R_KSLC + (n + 1) * hd], np.arange(t)),
                          jnp.asarray(NEG * _expand_matrix(LANE, t), BF16)], axis=0)
    kca = with_pos_rows(kct, np.arange(ncp) * CMP_STRIDE + (CMP_BLOCK - 1))
    kwa = with_pos_rows(kwtp[n * hd:(n + 1) * hd], np.arange(kwtp.shape[1]))

    def with_ones_row(vt):
        rows = np.zeros((hd, vt.shape[1]), np.float32)
        rows[0] = 1.0
        return jnp.concatenate([vt, jnp.asarray(rows, BF16)], axis=0)

    vsa = with_ones_row(kvtb[R_VSLC + n * hd:R_VSLC + (n + 1) * hd])
    vwa = with_ones_row(kwtp[KVW + n * hd:KVW + (n + 1) * hd])
    wq = g_n * hd
    whole = lambda a: pl.BlockSpec(a.shape, lambda i: (0,) * a.ndim)
    return pl.pallas_call(
        functools.partial(_nsa_prompt_body, n=n, n_top=min(SLC_TOP, ns)),
        grid=(t // Q_BLOCK,),
        in_specs=[
            pl.BlockSpec((Q_BLOCK, wq), lambda i: (i, C_Q // wq + n)),
            pl.BlockSpec((Q_BLOCK, LANE), lambda i: (i, C_NGATE // LANE)),
            whole(sl), whole(kca), whole(vc), whole(ka), whole(vsa), whole(kwa), whole(vwa), whole(ov),
        ],
        out_specs=pl.BlockSpec((Q_BLOCK, wq), lambda i: (i, 0)),
        out_shape=jax.ShapeDtypeStruct((t, wq), F32),
        scratch_shapes=[
            pltpu.VMEM((nrow, ncp), F32),
            pltpu.VMEM((nrow, max(ncp, SLC_CHUNK)), BF16),
            pltpu.VMEM((Q_BLOCK, ncp), F32),
            pltpu.VMEM((Q_BLOCK, ncp), F32),
            pltpu.VMEM((nrow, nw), F32),
            pltpu.VMEM((nrow, nw), BF16),
            pltpu.VMEM((Q_BLOCK, nw), F32),
            pltpu.VMEM((2, nrow, SLC_CHUNK), F32),
            pltpu.VMEM((2, nrow, LANE), F32),
            pltpu.VMEM((2, Q_BLOCK, SLC_CHUNK), F32),
            pltpu.VMEM((nrow, LANE), F32),
            pltpu.VMEM((nrow, LANE), F32),
            pltpu.VMEM((nrow, 2 * hd), F32),
            pltpu.SMEM((t // SLC_CHUNK + 2,), jnp.int32),
        ],
        compiler_params=_cparams(("parallel",)),
        name=f"nsa_prompt_{n}",
    )(main, main, sl, kca, vc, ka, vsa, kwa, vwa, ov)


N_KINDS = 4
SAMPLE_PER_STEP = 2


def _nsa_sample_body(pt_ref, cache_ref, q_ref, new_ref, win_ref, gate_ref, slope_ref,
                     posk_ref, w1k_ref, w2k_ref, posv_ref, w1v_ref, w2v_ref, ov_ref, ex_ref, same_ref,
                     o_ref, wout_ref, buf_ref, rows_ref, newp_ref, sem, *, tn, n_top):
    b = pl.program_id(0)
    nsteps = pl.num_programs(0)
    nbs, npages = buf_ref.shape[1], buf_ref.shape[2]
    slot = b % 2

    def page_copy(step, sl, e, j):
        return pltpu.make_async_copy(cache_ref.at[pt_ref[step * nbs + e, j]], buf_ref.at[sl, e, j], sem.at[sl])

    def all_pages(step, sl, fn):
        for e in range(nbs):
            for j in range(npages):
                fn(page_copy(step, sl, e, j))

    @pl.when(b == 0)
    def _():
        all_pages(0, 0, lambda c: c.start())
        newp_ref[...] = jnp.zeros(newp_ref.shape, F32)

    @pl.when(b + 1 < nsteps)
    def _():
        all_pages(b + 1, 1 - slot, lambda c: c.start())

    all_pages(b, slot, lambda c: c.wait())
    ones = [pl.ds(e, 1) for e in range(nbs)]
    elements = [
        _nsa_sample_one(buf_ref.at[slot, e], q_ref.at[one], new_ref.at[one], win_ref.at[one], gate_ref.at[one],
                        slope_ref, posk_ref, w1k_ref, w2k_ref, posv_ref, w1v_ref, w2v_ref, ov_ref, ex_ref,
                        same_ref, o_ref.at[one], wout_ref.at[one], rows_ref.at[e], newp_ref.at[e],
                        tn=tn, n_top=n_top)
        for e, one in enumerate(ones)]
    for _ in itertools.zip_longest(*elements):
        pass


def _nsa_sample_one(pg, q_ref, new_ref, win_ref, gate_ref, slope_ref,
                    posk_ref, w1k_ref, w2k_ref, posv_ref, w1v_ref, w2v_ref, ov_ref, ex_ref, same_ref,
                    o_ref, wout_ref, rows_ref, newp_ref, *, tn, n_top):
    npages, page = pg.shape[0], pg.shape[3]
    past_len = npages * page

    for a in range(4):
        newp_ref[a, :, 0:tn] = new_ref[0, (2 + a) * KVW:(3 + a) * KVW, :]

    ncp = past_len // CMP_STRIDE
    for a in range(2):
        for j in range(npages):
            rows_ref[a, j * page:(j + 1) * page, :] = jnp.transpose(pg[j, a])
        rows_ref[a, past_len:past_len + LANE, :] = jnp.zeros((LANE, KVW), F32)

    def compress(a, pos_ref, w1_ref, w2_ref):
        lo, hi = [], []
        for r in range(CMP_STRIDE):
            x = rows_ref[a, pl.ds(r, ncp + 8, stride=CMP_STRIDE), :]
            lo.append((x[0:ncp] + pos_ref[r]).astype(BF16))
            hi.append((x[1:ncp + 1] + pos_ref[CMP_STRIDE + r]).astype(BF16))
        x = jnp.concatenate(lo + hi, axis=1)
        return _dot(_gelu_tanh(_dot(x, w1_ref[...])).astype(BF16), w2_ref[...]).astype(BF16)

    yield
    kc = compress(0, posk_ref, w1k_ref, w2k_ref)
    yield
    vc = compress(1, posv_ref, w1v_ref, w2v_ref)
    yield

    rows = q_ref.shape[1]
    q = q_ref[0]
    slope = slope_ref[:, 0:1]
    r_i = lax.broadcasted_iota(jnp.int32, (rows, 1), 0)
    t_i = r_i & (tn - 1)
    qpos = past_len + t_i
    low = r_i < rows // 2

    def pick_head(o):
        return jnp.where(low, o[:, 0:HEAD_DIM], o[:, HEAD_DIM:KVW])

    cpos = CMP_STRIDE * lax.broadcasted_iota(jnp.int32, (rows, ncp), 1) + (CMP_BLOCK - 1)
    dist_c = (qpos - cpos).astype(F32)
    mask_c = dist_c >= 0.0
    s = jnp.where(mask_c, _dot_nt(q, kc) - slope * dist_c, NEG)
    p_c = _softmax2_rows(s, mask_c)
    o_cmp = pick_head(_dot(p_c.astype(BF16), vc))

    yield
    same = same_ref[...]
    ph, plo = _split2(p_c)
    psum = _dot(same, ph) + _dot(same, plo)
    ph, plo = _split2(psum)
    ov = ov_ref[...]
    score = _dot(ph, ov) + _dot(plo, ov)
    yield
    sel = _select_blocks(score, jnp.right_shift(qpos, SLC_SHIFT), n_top)
    yield

    u_i = lax.broadcasted_iota(jnp.int32, (rows, LANE), 1)
    new_ok = u_i <= t_i
    new_blk = past_len // SLC_BLOCK
    keep = _dot(sel.astype(BF16), ex_ref[...]) > 0.5
    spos = lax.broadcasted_iota(jnp.int32, (rows, past_len), 1)
    ks = jnp.concatenate([pg[j, 2].astype(BF16) for j in range(npages)], axis=1)
    vs = jnp.concatenate([pg[j, 3].astype(BF16) for j in range(npages)], axis=1)
    s_past = _dot(q, ks)
    s_past = jnp.where(keep, s_past - slope * (qpos - spos).astype(F32), NEG)
    keep_new = new_ok & (sel[:, new_blk:new_blk + 1] > 0.5)
    s_new = jnp.where(keep_new, _dot(q, newp_ref[0].astype(BF16)) - slope * (t_i - u_i).astype(F32), NEG)
    m = jnp.maximum(jnp.max(s_past, axis=-1, keepdims=True), jnp.max(s_new, axis=-1, keepdims=True))
    e_past = jnp.where(keep, jnp.exp2(s_past - m), 0.0)
    e_new = jnp.where(keep_new, jnp.exp2(s_new - m), 0.0)
    l = jnp.sum(e_past, axis=-1, keepdims=True) + jnp.sum(e_new, axis=-1, keepdims=True)
    o_sel = _dot_nt(e_new.astype(BF16), newp_ref[1].astype(BF16)) + _dot_nt(e_past.astype(BF16), vs)
    o_sel = pick_head(o_sel * (1.0 / jnp.maximum(l, 1e-30)))

    yield
    wl = win_ref.shape[3]
    w_i = lax.broadcasted_iota(jnp.int32, (rows, wl), 1)
    dist_wi = (wl + t_i) - w_i
    mask_w = (dist_wi < WINDOW) & (w_i >= wl - past_len)
    s_win = jnp.where(mask_w, _dot(q, win_ref[0, 0].astype(BF16)) - slope * dist_wi.astype(F32), NEG)
    s_new = jnp.where(new_ok, _dot(q, newp_ref[2].astype(BF16)) - slope * (t_i - u_i).astype(F32), NEG)
    m = jnp.maximum(jnp.max(s_win, axis=-1, keepdims=True), jnp.max(s_new, axis=-1, keepdims=True))
    e_win = jnp.where(mask_w, jnp.exp2(s_win - m), 0.0)
    e_new = jnp.where(new_ok, jnp.exp2(s_new - m), 0.0)
    l = jnp.sum(e_win, axis=-1, keepdims=True) + jnp.sum(e_new, axis=-1, keepdims=True)
    o_win = (_dot_nt(e_win.astype(BF16), win_ref[0, 1].astype(BF16))
             + _dot_nt(e_new.astype(BF16), newp_ref[3].astype(BF16)))
    o_win = pick_head(o_win * (1.0 / jnp.maximum(l, 1e-30)))

    gate = jax.nn.sigmoid(gate_ref[0])
    o_ref[0] = gate[:, 0:1] * o_cmp + gate[:, 1:2] * o_sel + gate[:, 2:3] * o_win

    lane_w = lax.broadcasted_iota(jnp.int32, (KVW, LANE), 1)
    for a in range(2):
        shifted = pltpu.roll(win_ref[0, a], wl - tn, 1)
        tail = pltpu.roll(newp_ref[2 + a], LANE - tn, 1)
        wout_ref[0, a, :, 0:wl - LANE] = shifted[:, 0:wl - LANE]
        wout_ref[0, a, :, wl - LANE:wl] = jnp.where(lane_w >= LANE - tn, tail, shifted[:, wl - LANE:wl])


def _nsa_sample(page_table, cache, qrows, newt, win, gates, cmpk, cmpv, *, tn):
    nb, npages = page_table.shape
    page = cache.shape[3]
    past_len = npages * page
    wl = win.shape[3]
    rows = qrows.shape[1]
    assert tn == 4 and page == LANE and wl % LANE == 0 and wl > LANE and past_len >= wl
    ncp = past_len // CMP_STRIDE
    ns = -(-(past_len + tn) // SLC_BLOCK)
    assert ncp <= LANE and ns <= LANE
    r = np.arange(rows)
    h = r // tn
    slope = jnp.asarray(np.broadcast_to((_alibi_slopes()[h].astype(np.float64) * LOG2E)[:, None], (rows, LANE)), F32)
    nt = h // NSA_GROUP * tn + r % tn
    same = jnp.asarray((nt[:, None] == nt[None, :]).astype(np.float32), BF16)
    ov = jnp.asarray(_overlap_matrix(ncp, LANE), BF16)
    ex = jnp.asarray(_expand_matrix(LANE, past_len), BF16)
    posk, w1k, w2k = cmpk
    posv, w1v, w2v = cmpv
    nbs = SAMPLE_PER_STEP
    assert nb % nbs == 0
    const = lambda a: pl.BlockSpec(a.shape, lambda b, pt: (0,) * a.ndim)
    per_b = lambda a: pl.BlockSpec((nbs,) + a.shape[1:], lambda b, pt: (b,) + (0,) * (a.ndim - 1))
    grid_spec = pltpu.PrefetchScalarGridSpec(
        num_scalar_prefetch=1,
        grid=(nb // nbs,),
        in_specs=[
            pl.BlockSpec(memory_space=pl.ANY),
            per_b(qrows), per_b(newt), per_b(win), per_b(gates), const(slope),
            const(posk), const(w1k), const(w2k), const(posv), const(w1v), const(w2v),
            const(ov), const(ex), const(same),
        ],
        out_specs=[pl.BlockSpec((nbs, rows, HEAD_DIM), lambda b, pt: (b, 0, 0)), per_b(win)],
        scratch_shapes=[
            pltpu.VMEM((2, nbs, npages, N_KINDS, KVW, page), F32),
            pltpu.VMEM((nbs, 2, past_len + LANE, KVW), F32),
            pltpu.VMEM((nbs, 4, KVW, LANE), F32),
            pltpu.SemaphoreType.DMA((2,)),
        ],
    )
    return pl.pallas_call(
        functools.partial(_nsa_sample_body, tn=tn, n_top=min(SLC_TOP, ns)),
        grid_spec=grid_spec,
        out_shape=[jax.ShapeDtypeStruct((nb, rows, HEAD_DIM), F32), jax.ShapeDtypeStruct(win.shape, F32)],
        compiler_params=_cparams(("arbitrary",)),
        name="nsa_sample",
    )(page_table, cache, qrows, newt, win, gates, slope, posk, w1k, w2k, posv, w1v, w2v, ov, ex, same)


def _prep_ffn(wg, wu, wd, tf=512):
    wg, wu, wd = wg.astype(BF16), wu.astype(BF16), wd.astype(BF16)
    main = wg.shape[1] // tf * tf
    return wg, wu, wd, wg[:, main:], wu[:, main:], wd[main:]


def _prep_w_in(w_in):
    wt = w_in.T
    d = wt.shape[1]
    rows = [wt[:W_NSAQ], wt[W_GA:], wt[W_KV:W_KV + 2 * KVW], wt[W_NGATE:W_GA]]
    used = sum(r.shape[0] for r in rows)
    w_main = jnp.concatenate(rows + [jnp.zeros((C_Q - used, d), wt.dtype), wt[W_NSAQ:W_KV]], axis=0).astype(BF16)
    return w_main, wt[W_KV:W_NGATE].astype(BF16)


def _prep_cmp_prompt(pos_k, w1_k, w2_k, pos_v, w1_v, w2_v):
    half = CMP_STRIDE * HEAD_DIM
    pos = jnp.stack([pos_k.reshape(2, 1, half), pos_v.reshape(2, 1, half)])
    w1 = jnp.stack([w1_k.reshape(2, half, -1), w1_v.reshape(2, half, -1)]).astype(BF16)
    w2 = jnp.stack([w2_k, w2_v]).astype(BF16)
    return pos, w1, w2


def _prep_cmp_sample(pos, w1, w2):
    hid = w1.shape[1]
    w1p = w1.reshape(CMP_BLOCK, HEAD_DIM, hid)
    z1 = jnp.zeros_like(w1p)
    w1b = jnp.concatenate([jnp.concatenate([w1p, z1], axis=2), jnp.concatenate([z1, w1p], axis=2)], axis=1)
    z2 = jnp.zeros_like(w2)
    w2b = jnp.concatenate([jnp.concatenate([w2, z2], axis=1), jnp.concatenate([z2, w2], axis=1)], axis=0)
    posb = jnp.concatenate([pos, pos], axis=1)[:, None, :]
    return posb, w1b.reshape(CMP_BLOCK * KVW, 2 * hid).astype(BF16), w2b.astype(BF16)


def _project(x, g, w_main, w_kvt):
    main = _inproj(x, g, w_main, tm=1024)
    kvt, kvtb = _inproj_t(x, g, w_kvt)
    return main, kvt, kvtb


def _mix_prompt(main, kvtb, lb, gnorm, cmp_w):
    t = main.shape[0]
    o_hg, s_fin = _hgrn_prompt(main, lb, gnorm)
    nkv, hd = NSA_KV_HEADS, HEAD_DIM
    groups = main[:, C_CMP:C_CMP + 2 * KVW].reshape(t // CMP_STRIDE, CMP_STRIDE, 2, nkv, hd)
    groups = groups.transpose(2, 3, 0, 1, 4).reshape(2 * nkv, t // CMP_STRIDE, CMP_STRIDE * hd)
    cmp = _compress_prompt(groups, *cmp_w)
    kct = cmp[:nkv].transpose(0, 2, 1)
    vc = cmp[nkv:]
    kwtp = jnp.pad(kvtb[R_KWIN:N_KV_ROWS], ((0, 0), (WINDOW, 0)))
    o_nsa = [_nsa_prompt(main, kvtb, kwtp, kct[n], vc[n], n) for n in range(nkv)]
    return o_hg, o_nsa, s_fin


def _mix_sample(main, kvt, lb, gnorm, state, cache, win, page_table, cmp_w, *, tn):
    nb = state.shape[0]
    o_hg, s_new = _hgrn_sample(main, lb, gnorm, state, tn=tn)
    g_n, nkv, hd = NSA_GROUP, NSA_KV_HEADS, HEAD_DIM
    q = (main[:, C_Q:C_Q + NSA_HEADS * hd] * Q_SCALE).astype(BF16)
    q = q.reshape(nb, tn, nkv, g_n, hd).transpose(0, 2, 3, 1, 4)
    eye = jnp.eye(nkv, dtype=BF16)
    qrows = (q[:, :, :, :, None, :] * eye[None, :, None, None, :, None]).reshape(nb, nkv * g_n * tn, nkv * hd)
    gates = main[:, C_NGATE:C_NGATE + N_NGATE].reshape(nb, tn, NSA_HEADS, 3).transpose(0, 2, 1, 3)
    gates = jnp.pad(gates.reshape(nb, NSA_HEADS * tn, 3), ((0, 0), (0, 0), (0, LANE - 3)))
    newt = kvt.reshape(N_KV_ROWS, nb, tn).transpose(1, 0, 2)
    o, wout = _nsa_sample(page_table, cache, qrows, newt, win, gates, *cmp_w, tn=tn)
    o = o.reshape(nb, nkv, g_n, tn, hd).transpose(0, 3, 1, 2, 4).reshape(nb * tn, nkv, g_n * hd)
    return o_hg, [o[:, n] for n in range(nkv)], s_new, wout


def kernel(x_prompt, x_sample, cache_kv, cache_win, state_hgrn, page_table, norm_pre1, norm_post1, ff1_gate, ff1_up, ff1_down, norm_pre2, norm_post2, w_in, hg_lb, hg_gnorm, cmp_pos_k, cmp_w1_k, cmp_w2_k, cmp_pos_v, cmp_w1_v, cmp_w2_v, w_proj_hg, w_proj_nsa, w_out, norm_pre3, norm_post3, ff2_gate, ff2_up, ff2_down):
    depth = norm_pre1.shape[0]
    bp, tp, d = x_prompt.shape
    nb, tn, _ = x_sample.shape
    assert bp == 1
    nkv, hd = NSA_KV_HEADS, HEAD_DIM
    lb_all = jnp.cumsum(jax.nn.softmax(hg_lb.astype(F32), axis=0), axis=0)[:depth]
    xp = x_prompt.reshape(tp, d)
    xs = x_sample.reshape(nb * tn, d)
    kv_p, kv_s, win_p, win_s, st_p, st_s = [], [], [], [], [], []
    row = lambda a: a.reshape(1, -1)
    for l in range(depth):
        ff1 = _prep_ffn(ff1_gate[l], ff1_up[l], ff1_down[l])
        ff2 = _prep_ffn(ff2_gate[l], ff2_up[l], ff2_down[l])
        w_in_l = _prep_w_in(w_in[l])
        wh, wn, wo = w_proj_hg[l].astype(BF16), w_proj_nsa[l].astype(BF16), w_out[l].astype(BF16)
        cmp_p = _prep_cmp_prompt(cmp_pos_k[l], cmp_w1_k[l], cmp_w2_k[l], cmp_pos_v[l], cmp_w1_v[l], cmp_w2_v[l])
        cmp_s = (_prep_cmp_sample(cmp_pos_k[l], cmp_w1_k[l], cmp_w2_k[l]),
                 _prep_cmp_sample(cmp_pos_v[l], cmp_w1_v[l], cmp_w2_v[l]))
        lb, gnorm = row(lb_all[l]), row(hg_gnorm[l])
        n_pool, page = cache_kv.shape[1], cache_kv.shape[2]
        cache = cache_kv[l].transpose(0, 2, 3, 4, 1).reshape(n_pool, N_KINDS, KVW, page)
        wl = cache_win.shape[2]
        win = cache_win[l].transpose(0, 2, 3, 4, 1).reshape(nb, 2, KVW, wl)

        xp = _ffn(xp, row(norm_pre1[l]), row(norm_post1[l]), *ff1)
        xs = _ffn(xs, row(norm_pre1[l]), row(norm_post1[l]), *ff1)

        main_p, kvt_p, kvtb_p = _project(xp, row(norm_pre2[l]), *w_in_l)
        main_s, kvt_s, _ = _project(xs, row(norm_pre2[l]), *w_in_l)
        ohg_p, onsa_p, s_p = _mix_prompt(main_p, kvtb_p, lb, gnorm, cmp_p)
        ohg_s, onsa_s, s_s, wout = _mix_sample(main_s, kvt_s, lb, gnorm, state_hgrn[l], cache, win,
                                               page_table, cmp_s, tn=tn)
        xp = _outproj(_merge(ohg_p, *onsa_p, main_p, wh, wn), xp, wo, row(norm_post2[l]))
        xs = _outproj(_merge(ohg_s, *onsa_s, main_s, wh, wn), xs, wo, row(norm_post2[l]))

        kv_p.append(kvt_p[:N_KINDS * KVW].reshape(N_KINDS, nkv, hd, bp, tp).transpose(3, 4, 0, 1, 2))
        kv_s.append(kvt_s[:N_KINDS * KVW].reshape(N_KINDS, nkv, hd, nb, tn).transpose(3, 4, 0, 1, 2))
        wp = min(WINDOW, tp)
        win_p.append(kvt_p[R_KWIN:, tp - wp:].reshape(2, nkv, hd, bp, wp).transpose(3, 4, 0, 1, 2))
        win_s.append(wout.reshape(nb, 2, nkv, hd, wl).transpose(0, 4, 1, 2, 3))
        st_p.append(s_p[None])
        st_s.append(s_s)

        xp = _ffn(xp, row(norm_pre3[l]), row(norm_post3[l]), *ff2)
        xs = _ffn(xs, row(norm_pre3[l]), row(norm_post3[l]), *ff2)
    return (xp.reshape(bp, tp, d), xs.reshape(nb, tn, d), jnp.stack(kv_p), jnp.stack(kv_s),
            jnp.stack(win_p), jnp.stack(win_s), jnp.stack(st_p), jnp.stack(st_s))
```
